```python
import jax, jax.numpy as jnp
from jax import lax
import numpy as np

D_MODEL = 2048
BATCH = 16
SEQ = 256
DEPTH = 2
DEC_BATCH = 4
DEC_SEQ = 4096
PAST_LEN = 256

GRID_W = 64
N_EVEN = (DEPTH + 1) // 2
N_ODD = DEPTH // 2
EPS = 1e-6
NEG_INF = -1e30
FNET_GROUPS = 8
FNET_GROUP_DIM = D_MODEL // 2 // FNET_GROUPS
FNET_WIDTH = FNET_GROUPS * FNET_GROUP_DIM
NA_HEADS = 8
NA_HEAD_DIM = D_MODEL // 2 // NA_HEADS
NA_WIDTH = NA_HEADS * NA_HEAD_DIM
NA_KH_MAX = 8
NA_KW = 16
ATTN_BLOCK = 128
EVEN_IN = FNET_WIDTH + 3 * NA_WIDTH
EVEN_MIX = FNET_WIDTH + NA_WIDTH
RET_HEADS = 8
RET_QK_DIM = D_MODEL // RET_HEADS
RET_V_DIM = 2 * D_MODEL // RET_HEADS
RET_QK_WIDTH = RET_HEADS * RET_QK_DIM
RET_V_WIDTH = RET_HEADS * RET_V_DIM
ODD_IN = 2 * RET_QK_WIDTH + 3 * RET_V_WIDTH
RET_CHUNK = 128
ROPE_BASE = 10000.0
D_FF = ((8 * D_MODEL + 3 * 256 - 1) // (3 * 256)) * 256

kernel_name = "hybrid_fnet_natten_retention_dit_step"


def rms_norm(x, g):
    xf = x.astype(jnp.float32)
    y = xf * lax.rsqrt(jnp.mean(xf * xf, axis=-1, keepdims=True) + EPS)
    return (y * g.astype(jnp.float32)).astype(x.dtype)


def adaln(cond, w, b):
    m = jax.nn.silu(cond) @ w + b
    return m.reshape(cond.shape[0], 6, D_MODEL)


def modulate(x, g, shift, scale):
    return rms_norm(x, g) * (1 + scale[:, None, :]) + shift[:, None, :]


def split_heads(a, n_heads):
    b, t, _ = a.shape
    return a.reshape(b, t, n_heads, -1).transpose(0, 2, 1, 3)


def merge_heads(a):
    b, h, t, d = a.shape
    return a.transpose(0, 2, 1, 3).reshape(b, t, h * d)


def fourier_mix(u):
    b, t, _ = u.shape
    ug = u.astype(jnp.float32).reshape(b, t, FNET_GROUPS, FNET_GROUP_DIM)
    f = jnp.fft.fft2(ug, axes=(1, 3), norm="ortho").real
    return f.reshape(b, t, FNET_WIDTH).astype(u.dtype)


def dense_attention(q, k, v):
    b, h, n, d = q.shape
    nb = n // ATTN_BLOCK
    qb = q.reshape(b, h, nb, ATTN_BLOCK, d).transpose(2, 0, 1, 3, 4)
    scale = d ** -0.5

    def blk(qi):
        s = jnp.einsum('bhqd,bhkd->bhqk', qi, k).astype(jnp.float32) * scale
        p = jax.nn.softmax(s, axis=-1)
        return jnp.einsum('bhqk,bhkd->bhqd', p.astype(v.dtype), v)

    o = lax.map(blk, qb)
    return o.transpose(1, 2, 0, 3, 4).reshape(b, h, n, d)


def neighbourhood_attention(q, k, v, kc, vc, rel_bias):
    b, h, t, d = q.shape
    rows = t // GRID_W
    kh = min(NA_KH_MAX, rows)
    r = jnp.arange(rows)
    row_start = jnp.clip(r - kh // 2, 0, rows - kh)
    key_rows = row_start[:, None] + jnp.arange(kh)[None, :]
    col = jnp.arange(GRID_W)
    col_start = jnp.clip(col - NA_KW // 2, 0, GRID_W - NA_KW)
    col_ok = (col[None, :] >= col_start[:, None]) & (col[None, :] < col_start[:, None] + NA_KW)
    qg = q.reshape(b, h, rows, GRID_W, d)
    kg = k.reshape(b, h, rows, GRID_W, d)[:, :, key_rows]
    vg = v.reshape(b, h, rows, GRID_W, d)[:, :, key_rows]
    scale = d ** -0.5
    s_win = jnp.einsum('bhrqd,bhrkwd->bhrqkw', qg, kg).astype(jnp.float32) * scale
    dr = key_rows - r[:, None] + (NA_KH_MAX - 1)
    dc = jnp.clip(col[None, :] - col[:, None] + (NA_KW - 1), 0, 2 * NA_KW - 2)
    bias = rel_bias[:, dr[:, None, :, None], dc[None, :, None, :]]
    s_win = jnp.where(col_ok[:, None, :], s_win + bias.astype(jnp.float32)[None], NEG_INF)
    s_ctx = jnp.einsum('bhrqd,bhmd->bhrqm', qg, kc).astype(jnp.float32) * scale
    nw = kh * GRID_W
    s = jnp.concatenate([s_win.reshape(b, h, rows, GRID_W, nw), s_ctx], axis=-1)
    p = jax.nn.softmax(s, axis=-1).astype(v.dtype)
    p_win = p[..., :nw].reshape(b, h, rows, GRID_W, kh, GRID_W)
    o = (jnp.einsum('bhrqkw,bhrkwd->bhrqd', p_win, vg)
         + jnp.einsum('bhrqm,bhmd->bhrqd', p[..., nw:], vc))
    return o.reshape(b, h, t, d)


def axial_rope(x):
    t_len, dk = x.shape[2], x.shape[3]
    half = dk // 2
    nf = half // 2
    inv = ROPE_BASE ** (-jnp.arange(nf, dtype=jnp.float32) / nf)
    t = jnp.arange(t_len)
    rows = (t // GRID_W).astype(jnp.float32)
    cols = (t % GRID_W).astype(jnp.float32)

    def rot(xa, pos):
        ang = pos[:, None] * inv[None, :]
        cos, sin = jnp.cos(ang), jnp.sin(ang)
        x1, x2 = xa[..., :nf], xa[..., nf:]
        return jnp.concatenate([x1 * cos - x2 * sin, x1 * sin + x2 * cos], axis=-1)

    return jnp.concatenate([rot(x[..., :half], rows), rot(x[..., half:], cols)], axis=-1)


def retention_scan(q, k, v, log_gamma, s0):
    b, h, t, dk = q.shape
    dv = v.shape[-1]
    c = RET_CHUNK
    n = t // c

    def chunks(a):
        return a.reshape(b, h, n, c, a.shape[-1]).transpose(2, 0, 1, 3, 4)

    idx = jnp.arange(c, dtype=jnp.float32)
    diff = idx[:, None] - idx[None, :]
    decay_mask = jnp.where(diff >= 0, jnp.exp(log_gamma[:, None, None] * jnp.maximum(diff, 0.0)), 0.0)
    q_decay = jnp.exp(log_gamma[:, None] * (idx + 1.0))[..., None]
    k_decay = jnp.exp(log_gamma[:, None] * (c - 1.0 - idx))[..., None]
    chunk_decay = jnp.exp(log_gamma * c)[:, None, None]

    def step(s, inp):
        qc, kc, vc = inp
        inner = jnp.einsum('bhij,bhje->bhie', jnp.einsum('bhid,bhjd->bhij', qc, kc) * decay_mask, vc)
        cross = jnp.einsum('bhid,bhde->bhie', qc, s) * q_decay
        s_new = s * chunk_decay + jnp.einsum('bhjd,bhje->bhde', kc * k_decay, vc)
        return s_new, inner + cross

    s_fin, out = lax.scan(step, s0, (chunks(q), chunks(k), chunks(v)))
    return out.transpose(1, 2, 0, 3, 4).reshape(b, h, t, dv), s_fin


def head_group_norm(o):
    mu = jnp.mean(o, axis=-1, keepdims=True)
    oc = o - mu
    return oc * lax.rsqrt(jnp.mean(oc * oc, axis=-1, keepdims=True) + EPS)


def retention_mixer(h, w_in, decay_logit, w_out, s0_f, s0_b, latent):
    p = h @ w_in
    q, k, v, g_f, g_b = jnp.split(
        p, [RET_QK_WIDTH, 2 * RET_QK_WIDTH, 2 * RET_QK_WIDTH + RET_V_WIDTH, 2 * RET_QK_WIDTH + 2 * RET_V_WIDTH], axis=-1)
    q = split_heads(q, RET_HEADS).astype(jnp.float32)
    k = split_heads(k, RET_HEADS).astype(jnp.float32)
    v = split_heads(v, RET_HEADS).astype(jnp.float32)
    if latent:
        q, k = axial_rope(q), axial_rope(k)
    k = k * (RET_QK_DIM ** -0.5)
    log_gamma = jax.nn.log_sigmoid(decay_logit.astype(jnp.float32))
    o_f, s_f = retention_scan(q, k, v, log_gamma[0], s0_f.astype(jnp.float32))
    o_b, s_b = retention_scan(jnp.flip(q, 2), jnp.flip(k, 2), jnp.flip(v, 2), log_gamma[1], s0_b.astype(jnp.float32))
    o_b = jnp.flip(o_b, 2)
    y = (jax.nn.silu(g_f.astype(jnp.float32)) * merge_heads(head_group_norm(o_f))
         + jax.nn.silu(g_b.astype(jnp.float32)) * merge_heads(head_group_norm(o_b)))
    return y.astype(h.dtype) @ w_out, s_f, s_b


def swiglu(h, w_gate, w_up, w_down):
    return (jax.nn.silu(h @ w_gate) * (h @ w_up)) @ w_down


def trunk(x, cond, ctx_k, ctx_v, ctx_state, ada_w, ada_b, norm_g, even_w_in, even_q_norm, even_k_norm,
          na_rel_bias, even_w_out, odd_w_in, ret_decay_logit, odd_w_out, ffn_w_gate, ffn_w_up, ffn_w_down):
    is_ctx = ctx_k is None
    new_k, new_v, new_s = [], [], []
    bsz = x.shape[0]
    for i in range(DEPTH):
        mod = adaln(cond, ada_w[i], ada_b[i])
        h = modulate(x, norm_g[i, 0], mod[:, 0], mod[:, 1])
        j = i // 2
        if i % 2 == 0:
            p = h @ even_w_in[j]
            u = p[..., :FNET_WIDTH]
            q, k, v = jnp.split(p[..., FNET_WIDTH:], 3, axis=-1)
            q = rms_norm(split_heads(q, NA_HEADS), even_q_norm[j])
            k = rms_norm(split_heads(k, NA_HEADS), even_k_norm[j])
            v = split_heads(v, NA_HEADS)
            if is_ctx:
                o = dense_attention(q, k, v)
                new_k.append(k)
                new_v.append(v)
            else:
                o = neighbourhood_attention(q, k, v, ctx_k[:, j].astype(q.dtype), ctx_v[:, j].astype(v.dtype), na_rel_bias[j])
            y = jnp.concatenate([fourier_mix(u), merge_heads(o)], axis=-1) @ even_w_out[j]
        else:
            if is_ctx:
                s0_f = jnp.zeros((bsz, RET_HEADS, RET_QK_DIM, RET_V_DIM), jnp.float32)
                s0_b = s0_f
            else:
                s0_f, s0_b = ctx_state[:, j, 0], ctx_state[:, j, 1]
            y, s_f, s_b = retention_mixer(h, odd_w_in[j], ret_decay_logit[j], odd_w_out[j], s0_f, s0_b, not is_ctx)
            if is_ctx:
                new_s.append(jnp.stack([s_f, s_b], axis=1))
        x = x + mod[:, 2][:, None, :] * y
        h = modulate(x, norm_g[i, 1], mod[:, 3], mod[:, 4])
        x = x + mod[:, 5][:, None, :] * swiglu(h, ffn_w_gate[i], ffn_w_up[i], ffn_w_down[i])
    return x, new_k, new_v, new_s


def setup_inputs(seed: int = 0) -> dict:
    key = jax.random.key(seed)
    ks = jax.random.split(key, 24)
    f32 = jnp.float32

    def nrm(k, shape, s):
        return jax.random.normal(k, shape, f32) * s

    hidx = jnp.arange(RET_HEADS, dtype=f32)
    gamma0 = 1.0 - jnp.exp2(-5.0 - hidx)
    logit0 = jnp.log(gamma0) - (-5.0 - hidx) * jnp.log(2.0)
    return {
        "x_prompt": nrm(ks[0], (BATCH, SEQ, D_MODEL), 1.0),
        "x_sample": nrm(ks[1], (DEC_BATCH, DEC_SEQ, D_MODEL), 1.0),
        "cache_k": nrm(ks[2], (DEC_BATCH, N_EVEN, NA_HEADS, PAST_LEN, NA_HEAD_DIM), 1.0),
        "cache_v": nrm(ks[3], (DEC_BATCH, N_EVEN, NA_HEADS, PAST_LEN, NA_HEAD_DIM), 1.0),
        "state_ret": nrm(ks[4], (DEC_BATCH, N_ODD, 2, RET_HEADS, RET_QK_DIM, RET_V_DIM), 0.1),
        "c": nrm(ks[5], (DEC_BATCH, D_MODEL), 1.0),
        "c_ctx": nrm(ks[6], (D_MODEL,), 1.0),
        "ada_w": nrm(ks[7], (DEPTH, D_MODEL, 6 * D_MODEL), 0.5 * D_MODEL ** -0.5),
        "ada_b": nrm(ks[8], (DEPTH, 6 * D_MODEL), 0.02),
        "norm_g": 1.0 + nrm(ks[9], (DEPTH, 2, D_MODEL), 0.02),
        "even_w_in": nrm(ks[10], (N_EVEN, D_MODEL, EVEN_IN), D_MODEL ** -0.5),
        "even_q_norm": 1.0 + nrm(ks[11], (N_EVEN, NA_HEAD_DIM), 0.02),
        "even_k_norm": 1.0 + nrm(ks[12], (N_EVEN, NA_HEAD_DIM), 0.02),
        "na_rel_bias": nrm(ks[13], (N_EVEN, NA_HEADS, 2 * NA_KH_MAX - 1, 2 * NA_KW - 1), 0.1),
        "even_w_out": nrm(ks[14], (N_EVEN, EVEN_MIX, D_MODEL), EVEN_MIX ** -0.5),
        "odd_w_in": nrm(ks[15], (N_ODD, D_MODEL, ODD_IN), D_MODEL ** -0.5),
        "ret_decay_logit": logit0[None, None, :] + nrm(ks[16], (N_ODD, 2, RET_HEADS), 0.1),
        "odd_w_out": nrm(ks[17], (N_ODD, RET_V_WIDTH, D_MODEL), RET_V_WIDTH ** -0.5),
        "ffn_w_gate": nrm(ks[18], (DEPTH, D_MODEL, D_FF), D_MODEL ** -0.5),
        "ffn_w_up": nrm(ks[19], (DEPTH, D_MODEL, D_FF), D_MODEL ** -0.5),
        "ffn_w_down": nrm(ks[20], (DEPTH, D_FF, D_MODEL), D_FF ** -0.5),
    }


def reference(x_prompt, x_sample, cache_k, cache_v, state_ret, c, c_ctx, ada_w, ada_b, norm_g,
              even_w_in, even_q_norm, even_k_norm, na_rel_bias, even_w_out, odd_w_in, ret_decay_logit,
              odd_w_out, ffn_w_gate, ffn_w_up, ffn_w_down):
    y_prompt, ks_new, vs_new, ss_new = trunk(
        x_prompt, c_ctx[None, :], None, None, None, ada_w, ada_b, norm_g, even_w_in, even_q_norm,
        even_k_norm, na_rel_bias, even_w_out, odd_w_in, ret_decay_logit, odd_w_out,
        ffn_w_gate, ffn_w_up, ffn_w_down)
    new_cache_k = jnp.stack(ks_new, axis=1)
    new_cache_v = jnp.stack(vs_new, axis=1)
    new_state_ret = jnp.stack(ss_new, axis=1)
    y_sample, _, _, _ = trunk(
        x_sample, c, cache_k, cache_v, state_ret, ada_w, ada_b, norm_g, even_w_in, even_q_norm,
        even_k_norm, na_rel_bias, even_w_out, odd_w_in, ret_decay_logit, odd_w_out,
        ffn_w_gate, ffn_w_up, ffn_w_down)
    return (y_prompt, y_sample, new_cache_k, new_cache_v, new_state_ret)
```

```python
import functools
import math

import numpy as np
import jax
import jax.numpy as jnp
from jax import lax
from jax.experimental import pallas as pl
from jax.experimental.pallas import tpu as pltpu

F32 = jnp.float32
BF16 = jnp.bfloat16

D_MODEL = 2048
DEPTH = 2
GRID_W = 64
EPS = 1e-6
NEG_INF = -1e30
FNET_GROUPS = 8
FNET_GROUP_DIM = 128
FNET_WIDTH = 1024
NA_HEADS = 8
NA_HEAD_DIM = 128
NA_WIDTH = 1024
NA_KH = 8
NA_KW = 16
EVEN_IN = FNET_WIDTH + 3 * NA_WIDTH
RET_HEADS = 8
RET_QK_DIM = 256
RET_V_DIM = 512
RET_QK_WIDTH = RET_HEADS * RET_QK_DIM
RET_V_WIDTH = RET_HEADS * RET_V_DIM
ODD_IN = 2 * RET_QK_WIDTH + 3 * RET_V_WIDTH
RET_BLOCK = 256
ROPE_BASE = 10000.0
D_FF = 5632
MOD_ROWS = 8
MOD_CHUNK = 128

V7X_VMEM_BUDGET = 56 * 1024 * 1024
VMEM_SLACK = 8 * 1024 * 1024


def _params(semantics, vmem_bytes):
    return pltpu.CompilerParams(dimension_semantics=semantics,
                                vmem_limit_bytes=min(int(vmem_bytes), V7X_VMEM_BUDGET))


def _silu(x):
    return x * (1.0 / (1.0 + jnp.exp(-x)))


def _adaln_kernel(c_ref, w_ref, b_ref, o_ref):
    s = _silu(c_ref[...]).astype(BF16)
    w = w_ref[0].astype(BF16)
    o_ref[0] = jnp.dot(s, w, preferred_element_type=F32) + b_ref[0]


def adaln_all(cond, ada_w, ada_b):
    n = ada_w.shape[-1]
    tn = 1024
    return pl.pallas_call(
        _adaln_kernel,
        out_shape=jax.ShapeDtypeStruct((DEPTH, MOD_ROWS, n), F32),
        grid=(DEPTH, n // tn),
        in_specs=[
            pl.BlockSpec((MOD_ROWS, D_MODEL), lambda l, j: (0, 0)),
            pl.BlockSpec((1, D_MODEL, tn), lambda l, j: (l, 0, j)),
            pl.BlockSpec((1, 1, tn), lambda l, j: (l, 0, j)),
        ],
        out_specs=pl.BlockSpec((1, MOD_ROWS, tn), lambda l, j: (l, 0, j)),
        compiler_params=_params(("arbitrary", "arbitrary"), 40 << 20),
        name="adaln",
    )(cond, ada_w, ada_b.reshape(DEPTH, 1, n))


def _modulate_into(h_ref, x_ref, g_ref, sh_ref, sc_ref):
    g = g_ref[...]
    shift = sh_ref[0]
    scale1 = 1.0 + sc_ref[0]

    def body(c, carry):
        rows = pl.ds(pl.multiple_of(c * MOD_CHUNK, MOD_CHUNK), MOD_CHUNK)
        x = x_ref[rows, :]
        ms = jnp.mean(x * x, axis=-1, keepdims=True)
        h_ref[rows, :] = (x * lax.rsqrt(ms + EPS) * g * scale1 + shift).astype(h_ref.dtype)
        return carry

    lax.fori_loop(0, x_ref.shape[0] // MOD_CHUNK, body, 0)


def _modproj_kernel(x_ref, sh_ref, sc_ref, g_ref, w_ref, o_ref, h_ref):
    @pl.when(pl.program_id(1) == 0)
    def _():
        _modulate_into(h_ref, x_ref, g_ref, sh_ref, sc_ref)

    o_ref[...] = jnp.dot(h_ref[...], w_ref[...], preferred_element_type=F32).astype(o_ref.dtype)


def modproj(x, shift, scale, g, w, rows_per_mod, out_dtype, tm, tn):
    m, d = x.shape
    n = w.shape[1]
    osz = jnp.dtype(out_dtype).itemsize
    vmem = 2 * tm * d * 4 + 2 * d * tn * 2 + 2 * tm * tn * osz + tm * d * 2 + tm * tn * 4 + VMEM_SLACK
    mod_idx = lambda i, j: ((i * tm) // rows_per_mod, 0, 0)
    return pl.pallas_call(
        _modproj_kernel,
        out_shape=jax.ShapeDtypeStruct((m, n), out_dtype),
        grid=(m // tm, n // tn),
        in_specs=[
            pl.BlockSpec((tm, d), lambda i, j: (i, 0)),
            pl.BlockSpec((1, 1, d), mod_idx),
            pl.BlockSpec((1, 1, d), mod_idx),
            pl.BlockSpec((1, d), lambda i, j: (0, 0)),
            pl.BlockSpec((d, tn), lambda i, j: (0, j)),
        ],
        out_specs=pl.BlockSpec((tm, tn), lambda i, j: (i, j)),
        scratch_shapes=[pltpu.VMEM((tm, d), BF16)],
        compiler_params=_params(("arbitrary", "arbitrary"), vmem),
        name="modproj",
    )(x, shift, scale, g, w)


def _head_rms(x, g):
    return x * lax.rsqrt(jnp.mean(x * x, axis=-1, keepdims=True) + EPS) * g


def _ctx_attn_kernel(q_ref, k_ref, v_ref, qg_ref, kg_ref, o_ref, nk_ref, nv_ref):
    scale = NA_HEAD_DIM ** -0.5
    for h in range(NA_HEADS):
        cs = slice(h * NA_HEAD_DIM, (h + 1) * NA_HEAD_DIM)
        q = _head_rms(q_ref[0, :, cs].astype(F32), qg_ref[...])
        k = _head_rms(k_ref[0, :, cs].astype(F32), kg_ref[...])
        v = v_ref[0, :, cs].astype(F32)
        nk_ref[0, 0, h] = k
        nv_ref[0, 0, h] = v
        s = lax.dot_general(q.astype(BF16), k.astype(BF16), (((1,), (1,)), ((), ())),
                            preferred_element_type=F32) * scale
        m = jnp.max(s, axis=-1, keepdims=True)
        e = jnp.exp(s - m)
        l = jnp.sum(e, axis=-1, keepdims=True)
        o = jnp.dot(e.astype(BF16), v.astype(BF16), preferred_element_type=F32) / l
        o_ref[0, :, cs] = o.astype(o_ref.dtype)


def ctx_attention(p, q_gain, k_gain):
    b, t, _ = p.shape
    cache_shape = (b, 1, NA_HEADS, t, NA_HEAD_DIM)
    blk = lambda c: pl.BlockSpec((1, t, NA_WIDTH), lambda i, c=c: (i, 0, c))
    gain = pl.BlockSpec((1, NA_HEAD_DIM), lambda i: (0, 0))
    cache_spec = pl.BlockSpec((1, 1, NA_HEADS, t, NA_HEAD_DIM), lambda i: (i, 0, 0, 0, 0))
    return pl.pallas_call(
        _ctx_attn_kernel,
        out_shape=(jax.ShapeDtypeStruct((b, t, NA_WIDTH), BF16),
                   jax.ShapeDtypeStruct(cache_shape, F32),
                   jax.ShapeDtypeStruct(cache_shape, F32)),
        grid=(b,),
        in_specs=[blk(1), blk(2), blk(3), gain, gain],
        out_specs=(pl.BlockSpec((1, t, NA_WIDTH), lambda i: (i, 0, 0)), cache_spec, cache_spec),
        compiler_params=_params(("arbitrary",), 32 << 20),
        name="ctx_attention",
    )(p, p, p, q_gain, k_gain)


def _na_bias_table(rel_bias):
    col = jnp.arange(GRID_W)
    col_start = jnp.clip(col - NA_KW // 2, 0, GRID_W - NA_KW)
    col_ok = (col[None, :] >= col_start[:, None]) & (col[None, :] < col_start[:, None] + NA_KW)
    dc = jnp.clip(col[None, :] - col[:, None] + (NA_KW - 1), 0, 2 * NA_KW - 2)
    off = jnp.arange(NA_KH)
    dr = jnp.arange(NA_KH)[None, :] - off[:, None] + (NA_KH - 1)
    tab = rel_bias[:, dr[:, None, :, None], dc[None, :, None, :]]
    tab = jnp.where(col_ok[None, None, :, None, :], tab.astype(F32), NEG_INF)
    return tab.reshape(rel_bias.shape[0], NA_KH, GRID_W, NA_KH * GRID_W)


def _na_kernel(q_ref, k_ref, v_ref, kc_ref, vc_ref, bias_ref, qg_ref, kg_ref, o_ref,
               qs_ref, ks_ref, vs_ref, *, rows):
    scale = NA_HEAD_DIM ** -0.5
    qs_ref[...] = (_head_rms(q_ref[0].astype(F32), qg_ref[...]) * scale).astype(BF16)
    ks_ref[...] = _head_rms(k_ref[0].astype(F32), kg_ref[...]).astype(BF16)
    vs_ref[...] = v_ref[0].astype(BF16)
    kc = kc_ref[0, 0, 0].astype(BF16)
    vc = vc_ref[0, 0, 0].astype(BF16)
    nt = (((1,), (1,)), ((), ()))

    def body(r, carry):
        rs = jnp.clip(r - NA_KH // 2, 0, rows - NA_KH)
        q = qs_ref[pl.ds(pl.multiple_of(r * GRID_W, GRID_W), GRID_W), :]
        kw = ks_ref[pl.ds(pl.multiple_of(rs * GRID_W, GRID_W), NA_KH * GRID_W), :]
        vw = vs_ref[pl.ds(pl.multiple_of(rs * GRID_W, GRID_W), NA_KH * GRID_W), :]
        s_w = lax.dot_general(q, kw, nt, preferred_element_type=F32) + bias_ref[0, r - rs]
        s_c = lax.dot_general(q, kc, nt, preferred_element_type=F32)
        m = jnp.maximum(jnp.max(s_w, axis=-1, keepdims=True), jnp.max(s_c, axis=-1, keepdims=True))
        e_w = jnp.exp(s_w - m)
        e_c = jnp.exp(s_c - m)
        l = jnp.sum(e_w, axis=-1, keepdims=True) + jnp.sum(e_c, axis=-1, keepdims=True)
        o = (jnp.dot(e_w.astype(BF16), vw, preferred_element_type=F32)
             + jnp.dot(e_c.astype(BF16), vc, preferred_element_type=F32)) / l
        o_ref[0, pl.ds(pl.multiple_of(r * GRID_W, GRID_W), GRID_W), :] = o.astype(o_ref.dtype)
        return carry

    lax.fori_loop(0, rows, body, 0)


def na_attention(p, cache_k, cache_v, layer_j, bias_tab, q_gain, k_gain):
    b, t, _ = p.shape
    past = cache_k.shape[3]
    rows = t // GRID_W
    hd = NA_HEAD_DIM
    col0 = FNET_WIDTH // hd
    blk = lambda c: pl.BlockSpec((1, t, hd), lambda i, h, c=c: (i, 0, col0 + c * NA_HEADS + h))
    cache_spec = pl.BlockSpec((1, 1, 1, past, hd), lambda i, h: (i, layer_j, h, 0, 0))
    gain = pl.BlockSpec((1, hd), lambda i, h: (0, 0))
    return pl.pallas_call(
        functools.partial(_na_kernel, rows=rows),
        out_shape=jax.ShapeDtypeStruct((b, t, NA_WIDTH), BF16),
        grid=(b, NA_HEADS),
        in_specs=[blk(0), blk(1), blk(2), cache_spec, cache_spec,
                  pl.BlockSpec((1, NA_KH, GRID_W, NA_KH * GRID_W), lambda i, h: (h, 0, 0, 0)),
                  gain, gain],
        out_specs=pl.BlockSpec((1, t, hd), lambda i, h: (i, 0, h)),
        scratch_shapes=[pltpu.VMEM((t, hd), BF16)] * 3,
        compiler_params=_params(("arbitrary", "arbitrary"), 40 << 20),
        name="na_attention",
    )(p, p, p, cache_k, cache_v, bias_tab, q_gain, k_gain)


def _dft_tables(n, scale):
    k = np.arange(n, dtype=np.int64)
    ang = (2.0 * np.pi / n) * ((k[:, None] * k[None, :]) % n).astype(np.float64)
    return (np.cos(ang) * scale).astype(np.float32), (np.sin(ang) * scale).astype(np.float32)


def _dft_tables_device(n, scale):
    k = jnp.arange(n, dtype=jnp.int32)
    ang = ((k[:, None] * k[None, :]) % n).astype(F32) * (2.0 * math.pi / n)
    return (jnp.cos(ang) * scale).astype(BF16), (jnp.sin(ang) * scale).astype(BF16)


def _fnet_kernel(ct_ref, st_ref, u_ref, cc_ref, sc_ref, o_ref):
    u = u_ref[0].astype(BF16)
    a = jnp.dot(ct_ref[...], u, preferred_element_type=F32).astype(BF16)
    b = jnp.dot(st_ref[...], u, preferred_element_type=F32).astype(BF16)
    cc = cc_ref[...]
    sc = sc_ref[...]
    for g in range(FNET_GROUPS):
        cs = slice(g * FNET_GROUP_DIM, (g + 1) * FNET_GROUP_DIM)
        y = (jnp.dot(a[:, cs], cc, preferred_element_type=F32)
             - jnp.dot(b[:, cs], sc, preferred_element_type=F32))
        o_ref[0, :, cs] = y.astype(o_ref.dtype)


def fourier_mix(p, tm):
    b, t, _ = p.shape
    ct, st = _dft_tables_device(t, 1.0 / math.sqrt(t))
    cc_np, sc_np = _dft_tables(FNET_GROUP_DIM, 1.0 / math.sqrt(FNET_GROUP_DIM))
    cc = jnp.asarray(cc_np, dtype=BF16)
    sc = jnp.asarray(sc_np, dtype=BF16)
    isz = p.dtype.itemsize
    vmem = 2 * 2 * tm * t * 2 + 2 * t * FNET_WIDTH * isz + t * FNET_WIDTH * 2 + 8 * tm * FNET_WIDTH * 4 + VMEM_SLACK
    tab = pl.BlockSpec((tm, t), lambda i, r: (r, 0))
    small = pl.BlockSpec((FNET_GROUP_DIM, FNET_GROUP_DIM), lambda i, r: (0, 0))
    return pl.pallas_call(
        _fnet_kernel,
        out_shape=jax.ShapeDtypeStruct((b, t, FNET_WIDTH), BF16),
        grid=(b, t // tm),
        in_specs=[tab, tab, pl.BlockSpec((1, t, FNET_WIDTH), lambda i, r: (i, 0, 0)), small, small],
        out_specs=pl.BlockSpec((1, tm, FNET_WIDTH), lambda i, r: (i, r, 0)),
        compiler_params=_params(("arbitrary", "arbitrary"), vmem),
        name="fourier_mix",
    )(ct, st, p, cc, sc)


def _even_out_kernel(f_ref, a_ref, wf_ref, wa_ref, x_ref, gate_ref, o_ref):
    y = (jnp.dot(f_ref[...], wf_ref[...], preferred_element_type=F32)
         + jnp.dot(a_ref[...], wa_ref[...], preferred_element_type=F32))
    o_ref[...] = x_ref[...] + gate_ref[0] * y


def even_out(f, a, w, x, gate, rows_per_mod, tm, tn):
    m, d = x.shape
    kf = f.shape[1]
    vmem = 2 * 2 * tm * kf * 2 + 2 * 2 * kf * tn * 2 + 4 * tm * tn * 4 + tm * tn * 4 + VMEM_SLACK
    return pl.pallas_call(
        _even_out_kernel,
        out_shape=jax.ShapeDtypeStruct((m, d), F32),
        grid=(d // tn, m // tm),
        in_specs=[
            pl.BlockSpec((tm, kf), lambda j, i: (i, 0)),
            pl.BlockSpec((tm, kf), lambda j, i: (i, 0)),
            pl.BlockSpec((kf, tn), lambda j, i: (0, j)),
            pl.BlockSpec((kf, tn), lambda j, i: (1, j)),
            pl.BlockSpec((tm, tn), lambda j, i: (i, j)),
            pl.BlockSpec((1, 1, tn), lambda j, i: ((i * tm) // rows_per_mod, 0, j)),
        ],
        out_specs=pl.BlockSpec((tm, tn), lambda j, i: (i, j)),
        compiler_params=_params(("arbitrary", "arbitrary"), vmem),
        name="even_out",
    )(f, a, w, w, x, gate)


def _odd_out_kernel(yf_ref, yb_ref, w_ref, x_ref, gate_ref, o_ref):
    y = yf_ref[0] + yb_ref[0]
    o_ref[...] = x_ref[...] + gate_ref[0] * jnp.dot(y, w_ref[...], preferred_element_type=F32)


def odd_out(y2, w, x, gate, rows_per_mod, tm, tn):
    m, d = x.shape
    k = w.shape[0]
    vmem = 2 * 2 * tm * k * 2 + 2 * k * tn * 2 + 4 * tm * tn * 4 + tm * k * 2 + tm * tn * 4 + VMEM_SLACK
    return pl.pallas_call(
        _odd_out_kernel,
        out_shape=jax.ShapeDtypeStruct((m, d), F32),
        grid=(d // tn, m // tm),
        in_specs=[
            pl.BlockSpec((1, tm, k), lambda j, i: (0, i, 0)),
            pl.BlockSpec((1, tm, k), lambda j, i: (1, i, 0)),
            pl.BlockSpec((k, tn), lambda j, i: (0, j)),
            pl.BlockSpec((tm, tn), lambda j, i: (i, j)),
            pl.BlockSpec((1, 1, tn), lambda j, i: ((i * tm) // rows_per_mod, 0, j)),
        ],
        out_specs=pl.BlockSpec((tm, tn), lambda j, i: (i, j)),
        compiler_params=_params(("arbitrary", "arbitrary"), vmem),
        name="odd_out",
    )(y2, y2, w, x, gate)


def _ffn_kernel(x_ref, sh_ref, sc_ref, gate_ref, g_ref, wg_ref, wu_ref, wd_ref, o_ref, h_ref):
    j = pl.program_id(1)

    @pl.when(j == 0)
    def _():
        _modulate_into(h_ref, x_ref, g_ref, sh_ref, sc_ref)
        o_ref[...] = jnp.zeros_like(o_ref)

    h = h_ref[...]
    a = jnp.dot(h, wg_ref[...], preferred_element_type=F32)
    u = jnp.dot(h, wu_ref[...], preferred_element_type=F32)
    act = (_silu(a) * u).astype(BF16)
    o_ref[...] += jnp.dot(act, wd_ref[...], preferred_element_type=F32)

    @pl.when(j == pl.num_programs(1) - 1)
    def _():
        o_ref[...] = x_ref[...] + gate_ref[0] * o_ref[...]


def ffn_block(x, shift, scale, gate, g, w_gate, w_up, w_down, rows_per_mod, tm, tf):
    m, d = x.shape
    f = w_gate.shape[1]
    vmem = 4 * tm * d * 4 + tm * d * 2 + 2 * 3 * d * tf * 2 + 4 * tm * tf * 4 + tm * d * 4 + VMEM_SLACK
    mod_idx = lambda i, j: ((i * tm) // rows_per_mod, 0, 0)
    mod = pl.BlockSpec((1, 1, d), mod_idx)
    return pl.pallas_call(
        _ffn_kernel,
        out_shape=jax.ShapeDtypeStruct((m, d), F32),
        grid=(m // tm, f // tf),
        in_specs=[
            pl.BlockSpec((tm, d), lambda i, j: (i, 0)),
            mod, mod, mod,
            pl.BlockSpec((1, d), lambda i, j: (0, 0)),
            pl.BlockSpec((d, tf), lambda i, j: (0, j)),
            pl.BlockSpec((d, tf), lambda i, j: (0, j)),
            pl.BlockSpec((tf, d), lambda i, j: (j, 0)),
        ],
        out_specs=pl.BlockSpec((tm, d), lambda i, j: (i, 0)),
        scratch_shapes=[pltpu.VMEM((tm, d), BF16)],
        compiler_params=_params(("arbitrary", "arbitrary"), vmem),
        name="ffn_block",
    )(x, shift, scale, gate, g, w_gate, w_up, w_down)


def _rope_tables(t_len):
    half = RET_QK_DIM // 2
    nf = half // 2
    inv = ROPE_BASE ** (-jnp.arange(nf, dtype=F32) / nf)
    t = jnp.arange(t_len)
    rows = (t // GRID_W).astype(F32)
    cols = (t % GRID_W).astype(F32)

    def tabs(pos):
        ang = pos[:, None] * inv[None, :]
        c, s = jnp.cos(ang), jnp.sin(ang)
        return jnp.concatenate([c, c], axis=-1), jnp.concatenate([-s, s], axis=-1)

    cr, sr = tabs(rows)
    cc, sc = tabs(cols)
    return jnp.concatenate([cr, cc], axis=-1), jnp.concatenate([sr, sc], axis=-1)


def _rope(x, cos, sin):
    half = RET_QK_DIM // 2
    parts = []
    for s in range(2):
        xs = x[:, s * half:(s + 1) * half]
        parts.append(pltpu.roll(xs, half // 2, axis=1))
    return x * cos + jnp.concatenate(parts, axis=-1) * sin


def _retention_kernel(lg_ref, *refs, latent, n_blocks):
    if latent:
        q_ref, k_ref, v_ref, g_ref, cos_ref, sin_ref, s0_ref, y_ref, s_ref = refs
    else:
        q_ref, k_ref, v_ref, g_ref, y_ref, sfin_ref, s_ref = refs
    h = pl.program_id(1)
    d = pl.program_id(2)
    i = pl.program_id(3)
    c = RET_BLOCK

    @pl.when(i == 0)
    def _():
        if latent:
            s_ref[...] = s0_ref[0, 0, 0, 0]
        else:
            s_ref[...] = jnp.zeros_like(s_ref)

    q = q_ref[0].astype(F32)
    k = k_ref[0].astype(F32)
    if latent:
        q = _rope(q, cos_ref[...], sin_ref[...])
        k = _rope(k, cos_ref[...], sin_ref[...])
    k = k * (RET_QK_DIM ** -0.5)
    v = v_ref[0].astype(BF16)

    lg = lg_ref[d, h]
    df = d.astype(F32)
    row = lax.broadcasted_iota(jnp.int32, (c, c), 0).astype(F32)
    col = lax.broadcasted_iota(jnp.int32, (c, c), 1).astype(F32)
    diff = (row - col) * (1.0 - 2.0 * df)
    decay = jnp.where(diff >= 0, jnp.exp(lg * jnp.maximum(diff, 0.0)), 0.0)
    t_idx = lax.broadcasted_iota(jnp.int32, (c, 1), 0).astype(F32)
    pos = t_idx + df * (c - 1.0 - 2.0 * t_idx)
    q_decay = jnp.exp(lg * (pos + 1.0))
    k_decay = jnp.exp(lg * (c - 1.0 - pos))
    chunk_decay = jnp.exp(lg * c)

    qb = q.astype(BF16)
    a = lax.dot_general(qb, k.astype(BF16), (((1,), (1,)), ((), ())), preferred_element_type=F32)
    inner = jnp.dot((a * decay).astype(BF16), v, preferred_element_type=F32)
    s_old = s_ref[...]
    cross = jnp.dot(qb, s_old.astype(BF16), preferred_element_type=F32) * q_decay
    kd = (k * k_decay).astype(BF16)
    s_new = s_old * chunk_decay + lax.dot_general(kd, v, (((0,), (0,)), ((), ())),
                                                  preferred_element_type=F32)
    s_ref[...] = s_new

    o = inner + cross
    oc = o - jnp.mean(o, axis=-1, keepdims=True)
    gn = oc * lax.rsqrt(jnp.mean(oc * oc, axis=-1, keepdims=True) + EPS)
    y_ref[0, 0] = (_silu(g_ref[0].astype(F32)) * gn).astype(y_ref.dtype)

    if not latent:
        @pl.when(i == n_blocks - 1)
        def _():
            sfin_ref[0, 0, 0, 0] = s_new


def retention(p, log_gamma, state0, layer_j):
    b, t, _ = p.shape
    latent = state0 is not None
    c = RET_BLOCK
    nb = t // c
    dk, dv, nh = RET_QK_DIM, RET_V_DIM, RET_HEADS
    tblk = lambda d, i: i + d * (nb - 1 - 2 * i)
    in_specs = [
        pl.BlockSpec((1, c, dk), lambda bi, h, d, i, lg: (bi, tblk(d, i), h)),
        pl.BlockSpec((1, c, dk), lambda bi, h, d, i, lg: (bi, tblk(d, i), nh + h)),
        pl.BlockSpec((1, c, dv), lambda bi, h, d, i, lg: (bi, tblk(d, i), nh + h)),
        pl.BlockSpec((1, c, dv), lambda bi, h, d, i, lg: (bi, tblk(d, i), 2 * nh + d * nh + h)),
    ]
    args = [p, p, p, p]
    y_shape = jax.ShapeDtypeStruct((2, b, t, RET_V_WIDTH), BF16)
    y_spec = pl.BlockSpec((1, 1, c, dv), lambda bi, h, d, i, lg: (d, bi, tblk(d, i), h))
    if latent:
        cos, sin = _rope_tables(t)
        tab = pl.BlockSpec((c, dk), lambda bi, h, d, i, lg: (tblk(d, i), 0))
        in_specs += [tab, tab,
                     pl.BlockSpec((1, 1, 1, 1, dk, dv), lambda bi, h, d, i, lg: (bi, layer_j, d, h, 0, 0))]
        args += [cos, sin, state0]
        out_shape, out_specs = y_shape, y_spec
    else:
        out_shape = (y_shape, jax.ShapeDtypeStruct((b, 1, 2, nh, dk, dv), F32))
        out_specs = (y_spec, pl.BlockSpec((1, 1, 1, 1, dk, dv), lambda bi, h, d, i, lg: (bi, 0, d, h, 0, 0)))
    return pl.pallas_call(
        functools.partial(_retention_kernel, latent=latent, n_blocks=nb),
        out_shape=out_shape,
        grid_spec=pltpu.PrefetchScalarGridSpec(
            num_scalar_prefetch=1,
            grid=(b, nh, 2, nb),
            in_specs=in_specs,
            out_specs=out_specs,
            scratch_shapes=[pltpu.VMEM((dk, dv), F32)],
        ),
        compiler_params=_params(("arbitrary",) * 4, 32 << 20),
        name="retention",
    )(log_gamma, *args)


def _trunk(x, mods, ctx_k, ctx_v, ctx_state, wts, p_dtype):
    b, t, d = x.shape
    is_ctx = ctx_k is None
    nbm = mods.shape[1]
    rows_per_mod = (b * t) // nbm
    x2 = x.reshape(b * t, d)
    new_k = new_v = new_s = None
    for i in range(DEPTH):
        mod = [mods[i, :, k][:, None, :] for k in range(6)]
        j = i // 2
        if i % 2 == 0:
            p = modproj(x2, mod[0], mod[1], wts["norm_g"][i, 0][None], wts["even_w_in"][j],
                        rows_per_mod, p_dtype, tm=1024, tn=512).reshape(b, t, EVEN_IN)
            qg, kg = wts["even_q_norm"][j][None], wts["even_k_norm"][j][None]
            if is_ctx:
                attn, new_k, new_v = ctx_attention(p, qg, kg)
            else:
                attn = na_attention(p, ctx_k, ctx_v, j, _na_bias_table(wts["na_rel_bias"][j]), qg, kg)
            fm = fourier_mix(p, tm=min(t, 256))
            x2 = even_out(fm.reshape(b * t, FNET_WIDTH), attn.reshape(b * t, NA_WIDTH),
                          wts["even_w_out"][j], x2, mod[2], rows_per_mod, tm=512, tn=1024)
        else:
            p = modproj(x2, mod[0], mod[1], wts["norm_g"][i, 0][None], wts["odd_w_in"][j],
                        rows_per_mod, p_dtype, tm=1024, tn=512).reshape(b, t, ODD_IN)
            lg = jax.nn.log_sigmoid(wts["ret_decay_logit"][j].astype(F32))
            if is_ctx:
                y2, new_s = retention(p, lg, None, j)
            else:
                y2 = retention(p, lg, ctx_state, j)
            x2 = odd_out(y2.reshape(2, b * t, RET_V_WIDTH), wts["odd_w_out"][j], x2, mod[2],
                         rows_per_mod, tm=512, tn=1024)
        x2 = ffn_block(x2, mod[3], mod[4], mod[5], wts["norm_g"][i, 1][None], wts["ffn_w_gate"][i],
                       wts["ffn_w_up"][i], wts["ffn_w_down"][i], rows_per_mod, tm=512, tf=512)
    return x2.reshape(b, t, d), new_k, new_v, new_s


def kernel(x_prompt, x_sample, cache_k, cache_v, state_ret, c, c_ctx, ada_w, ada_b, norm_g,
           even_w_in, even_q_norm, even_k_norm, na_rel_bias, even_w_out, odd_w_in, ret_decay_logit,
           odd_w_out, ffn_w_gate, ffn_w_up, ffn_w_down):
    nb_lat = c.shape[0]
    cond = jnp.concatenate(
        [c, c_ctx[None, :], jnp.zeros((MOD_ROWS - nb_lat - 1, D_MODEL), F32)], axis=0)
    mods = adaln_all(cond, ada_w, ada_b).reshape(DEPTH, MOD_ROWS, 6, D_MODEL)
    wts = dict(
        norm_g=norm_g, even_q_norm=even_q_norm, even_k_norm=even_k_norm, na_rel_bias=na_rel_bias,
        ret_decay_logit=ret_decay_logit,
        even_w_in=even_w_in.astype(BF16), even_w_out=even_w_out.astype(BF16),
        odd_w_in=odd_w_in.astype(BF16), odd_w_out=odd_w_out.astype(BF16),
        ffn_w_gate=ffn_w_gate.astype(BF16), ffn_w_up=ffn_w_up.astype(BF16),
        ffn_w_down=ffn_w_down.astype(BF16),
    )
    y_prompt, new_k, new_v, new_s = _trunk(
        x_prompt, mods[:, nb_lat:nb_lat + 1], None, None, None, wts, F32)
    y_sample, _, _, _ = _trunk(x_sample, mods[:, :nb_lat], cache_k, cache_v, state_ret, wts, BF16)
    return (y_prompt, y_sample, new_k, new_v, new_s)
```

```python
import functools
import math

import numpy as np
import jax
import jax.numpy as jnp
from jax import lax
from jax.experimental import pallas as pl
from jax.experimental.pallas import tpu as pltpu

F32 = jnp.float32
BF16 = jnp.bfloat16

D_MODEL = 2048
DEPTH = 2
GRID_W = 64
EPS = 1e-6
NEG_INF = -1e30
FNET_GROUPS = 8
FNET_GROUP_DIM = 128
FNET_WIDTH = 1024
NA_HEADS = 8
NA_HEAD_DIM = 128
NA_WIDTH = 1024
NA_KH = 8
NA_KW = 16
EVEN_IN = FNET_WIDTH + 3 * NA_WIDTH
RET_HEADS = 8
RET_QK_DIM = 256
RET_V_DIM = 512
RET_QK_WIDTH = RET_HEADS * RET_QK_DIM
RET_V_WIDTH = RET_HEADS * RET_V_DIM
ODD_IN = 2 * RET_QK_WIDTH + 3 * RET_V_WIDTH
RET_BLOCK = 256
ROPE_BASE = 10000.0
D_FF = 5632
MOD_ROWS = 8
MOD_CHUNK = 128

V7X_VMEM_BUDGET = 56 * 1024 * 1024
VMEM_SLACK = 8 * 1024 * 1024


def _params(semantics, vmem_bytes):
    return pltpu.CompilerParams(dimension_semantics=semantics,
                                vmem_limit_bytes=min(int(vmem_bytes), V7X_VMEM_BUDGET))


def _silu(x):
    return x * (1.0 / (1.0 + jnp.exp(-x)))


def _adaln_kernel(c_ref, w_ref, b_ref, o_ref):
    s = _silu(c_ref[...]).astype(BF16)
    w = w_ref[0].astype(BF16)
    o_ref[0] = jnp.dot(s, w, preferred_element_type=F32) + b_ref[0]


def adaln_all(cond, ada_w, ada_b):
    n = ada_w.shape[-1]
    tn = 1024
    return pl.pallas_call(
        _adaln_kernel,
        out_shape=jax.ShapeDtypeStruct((DEPTH, MOD_ROWS, n), F32),
        grid=(DEPTH, n // tn),
        in_specs=[
            pl.BlockSpec((MOD_ROWS, D_MODEL), lambda l, j: (0, 0)),
            pl.BlockSpec((1, D_MODEL, tn), lambda l, j: (l, 0, j)),
            pl.BlockSpec((1, 1, tn), lambda l, j: (l, 0, j)),
        ],
        out_specs=pl.BlockSpec((1, MOD_ROWS, tn), lambda l, j: (l, 0, j)),
        compiler_params=_params(("arbitrary", "arbitrary"), 40 << 20),
        name="adaln",
    )(cond, ada_w, ada_b.reshape(DEPTH, 1, n))


def _modulate_into(h_ref, x_ref, g_ref, sh_ref, sc_ref):
    g = g_ref[...]
    shift = sh_ref[0]
    scale1 = 1.0 + sc_ref[0]

    def body(c, carry):
        rows = pl.ds(pl.multiple_of(c * MOD_CHUNK, MOD_CHUNK), MOD_CHUNK)
        x = x_ref[rows, :]
        ms = jnp.mean(x * x, axis=-1, keepdims=True)
        h_ref[rows, :] = (x * lax.rsqrt(ms + EPS) * g * scale1 + shift).astype(h_ref.dtype)
        return carry

    lax.fori_loop(0, x_ref.shape[0] // MOD_CHUNK, body, 0)


def _rope_tables(t_len):
    half = RET_QK_DIM // 2
    nf = half // 2
    inv = ROPE_BASE ** (-jnp.arange(nf, dtype=F32) / nf)
    t = jnp.arange(t_len)
    rows = (t // GRID_W).astype(F32)
    cols = (t % GRID_W).astype(F32)

    def tabs(pos):
        ang = pos[:, None] * inv[None, :]
        c, s = jnp.cos(ang), jnp.sin(ang)
        return jnp.concatenate([c, c], axis=-1), jnp.concatenate([-s, s], axis=-1)

    cr, sr = tabs(rows)
    cc, sc = tabs(cols)
    return jnp.concatenate([cr, cc], axis=-1), jnp.concatenate([sr, sc], axis=-1)


def _rope(x, cos, sin):
    half = RET_QK_DIM // 2
    swapped = [pltpu.roll(x[:, s * half:(s + 1) * half], half // 2, axis=1) for s in range(2)]
    return x * cos + jnp.concatenate(swapped, axis=-1) * sin


def _modproj_kernel(*refs, rope_tiles):
    if rope_tiles:
        x_ref, sh_ref, sc_ref, g_ref, w_ref, cos_ref, sin_ref, o_ref, h_ref = refs
    else:
        x_ref, sh_ref, sc_ref, g_ref, w_ref, o_ref, h_ref = refs
    j = pl.program_id(1)

    @pl.when(j == 0)
    def _():
        _modulate_into(h_ref, x_ref, g_ref, sh_ref, sc_ref)

    acc = jnp.dot(h_ref[...], w_ref[...], preferred_element_type=F32)
    if not rope_tiles:
        o_ref[...] = acc.astype(o_ref.dtype)
        return

    @pl.when(j < rope_tiles)
    def _():
        cos = cos_ref[...]
        sin = sin_ref[...]
        for s in range(acc.shape[1] // RET_QK_DIM):
            cs = slice(s * RET_QK_DIM, (s + 1) * RET_QK_DIM)
            o_ref[:, cs] = _rope(acc[:, cs], cos, sin).astype(o_ref.dtype)

    @pl.when(j >= rope_tiles)
    def _():
        o_ref[...] = acc.astype(o_ref.dtype)


def modproj(x, shift, scale, g, w, rows_per_mod, out_dtype, tm, tn, rope=None):
    m, d = x.shape
    n = w.shape[1]
    osz = jnp.dtype(out_dtype).itemsize
    vmem = 2 * tm * d * 4 + 2 * d * tn * 2 + 2 * tm * tn * osz + tm * d * 2 + 2 * tm * tn * 4 + VMEM_SLACK
    mod_idx = lambda i, j: ((i * tm) // rows_per_mod, 0, 0)
    in_specs = [
        pl.BlockSpec((tm, d), lambda i, j: (i, 0)),
        pl.BlockSpec((1, 1, d), mod_idx),
        pl.BlockSpec((1, 1, d), mod_idx),
        pl.BlockSpec((1, d), lambda i, j: (0, 0)),
        pl.BlockSpec((d, tn), lambda i, j: (0, j)),
    ]
    args = [x, shift, scale, g, w]
    rope_tiles = 0
    if rope is not None:
        seq_len, n_cols = rope
        rope_tiles = n_cols // tn
        cos, sin = _rope_tables(seq_len)
        tab = pl.BlockSpec((tm, RET_QK_DIM), lambda i, j: (i % (seq_len // tm), 0))
        in_specs += [tab, tab]
        args += [cos, sin]
        vmem += 2 * 2 * tm * RET_QK_DIM * 4
    return pl.pallas_call(
        functools.partial(_modproj_kernel, rope_tiles=rope_tiles),
        out_shape=jax.ShapeDtypeStruct((m, n), out_dtype),
        grid=(m // tm, n // tn),
        in_specs=in_specs,
        out_specs=pl.BlockSpec((tm, tn), lambda i, j: (i, j)),
        scratch_shapes=[pltpu.VMEM((tm, d), BF16)],
        compiler_params=_params(("arbitrary", "arbitrary"), vmem),
        name="modproj",
    )(*args)


def _head_rms(x, g):
    return x * lax.rsqrt(jnp.mean(x * x, axis=-1, keepdims=True) + EPS) * g


def _ctx_attn_kernel(q_ref, k_ref, v_ref, qg_ref, kg_ref, o_ref, nk_ref, nv_ref):
    scale = NA_HEAD_DIM ** -0.5
    for h in range(NA_HEADS):
        cs = slice(h * NA_HEAD_DIM, (h + 1) * NA_HEAD_DIM)
        q = _head_rms(q_ref[0, :, cs].astype(F32), qg_ref[...])
        k = _head_rms(k_ref[0, :, cs].astype(F32), kg_ref[...])
        v = v_ref[0, :, cs].astype(F32)
        nk_ref[0, 0, h] = k
        nv_ref[0, 0, h] = v
        s = lax.dot_general(q.astype(BF16), k.astype(BF16), (((1,), (1,)), ((), ())),
                            preferred_element_type=F32) * scale
        m = jnp.max(s, axis=-1, keepdims=True)
        e = jnp.exp(s - m)
        l = jnp.sum(e, axis=-1, keepdims=True)
        o = jnp.dot(e.astype(BF16), v.astype(BF16), preferred_element_type=F32) / l
        o_ref[0, :, cs] = o.astype(o_ref.dtype)


def ctx_attention(p, q_gain, k_gain):
    b, t, _ = p.shape
    cache_shape = (b, 1, NA_HEADS, t, NA_HEAD_DIM)
    blk = lambda c: pl.BlockSpec((1, t, NA_WIDTH), lambda i, c=c: (i, 0, c))
    gain = pl.BlockSpec((1, NA_HEAD_DIM), lambda i: (0, 0))
    cache_spec = pl.BlockSpec((1, 1, NA_HEADS, t, NA_HEAD_DIM), lambda i: (i, 0, 0, 0, 0))
    return pl.pallas_call(
        _ctx_attn_kernel,
        out_shape=(jax.ShapeDtypeStruct((b, t, NA_WIDTH), BF16),
                   jax.ShapeDtypeStruct(cache_shape, F32),
                   jax.ShapeDtypeStruct(cache_shape, F32)),
        grid=(b,),
        in_specs=[blk(1), blk(2), blk(3), gain, gain],
        out_specs=(pl.BlockSpec((1, t, NA_WIDTH), lambda i: (i, 0, 0)), cache_spec, cache_spec),
        compiler_params=_params(("arbitrary",), 32 << 20),
        name="ctx_attention",
    )(p, p, p, q_gain, k_gain)


NA_BIAS_ROWS = 2 * NA_KH - 1
NA_BIAS_COLS = 2 * NA_KW - 1
NA_MASK_TILE = NA_BIAS_ROWS
NA_QROWS = 4
NA_KROWS = NA_KH + NA_QROWS


def _na_build_bias(rb_ref, bias_ref, head):
    shape = (GRID_W, 2 * GRID_W)
    lane = lax.broadcasted_iota(jnp.int32, shape, 1)
    qc = lax.broadcasted_iota(jnp.int32, shape, 0)
    kc = lane & (GRID_W - 1)
    start = jnp.clip(qc - NA_KW // 2, 0, GRID_W - NA_KW)
    ok = (kc >= start) & (kc < start + NA_KW)
    delta = kc - qc + (NA_KW - 1)
    left = lane < GRID_W
    base = head * (NA_BIAS_ROWS * NA_BIAS_COLS)

    def build(dr, carry):
        acc = jnp.full(shape, NEG_INF, F32)
        for dc in range(NA_BIAS_COLS):
            acc = jnp.where(delta == dc, rb_ref[base + dr * NA_BIAS_COLS + dc], acc)
        tile = jnp.where(ok, acc, NEG_INF)
        bias_ref[0, dr] = jnp.where(left, tile, 0.0)
        bias_ref[1, dr] = jnp.where(left, 0.0, tile)
        return carry

    lax.fori_loop(0, NA_BIAS_ROWS, build, 0)
    bias_ref[0, NA_MASK_TILE] = jnp.where(left, NEG_INF, 0.0)
    bias_ref[1, NA_MASK_TILE] = jnp.where(left, 0.0, NEG_INF)


def _na_kernel(rb_ref, q_ref, k_ref, v_ref, kc_ref, vc_ref, qg_ref, kg_ref, o_ref,
               qs_ref, ks_ref, vs_ref, bias_ref, *, rows):
    @pl.when(pl.program_id(1) == 0)
    def _():
        _na_build_bias(rb_ref, bias_ref, pl.program_id(0))

    scale = NA_HEAD_DIM ** -0.5
    qs_ref[...] = (_head_rms(q_ref[0].astype(F32), qg_ref[...]) * scale).astype(BF16)
    ks_ref[...] = _head_rms(k_ref[0].astype(F32), kg_ref[...]).astype(BF16)
    vs_ref[...] = v_ref[0].astype(BF16)
    kc = kc_ref[0, 0, 0].astype(BF16)
    vc = vc_ref[0, 0, 0].astype(BF16)
    nt = (((1,), (1,)), ((), ()))
    nq = NA_QROWS * GRID_W
    nk = NA_KROWS * GRID_W

    def tile_index(r, rs, kr):
        inside = (kr >= rs) & (kr < rs + NA_KH)
        return jnp.where(inside, kr - r + (NA_KH - 1), NA_MASK_TILE)

    def body(blk, carry):
        r0 = blk * NA_QROWS
        k0 = jnp.clip(r0 - NA_KH // 2, 0, rows - NA_KROWS)
        q = qs_ref[pl.ds(pl.multiple_of(r0 * GRID_W, nq), nq), :]
        kw = ks_ref[pl.ds(pl.multiple_of(k0 * GRID_W, GRID_W), nk), :]
        vw = vs_ref[pl.ds(pl.multiple_of(k0 * GRID_W, GRID_W), nk), :]
        bias_rows = []
        for i in range(NA_QROWS):
            r = r0 + i
            rs = jnp.clip(r - NA_KH // 2, 0, rows - NA_KH)
            pairs = [bias_ref[0, tile_index(r, rs, k0 + 2 * jp)]
                     + bias_ref[1, tile_index(r, rs, k0 + 2 * jp + 1)]
                     for jp in range(NA_KROWS // 2)]
            bias_rows.append(jnp.concatenate(pairs, axis=-1))
        bias = jnp.concatenate(bias_rows, axis=0)
        s_w = lax.dot_general(q, kw, nt, preferred_element_type=F32) + bias
        s_c = lax.dot_general(q, kc, nt, preferred_element_type=F32)
        m = jnp.maximum(jnp.max(s_w, axis=-1, keepdims=True), jnp.max(s_c, axis=-1, keepdims=True))
        e_w = jnp.exp(s_w - m)
        e_c = jnp.exp(s_c - m)
        l = jnp.sum(e_w, axis=-1, keepdims=True) + jnp.sum(e_c, axis=-1, keepdims=True)
        o = (jnp.dot(e_w.astype(BF16), vw, preferred_element_type=F32)
             + jnp.dot(e_c.astype(BF16), vc, preferred_element_type=F32)) / l
        o_ref[0, pl.ds(pl.multiple_of(r0 * GRID_W, nq), nq), :] = o.astype(o_ref.dtype)
        return carry

    lax.fori_loop(0, rows // NA_QROWS, body, 0, unroll=2)


def na_attention(p, cache_k, cache_v, layer_j, rel_bias, q_gain, k_gain):
    b, t, _ = p.shape
    past = cache_k.shape[3]
    rows = t // GRID_W
    hd = NA_HEAD_DIM
    col0 = FNET_WIDTH // hd
    blk = lambda c: pl.BlockSpec((1, t, hd), lambda h, i, rb, c=c: (i, 0, col0 + c * NA_HEADS + h))
    cache_spec = pl.BlockSpec((1, 1, 1, past, hd), lambda h, i, rb: (i, layer_j, h, 0, 0))
    gain = pl.BlockSpec((1, hd), lambda h, i, rb: (0, 0))
    return pl.pallas_call(
        functools.partial(_na_kernel, rows=rows),
        out_shape=jax.ShapeDtypeStruct((b, t, NA_WIDTH), BF16),
        grid_spec=pltpu.PrefetchScalarGridSpec(
            num_scalar_prefetch=1,
            grid=(NA_HEADS, b),
            in_specs=[blk(0), blk(1), blk(2), cache_spec, cache_spec, gain, gain],
            out_specs=pl.BlockSpec((1, t, hd), lambda h, i, rb: (i, 0, h)),
            scratch_shapes=[pltpu.VMEM((t, hd), BF16)] * 3
            + [pltpu.VMEM((2, NA_BIAS_ROWS + 1, GRID_W, 2 * GRID_W), F32)],
        ),
        compiler_params=_params(("arbitrary", "arbitrary"), 40 << 20),
        name="na_attention",
    )(rel_bias.astype(F32).reshape(-1), p, p, p, cache_k, cache_v, q_gain, k_gain)


def _dft_tables(n, scale):
    k = np.arange(n, dtype=np.int64)
    ang = (2.0 * np.pi / n) * ((k[:, None] * k[None, :]) % n).astype(np.float64)
    return (np.cos(ang) * scale).astype(np.float32), (np.sin(ang) * scale).astype(np.float32)


def _dft_tables_device(n, scale):
    k = jnp.arange(n, dtype=jnp.int32)
    ang = ((k[:, None] * k[None, :]) % n).astype(F32) * (2.0 * math.pi / n)
    return (jnp.cos(ang) * scale).astype(BF16), (jnp.sin(ang) * scale).astype(BF16)


def _fnet_kernel(ct_ref, st_ref, u_ref, cc_ref, sc_ref, o_ref):
    u = u_ref[0].astype(BF16)
    a = jnp.dot(ct_ref[...], u, preferred_element_type=F32).astype(BF16)
    b = jnp.dot(st_ref[...], u, preferred_element_type=F32).astype(BF16)
    cc = cc_ref[...]
    sc = sc_ref[...]
    for g in range(FNET_GROUPS):
        cs = slice(g * FNET_GROUP_DIM, (g + 1) * FNET_GROUP_DIM)
        y = (jnp.dot(a[:, cs], cc, preferred_element_type=F32)
             - jnp.dot(b[:, cs], sc, preferred_element_type=F32))
        o_ref[0, :, cs] = y.astype(o_ref.dtype)


def fourier_mix(p, tm):
    b, t, _ = p.shape
    ct, st = _dft_tables_device(t, 1.0 / math.sqrt(t))
    cc_np, sc_np = _dft_tables(FNET_GROUP_DIM, 1.0 / math.sqrt(FNET_GROUP_DIM))
    cc = jnp.asarray(cc_np, dtype=BF16)
    sc = jnp.asarray(sc_np, dtype=BF16)
    isz = p.dtype.itemsize
    vmem = 2 * 2 * tm * t * 2 + 2 * t * FNET_WIDTH * isz + t * FNET_WIDTH * 2 + 8 * tm * FNET_WIDTH * 4 + VMEM_SLACK
    tab = pl.BlockSpec((tm, t), lambda i, r: (r, 0))
    small = pl.BlockSpec((FNET_GROUP_DIM, FNET_GROUP_DIM), lambda i, r: (0, 0))
    return pl.pallas_call(
        _fnet_kernel,
        out_shape=jax.ShapeDtypeStruct((b, t, FNET_WIDTH), BF16),
        grid=(b, t // tm),
        in_specs=[tab, tab, pl.BlockSpec((1, t, FNET_WIDTH), lambda i, r: (i, 0, 0)), small, small],
        out_specs=pl.BlockSpec((1, tm, FNET_WIDTH), lambda i, r: (i, r, 0)),
        compiler_params=_params(("arbitrary", "arbitrary"), vmem),
        name="fourier_mix",
    )(ct, st, p, cc, sc)


def _even_out_kernel(f_ref, a_ref, wf_ref, wa_ref, x_ref, gate_ref, o_ref):
    y = (jnp.dot(f_ref[...], wf_ref[...], preferred_element_type=F32)
         + jnp.dot(a_ref[...], wa_ref[...], preferred_element_type=F32))
    o_ref[...] = x_ref[...] + gate_ref[0] * y


def even_out(f, a, w, x, gate, rows_per_mod, tm, tn):
    m, d = x.shape
    kf = f.shape[1]
    vmem = 2 * 2 * tm * kf * 2 + 2 * 2 * kf * tn * 2 + 4 * tm * tn * 4 + tm * tn * 4 + VMEM_SLACK
    return pl.pallas_call(
        _even_out_kernel,
        out_shape=jax.ShapeDtypeStruct((m, d), F32),
        grid=(d // tn, m // tm),
        in_specs=[
            pl.BlockSpec((tm, kf), lambda j, i: (i, 0)),
            pl.BlockSpec((tm, kf), lambda j, i: (i, 0)),
            pl.BlockSpec((kf, tn), lambda j, i: (0, j)),
            pl.BlockSpec((kf, tn), lambda j, i: (1, j)),
            pl.BlockSpec((tm, tn), lambda j, i: (i, j)),
            pl.BlockSpec((1, 1, tn), lambda j, i: ((i * tm) // rows_per_mod, 0, j)),
        ],
        out_specs=pl.BlockSpec((tm, tn), lambda j, i: (i, j)),
        compiler_params=_params(("arbitrary", "arbitrary"), vmem),
        name="even_out",
    )(f, a, w, w, x, gate)


def _gated_group_norm(o, g):
    o = o.astype(F32)
    oc = o - jnp.mean(o, axis=-1, keepdims=True)
    gn = oc * lax.rsqrt(jnp.mean(oc * oc, axis=-1, keepdims=True) + EPS)
    hg = 0.5 * g.astype(F32)
    return (hg + hg * jnp.tanh(hg)) * gn


def _odd_out_kernel(of_ref, ob_ref, gf_ref, gb_ref, w_ref, x_ref, gate_ref, o_ref):
    acc = None
    for h in range(RET_HEADS):
        cs = slice(h * RET_V_DIM, (h + 1) * RET_V_DIM)
        y = (_gated_group_norm(of_ref[:, cs], gf_ref[:, cs])
             + _gated_group_norm(ob_ref[:, cs], gb_ref[:, cs])).astype(BF16)
        part = jnp.dot(y, w_ref[cs, :], preferred_element_type=F32)
        acc = part if acc is None else acc + part
    o_ref[...] = x_ref[...] + gate_ref[0] * acc


def odd_out(o_f, o_b, p, w, x, gate, rows_per_mod, tm):
    m, d = x.shape
    k = w.shape[0]
    gcol = (2 * RET_QK_WIDTH + RET_V_WIDTH) // RET_V_WIDTH
    vmem = (2 * 4 * tm * k * 2 + k * d * 2 + 4 * tm * d * 4 + 2 * tm * d * 4
            + 6 * tm * RET_V_DIM * 4 + VMEM_SLACK)
    act = lambda c: pl.BlockSpec((tm, k), lambda i, c=c: (i, c))
    return pl.pallas_call(
        _odd_out_kernel,
        out_shape=jax.ShapeDtypeStruct((m, d), F32),
        grid=(m // tm,),
        in_specs=[
            act(0), act(0), act(gcol), act(gcol + 1),
            pl.BlockSpec((k, d), lambda i: (0, 0), pipeline_mode=pl.Buffered(1)),
            pl.BlockSpec((tm, d), lambda i: (i, 0)),
            pl.BlockSpec((1, 1, d), lambda i: ((i * tm) // rows_per_mod, 0, 0)),
        ],
        out_specs=pl.BlockSpec((tm, d), lambda i: (i, 0)),
        compiler_params=_params(("arbitrary",), vmem),
        name="odd_out",
    )(o_f, o_b, p, p, w, x, gate)


def _ffn_kernel(x_ref, sh_ref, sc_ref, gate_ref, g_ref, wg_ref, wu_ref, wd_ref, o_ref, h_ref):
    j = pl.program_id(1)

    @pl.when(j == 0)
    def _():
        _modulate_into(h_ref, x_ref, g_ref, sh_ref, sc_ref)
        o_ref[...] = jnp.zeros_like(o_ref)

    h = h_ref[...]
    a = jnp.dot(h, wg_ref[...], preferred_element_type=F32)
    u = jnp.dot(h, wu_ref[...], preferred_element_type=F32)
    act = (_silu(a) * u).astype(BF16)
    o_ref[...] += jnp.dot(act, wd_ref[...], preferred_element_type=F32)

    @pl.when(j == pl.num_programs(1) - 1)
    def _():
        o_ref[...] = x_ref[...] + gate_ref[0] * o_ref[...]


def ffn_block(x, shift, scale, gate, g, w_gate, w_up, w_down, rows_per_mod, tm, tf):
    m, d = x.shape
    f = w_gate.shape[1]
    vmem = 4 * tm * d * 4 + tm * d * 2 + 2 * 3 * d * tf * 2 + 4 * tm * tf * 4 + tm * d * 4 + VMEM_SLACK
    mod_idx = lambda i, j: ((i * tm) // rows_per_mod, 0, 0)
    mod = pl.BlockSpec((1, 1, d), mod_idx)
    return pl.pallas_call(
        _ffn_kernel,
        out_shape=jax.ShapeDtypeStruct((m, d), F32),
        grid=(m // tm, f // tf),
        in_specs=[
            pl.BlockSpec((tm, d), lambda i, j: (i, 0)),
            mod, mod, mod,
            pl.BlockSpec((1, d), lambda i, j: (0, 0)),
            pl.BlockSpec((d, tf), lambda i, j: (0, j)),
            pl.BlockSpec((d, tf), lambda i, j: (0, j)),
            pl.BlockSpec((tf, d), lambda i, j: (j, 0)),
        ],
        out_specs=pl.BlockSpec((tm, d), lambda i, j: (i, 0)),
        scratch_shapes=[pltpu.VMEM((tm, d), BF16)],
        compiler_params=_params(("arbitrary", "arbitrary"), vmem),
        name="ffn_block",
    )(x, shift, scale, gate, g, w_gate, w_up, w_down)


def _retention_kernel(lg_ref, *refs, latent, n_blocks):
    if latent:
        (qf_ref, kf_ref, vf_ref, qb_ref, kb_ref, vb_ref, s0_ref,
         of_ref, ob_ref, s_ref, dec_ref, qd_ref, kd_ref) = refs
    else:
        (qf_ref, kf_ref, vf_ref, qb_ref, kb_ref, vb_ref,
         of_ref, ob_ref, sfin_ref, s_ref, dec_ref, qd_ref, kd_ref) = refs
    h = pl.program_id(0)
    i = pl.program_id(2)
    c = RET_BLOCK
    scale = RET_QK_DIM ** -0.5

    @pl.when((pl.program_id(1) == 0) & (i == 0))
    def _():
        row = lax.broadcasted_iota(jnp.int32, (c, c), 0).astype(F32)
        col = lax.broadcasted_iota(jnp.int32, (c, c), 1).astype(F32)
        t_idx = lax.broadcasted_iota(jnp.int32, (c, 1), 0).astype(F32)
        for d in range(2):
            lg = lg_ref[d, h]
            diff = row - col if d == 0 else col - row
            dec_ref[d] = jnp.where(diff >= 0, jnp.exp(lg * jnp.maximum(diff, 0.0)), 0.0) * scale
            pos = t_idx if d == 0 else (c - 1.0) - t_idx
            qd_ref[d] = jnp.exp(lg * (pos + 1.0))
            kd_ref[d] = jnp.exp(lg * ((c - 1.0) - pos)) * scale

    @pl.when(i == 0)
    def _():
        for d in range(2):
            s_ref[d] = s0_ref[0, 0, d, 0] if latent else jnp.zeros(s_ref.shape[1:], F32)

    nt = (((1,), (1,)), ((), ()))
    tn = (((0,), (0,)), ((), ()))
    for d, (q_ref, k_ref, v_ref, o_ref) in enumerate(((qf_ref, kf_ref, vf_ref, of_ref),
                                                      (qb_ref, kb_ref, vb_ref, ob_ref))):
        q = q_ref[0]
        k = k_ref[0]
        v = v_ref[0]
        a = lax.dot_general(q, k, nt, preferred_element_type=F32)
        inner_w = (a * dec_ref[d]).astype(BF16)
        q_dec = (q.astype(F32) * qd_ref[d]).astype(BF16)
        k_dec = (k.astype(F32) * kd_ref[d]).astype(BF16)
        s_old = s_ref[d]
        o = (jnp.dot(inner_w, v, preferred_element_type=F32)
             + jnp.dot(q_dec, s_old.astype(BF16), preferred_element_type=F32))
        s_new = s_old * jnp.exp(lg_ref[d, h] * c) + lax.dot_general(k_dec, v, tn,
                                                                   preferred_element_type=F32)
        s_ref[d] = s_new
        o_ref[0] = o.astype(o_ref.dtype)
        if not latent:
            @pl.when(i == n_blocks - 1)
            def _():
                sfin_ref[0, 0, d, 0] = s_new


def retention(p, log_gamma, state0, layer_j):
    b, t, _ = p.shape
    latent = state0 is not None
    c = RET_BLOCK
    nb = t // c
    dk, dv, nh = RET_QK_DIM, RET_V_DIM, RET_HEADS

    def chunk_specs(blk_of):
        return [
            pl.BlockSpec((1, c, dk), lambda h, bi, i, lg: (bi, blk_of(i), h)),
            pl.BlockSpec((1, c, dk), lambda h, bi, i, lg: (bi, blk_of(i), nh + h)),
            pl.BlockSpec((1, c, dv), lambda h, bi, i, lg: (bi, blk_of(i), nh + h)),
        ]

    fwd = lambda i: i
    bwd = lambda i: nb - 1 - i
    in_specs = chunk_specs(fwd) + chunk_specs(bwd)
    args = [p] * 6
    o_shape = jax.ShapeDtypeStruct((b, t, RET_V_WIDTH), BF16)
    out_shape = [o_shape, o_shape]
    out_specs = [pl.BlockSpec((1, c, dv), lambda h, bi, i, lg: (bi, fwd(i), h)),
                 pl.BlockSpec((1, c, dv), lambda h, bi, i, lg: (bi, bwd(i), h))]
    state_block = (1, 1, 2, 1, dk, dv)
    if latent:
        in_specs.append(pl.BlockSpec(state_block, lambda h, bi, i, lg: (bi, layer_j, 0, h, 0, 0)))
        args.append(state0)
    else:
        out_shape.append(jax.ShapeDtypeStruct((b, 1, 2, nh, dk, dv), F32))
        out_specs.append(pl.BlockSpec(state_block, lambda h, bi, i, lg: (bi, 0, 0, h, 0, 0)))
    return pl.pallas_call(
        functools.partial(_retention_kernel, latent=latent, n_blocks=nb),
        out_shape=tuple(out_shape),
        grid_spec=pltpu.PrefetchScalarGridSpec(
            num_scalar_prefetch=1,
            grid=(nh, b, nb),
            in_specs=in_specs,
            out_specs=tuple(out_specs),
            scratch_shapes=[pltpu.VMEM((2, dk, dv), F32), pltpu.VMEM((2, c, c), F32),
                            pltpu.VMEM((2, c, 1), F32), pltpu.VMEM((2, c, 1), F32)],
        ),
        compiler_params=_params(("arbitrary",) * 3, 32 << 20),
        name="retention",
    )(log_gamma, *args)


def _trunk(x, mods, ctx_k, ctx_v, ctx_state, wts, p_dtype):
    b, t, d = x.shape
    is_ctx = ctx_k is None
    nbm = mods.shape[1]
    rows_per_mod = (b * t) // nbm
    x2 = x.reshape(b * t, d)
    new_k = new_v = new_s = None
    for i in range(DEPTH):
        mod = [mods[i, :, k][:, None, :] for k in range(6)]
        j = i // 2
        if i % 2 == 0:
            p = modproj(x2, mod[0], mod[1], wts["norm_g"][i, 0][None], wts["even_w_in"][j],
                        rows_per_mod, p_dtype, tm=1024, tn=512).reshape(b, t, EVEN_IN)
            qg, kg = wts["even_q_norm"][j][None], wts["even_k_norm"][j][None]
            if is_ctx:
                attn, new_k, new_v = ctx_attention(p, qg, kg)
            else:
                attn = na_attention(p, ctx_k, ctx_v, j, wts["na_rel_bias"][j], qg, kg)
            fm = fourier_mix(p, tm=min(t, 256))
            x2 = even_out(fm.reshape(b * t, FNET_WIDTH), attn.reshape(b * t, NA_WIDTH),
                          wts["even_w_out"][j], x2, mod[2], rows_per_mod, tm=512, tn=1024)
        else:
            rope = (t, 2 * RET_QK_WIDTH) if not is_ctx else None
            p = modproj(x2, mod[0], mod[1], wts["norm_g"][i, 0][None], wts["odd_w_in"][j],
                        rows_per_mod, BF16, tm=1024, tn=512, rope=rope)
            lg = jax.nn.log_sigmoid(wts["ret_decay_logit"][j].astype(F32))
            if is_ctx:
                o_f, o_b, new_s = retention(p.reshape(b, t, ODD_IN), lg, None, j)
            else:
                o_f, o_b = retention(p.reshape(b, t, ODD_IN), lg, ctx_state, j)
            x2 = odd_out(o_f.reshape(b * t, RET_V_WIDTH), o_b.reshape(b * t, RET_V_WIDTH), p,
                         wts["odd_w_out"][j], x2, mod[2], rows_per_mod, tm=256)
        x2 = ffn_block(x2, mod[3], mod[4], mod[5], wts["norm_g"][i, 1][None], wts["ffn_w_gate"][i],
                       wts["ffn_w_up"][i], wts["ffn_w_down"][i], rows_per_mod, tm=512, tf=512)
    return x2.reshape(b, t, d), new_k, new_v, new_s


def kernel(x_prompt, x_sample, cache_k, cache_v, state_ret, c, c_ctx, ada_w, ada_b, norm_g,
           even_w_in, even_q_norm, even_k_norm, na_rel_bias, even_w_out, odd_w_in, ret_decay_logit,
           odd_w_out, ffn_w_gate, ffn_w_up, ffn_w_down):
    nb_lat = c.shape[0]
    cond = jnp.concatenate(
        [c, c_ctx[None, :], jnp.zeros((MOD_ROWS - nb_lat - 1, D_MODEL), F32)], axis=0)
    mods = adaln_all(cond, ada_w, ada_b).reshape(DEPTH, MOD_ROWS, 6, D_MODEL)
    wts = dict(
        norm_g=norm_g, even_q_norm=even_q_norm, even_k_norm=even_k_norm, na_rel_bias=na_rel_bias,
        ret_decay_logit=ret_decay_logit,
        even_w_in=even_w_in.astype(BF16), even_w_out=even_w_out.astype(BF16),
        odd_w_in=odd_w_in.astype(BF16), odd_w_out=odd_w_out.astype(BF16),
        ffn_w_gate=ffn_w_gate.astype(BF16), ffn_w_up=ffn_w_up.astype(BF16),
        ffn_w_down=ffn_w_down.astype(BF16),
    )
    y_prompt, new_k, new_v, new_s = _trunk(
        x_prompt, mods[:, nb_lat:nb_lat + 1], None, None, None, wts, F32)
    y_sample, _, _, _ = _trunk(x_sample, mods[:, :nb_lat], cache_k, cache_v, state_ret, wts, BF16)
    return (y_prompt, y_sample, new_k, new_v, new_s)
```

```python
import functools
import math

import numpy as np
import jax
import jax.numpy as jnp
from jax import lax
from jax.experimental import pallas as pl
from jax.experimental.pallas import tpu as pltpu

F32 = jnp.float32
BF16 = jnp.bfloat16

D_MODEL = 2048
DEPTH = 2
GRID_W = 64
EPS = 1e-6
NEG_INF = -1e30
FNET_GROUPS = 8
FNET_GROUP_DIM = 128
FNET_WIDTH = 1024
NA_HEADS = 8
NA_HEAD_DIM = 128
NA_WIDTH = 1024
NA_KH = 8
NA_KW = 16
EVEN_IN = FNET_WIDTH + 3 * NA_WIDTH
RET_HEADS = 8
RET_QK_DIM = 256
RET_V_DIM = 512
RET_QK_WIDTH = RET_HEADS * RET_QK_DIM
RET_V_WIDTH = RET_HEADS * RET_V_DIM
ODD_IN = 2 * RET_QK_WIDTH + 3 * RET_V_WIDTH
RET_BLOCK = 256
RET_HEADS_PER_STEP = 2
ROPE_BASE = 10000.0
D_FF = 5632
MOD_ROWS = 8
MOD_CHUNK = 128

V7X_VMEM_BUDGET = 56 * 1024 * 1024
VMEM_SLACK = 8 * 1024 * 1024


def _params(semantics, vmem_bytes):
    return pltpu.CompilerParams(dimension_semantics=semantics,
                                vmem_limit_bytes=min(int(vmem_bytes), V7X_VMEM_BUDGET))


def _silu(x):
    return x * (1.0 / (1.0 + jnp.exp(-x)))


def _adaln_kernel(c_ref, w_ref, b_ref, o_ref):
    s = _silu(c_ref[...]).astype(BF16)
    w = w_ref[0].astype(BF16)
    o_ref[0] = jnp.dot(s, w, preferred_element_type=F32) + b_ref[0]


def adaln_all(cond, ada_w, ada_b):
    n = ada_w.shape[-1]
    tn = 1024
    return pl.pallas_call(
        _adaln_kernel,
        out_shape=jax.ShapeDtypeStruct((DEPTH, MOD_ROWS, n), F32),
        grid=(DEPTH, n // tn),
        in_specs=[
            pl.BlockSpec((MOD_ROWS, D_MODEL), lambda l, j: (0, 0)),
            pl.BlockSpec((1, D_MODEL, tn), lambda l, j: (l, 0, j)),
            pl.BlockSpec((1, 1, tn), lambda l, j: (l, 0, j)),
        ],
        out_specs=pl.BlockSpec((1, MOD_ROWS, tn), lambda l, j: (l, 0, j)),
        compiler_params=_params(("arbitrary", "arbitrary"), 40 << 20),
        name="adaln",
    )(cond, ada_w, ada_b.reshape(DEPTH, 1, n))


def _modulate_into(h_ref, x_ref, g_ref, sh_ref, sc_ref):
    g = g_ref[...]
    shift = sh_ref[0]
    scale1 = 1.0 + sc_ref[0]

    def body(c, carry):
        rows = pl.ds(pl.multiple_of(c * MOD_CHUNK, MOD_CHUNK), MOD_CHUNK)
        x = x_ref[rows, :]
        ms = jnp.mean(x * x, axis=-1, keepdims=True)
        h_ref[rows, :] = (x * lax.rsqrt(ms + EPS) * g * scale1 + shift).astype(h_ref.dtype)
        return carry

    lax.fori_loop(0, x_ref.shape[0] // MOD_CHUNK, body, 0)


def _rope_tables(t_len):
    half = RET_QK_DIM // 2
    nf = half // 2
    inv = ROPE_BASE ** (-jnp.arange(nf, dtype=F32) / nf)
    t = jnp.arange(t_len)
    rows = (t // GRID_W).astype(F32)
    cols = (t % GRID_W).astype(F32)

    def tabs(pos):
        ang = pos[:, None] * inv[None, :]
        c, s = jnp.cos(ang), jnp.sin(ang)
        return jnp.concatenate([c, c], axis=-1), jnp.concatenate([-s, s], axis=-1)

    cr, sr = tabs(rows)
    cc, sc = tabs(cols)
    return jnp.concatenate([cr, cc], axis=-1), jnp.concatenate([sr, sc], axis=-1)


def _rope(x, cos, sin):
    half = RET_QK_DIM // 2
    swapped = [pltpu.roll(x[:, s * half:(s + 1) * half], half // 2, axis=1) for s in range(2)]
    return x * cos + jnp.concatenate(swapped, axis=-1) * sin


def _modproj_kernel(*refs, rope_tiles):
    if rope_tiles:
        x_ref, sh_ref, sc_ref, g_ref, w_ref, cos_ref, sin_ref, o_ref, h_ref = refs
    else:
        x_ref, sh_ref, sc_ref, g_ref, w_ref, o_ref, h_ref = refs
    j = pl.program_id(1)

    @pl.when(j == 0)
    def _():
        _modulate_into(h_ref, x_ref, g_ref, sh_ref, sc_ref)

    acc = jnp.dot(h_ref[...], w_ref[...], preferred_element_type=F32)
    if not rope_tiles:
        o_ref[...] = acc.astype(o_ref.dtype)
        return

    @pl.when(j < rope_tiles)
    def _():
        cos = cos_ref[...]
        sin = sin_ref[...]
        for s in range(acc.shape[1] // RET_QK_DIM):
            cs = slice(s * RET_QK_DIM, (s + 1) * RET_QK_DIM)
            o_ref[:, cs] = _rope(acc[:, cs], cos, sin).astype(o_ref.dtype)

    @pl.when(j >= rope_tiles)
    def _():
        o_ref[...] = acc.astype(o_ref.dtype)


def modproj(x, shift, scale, g, w, rows_per_mod, out_dtype, tm, tn, rope=None):
    m, d = x.shape
    n = w.shape[1]
    osz = jnp.dtype(out_dtype).itemsize
    vmem = 2 * tm * d * 4 + 2 * d * tn * 2 + 2 * tm * tn * osz + tm * d * 2 + 2 * tm * tn * 4 + VMEM_SLACK
    mod_idx = lambda i, j: ((i * tm) // rows_per_mod, 0, 0)
    in_specs = [
        pl.BlockSpec((tm, d), lambda i, j: (i, 0)),
        pl.BlockSpec((1, 1, d), mod_idx),
        pl.BlockSpec((1, 1, d), mod_idx),
        pl.BlockSpec((1, d), lambda i, j: (0, 0)),
        pl.BlockSpec((d, tn), lambda i, j: (0, j)),
    ]
    args = [x, shift, scale, g, w]
    rope_tiles = 0
    if rope is not None:
        seq_len, n_cols = rope
        rope_tiles = n_cols // tn
        cos, sin = _rope_tables(seq_len)
        tab = pl.BlockSpec((tm, RET_QK_DIM), lambda i, j: (i % (seq_len // tm), 0))
        in_specs += [tab, tab]
        args += [cos, sin]
        vmem += 2 * 2 * tm * RET_QK_DIM * 4
    return pl.pallas_call(
        functools.partial(_modproj_kernel, rope_tiles=rope_tiles),
        out_shape=jax.ShapeDtypeStruct((m, n), out_dtype),
        grid=(m // tm, n // tn),
        in_specs=in_specs,
        out_specs=pl.BlockSpec((tm, tn), lambda i, j: (i, j)),
        scratch_shapes=[pltpu.VMEM((tm, d), BF16)],
        compiler_params=_params(("arbitrary", "arbitrary"), vmem),
        name="modproj",
    )(*args)


def _head_rms(x, g):
    return x * lax.rsqrt(jnp.mean(x * x, axis=-1, keepdims=True) + EPS) * g


def _ctx_attn_kernel(q_ref, k_ref, v_ref, qg_ref, kg_ref, o_ref, nk_ref, nv_ref):
    scale = NA_HEAD_DIM ** -0.5
    for h in range(NA_HEADS):
        cs = slice(h * NA_HEAD_DIM, (h + 1) * NA_HEAD_DIM)
        q = _head_rms(q_ref[0, :, cs].astype(F32), qg_ref[...])
        k = _head_rms(k_ref[0, :, cs].astype(F32), kg_ref[...])
        v = v_ref[0, :, cs].astype(F32)
        nk_ref[0, 0, h] = k
        nv_ref[0, 0, h] = v
        s = lax.dot_general(q.astype(BF16), k.astype(BF16), (((1,), (1,)), ((), ())),
                            preferred_element_type=F32) * scale
        m = jnp.max(s, axis=-1, keepdims=True)
        e = jnp.exp(s - m)
        l = jnp.sum(e, axis=-1, keepdims=True)
        o = jnp.dot(e.astype(BF16), v.astype(BF16), preferred_element_type=F32) / l
        o_ref[0, :, cs] = o.astype(o_ref.dtype)


def ctx_attention(p, q_gain, k_gain):
    b, t, _ = p.shape
    cache_shape = (b, 1, NA_HEADS, t, NA_HEAD_DIM)
    blk = lambda c: pl.BlockSpec((1, t, NA_WIDTH), lambda i, c=c: (i, 0, c))
    gain = pl.BlockSpec((1, NA_HEAD_DIM), lambda i: (0, 0))
    cache_spec = pl.BlockSpec((1, 1, NA_HEADS, t, NA_HEAD_DIM), lambda i: (i, 0, 0, 0, 0))
    return pl.pallas_call(
        _ctx_attn_kernel,
        out_shape=(jax.ShapeDtypeStruct((b, t, NA_WIDTH), BF16),
                   jax.ShapeDtypeStruct(cache_shape, F32),
                   jax.ShapeDtypeStruct(cache_shape, F32)),
        grid=(b,),
        in_specs=[blk(1), blk(2), blk(3), gain, gain],
        out_specs=(pl.BlockSpec((1, t, NA_WIDTH), lambda i: (i, 0, 0)), cache_spec, cache_spec),
        compiler_params=_params(("arbitrary",), 32 << 20),
        name="ctx_attention",
    )(p, p, p, q_gain, k_gain)


NA_BIAS_ROWS = 2 * NA_KH - 1
NA_BIAS_COLS = 2 * NA_KW - 1
NA_MASK_TILE = NA_BIAS_ROWS
NA_QROWS = 4
NA_KROWS = NA_KH + NA_QROWS


def _na_build_bias(rb_ref, bias_ref, head):
    shape = (GRID_W, 2 * GRID_W)
    lane = lax.broadcasted_iota(jnp.int32, shape, 1)
    qc = lax.broadcasted_iota(jnp.int32, shape, 0)
    kc = lane & (GRID_W - 1)
    start = jnp.clip(qc - NA_KW // 2, 0, GRID_W - NA_KW)
    ok = (kc >= start) & (kc < start + NA_KW)
    delta = kc - qc + (NA_KW - 1)
    left = lane < GRID_W
    base = head * (NA_BIAS_ROWS * NA_BIAS_COLS)

    def build(dr, carry):
        acc = jnp.full(shape, NEG_INF, F32)
        for dc in range(NA_BIAS_COLS):
            acc = jnp.where(delta == dc, rb_ref[base + dr * NA_BIAS_COLS + dc], acc)
        tile = jnp.where(ok, acc, NEG_INF)
        bias_ref[0, dr] = jnp.where(left, tile, 0.0)
        bias_ref[1, dr] = jnp.where(left, 0.0, tile)
        return carry

    lax.fori_loop(0, NA_BIAS_ROWS, build, 0)
    bias_ref[0, NA_MASK_TILE] = jnp.where(left, NEG_INF, 0.0)
    bias_ref[1, NA_MASK_TILE] = jnp.where(left, 0.0, NEG_INF)


def _na_kernel(rb_ref, q_ref, k_ref, v_ref, kc_ref, vc_ref, qg_ref, kg_ref, o_ref,
               qs_ref, ks_ref, vs_ref, bias_ref, *, rows):
    @pl.when(pl.program_id(1) == 0)
    def _():
        _na_build_bias(rb_ref, bias_ref, pl.program_id(0))

    scale = NA_HEAD_DIM ** -0.5
    qs_ref[...] = (_head_rms(q_ref[0].astype(F32), qg_ref[...]) * scale).astype(BF16)
    ks_ref[...] = _head_rms(k_ref[0].astype(F32), kg_ref[...]).astype(BF16)
    vs_ref[...] = v_ref[0].astype(BF16)
    kc = kc_ref[0, 0, 0].astype(BF16)
    vc = vc_ref[0, 0, 0].astype(BF16)
    nt = (((1,), (1,)), ((), ()))
    nq = NA_QROWS * GRID_W
    nk = NA_KROWS * GRID_W

    def tile_index(r, rs, kr):
        inside = (kr >= rs) & (kr < rs + NA_KH)
        return jnp.where(inside, kr - r + (NA_KH - 1), NA_MASK_TILE)

    def body(blk, carry):
        r0 = blk * NA_QROWS
        k0 = jnp.clip(r0 - NA_KH // 2, 0, rows - NA_KROWS)
        q = qs_ref[pl.ds(pl.multiple_of(r0 * GRID_W, nq), nq), :]
        kw = ks_ref[pl.ds(pl.multiple_of(k0 * GRID_W, GRID_W), nk), :]
        vw = vs_ref[pl.ds(pl.multiple_of(k0 * GRID_W, GRID_W), nk), :]
        bias_rows = []
        for i in range(NA_QROWS):
            r = r0 + i
            rs = jnp.clip(r - NA_KH // 2, 0, rows - NA_KH)
            pairs = [bias_ref[0, tile_index(r, rs, k0 + 2 * jp)]
                     + bias_ref[1, tile_index(r, rs, k0 + 2 * jp + 1)]
                     for jp in range(NA_KROWS // 2)]
            bias_rows.append(jnp.concatenate(pairs, axis=-1))
        bias = jnp.concatenate(bias_rows, axis=0)
        s_w = lax.dot_general(q, kw, nt, preferred_element_type=F32) + bias
        s_c = lax.dot_general(q, kc, nt, preferred_element_type=F32)
        m = jnp.maximum(jnp.max(s_w, axis=-1, keepdims=True), jnp.max(s_c, axis=-1, keepdims=True))
        e_w = jnp.exp(s_w - m)
        e_c = jnp.exp(s_c - m)
        l = jnp.sum(e_w, axis=-1, keepdims=True) + jnp.sum(e_c, axis=-1, keepdims=True)
        o = (jnp.dot(e_w.astype(BF16), vw, preferred_element_type=F32)
             + jnp.dot(e_c.astype(BF16), vc, preferred_element_type=F32)) / l
        o_ref[0, pl.ds(pl.multiple_of(r0 * GRID_W, nq), nq), :] = o.astype(o_ref.dtype)
        return carry

    lax.fori_loop(0, rows // NA_QROWS, body, 0, unroll=2)


def na_attention(p, cache_k, cache_v, layer_j, rel_bias, q_gain, k_gain):
    b, t, _ = p.shape
    past = cache_k.shape[3]
    rows = t // GRID_W
    hd = NA_HEAD_DIM
    col0 = FNET_WIDTH // hd
    blk = lambda c: pl.BlockSpec((1, t, hd), lambda h, i, rb, c=c: (i, 0, col0 + c * NA_HEADS + h))
    cache_spec = pl.BlockSpec((1, 1, 1, past, hd), lambda h, i, rb: (i, layer_j, h, 0, 0))
    gain = pl.BlockSpec((1, hd), lambda h, i, rb: (0, 0))
    return pl.pallas_call(
        functools.partial(_na_kernel, rows=rows),
        out_shape=jax.ShapeDtypeStruct((b, t, NA_WIDTH), BF16),
        grid_spec=pltpu.PrefetchScalarGridSpec(
            num_scalar_prefetch=1,
            grid=(NA_HEADS, b),
            in_specs=[blk(0), blk(1), blk(2), cache_spec, cache_spec, gain, gain],
            out_specs=pl.BlockSpec((1, t, hd), lambda h, i, rb: (i, 0, h)),
            scratch_shapes=[pltpu.VMEM((t, hd), BF16)] * 3
            + [pltpu.VMEM((2, NA_BIAS_ROWS + 1, GRID_W, 2 * GRID_W), F32)],
        ),
        compiler_params=_params(("arbitrary", "arbitrary"), 40 << 20),
        name="na_attention",
    )(rel_bias.astype(F32).reshape(-1), p, p, p, cache_k, cache_v, q_gain, k_gain)


def _fnet_tables(n1, n2):
    t_len = n1 * n2
    k1 = np.arange(n1)[None, :, None]
    t = n2 * np.arange(n1)[None, None, :] + np.arange(n2)[:, None, None]
    ang = (2.0 * np.pi / t_len) * ((k1 * t) % t_len)
    rows = np.concatenate([np.cos(ang), -np.sin(ang)], axis=1) / math.sqrt(t_len)
    k2 = np.arange(n2)
    ang2 = (2.0 * np.pi / n2) * ((k2[:, None] * k2[None, :]) % n2)
    c2, s2 = np.cos(ang2), np.sin(ang2)
    cols = np.block([[c2, s2], [-s2, c2]])
    c = np.arange(FNET_GROUP_DIM)
    ang3 = (2.0 * np.pi / FNET_GROUP_DIM) * ((c[:, None] * c[None, :]) % FNET_GROUP_DIM)
    chan = np.concatenate([np.cos(ang3), np.sin(ang3)], axis=0) / math.sqrt(FNET_GROUP_DIM)
    return (jnp.asarray(rows, dtype=BF16), jnp.asarray(cols, dtype=BF16),
            jnp.asarray(chan, dtype=BF16))


def _fnet_rows_kernel(tab_ref, *refs, tb, n1):
    x_refs = refs[:tb]
    gr_ref, gi_ref = refs[tb:]
    for e in range(tb):
        g = jnp.dot(tab_ref[e], x_refs[e][0].astype(BF16), preferred_element_type=F32)
        cs = slice(e * FNET_WIDTH, (e + 1) * FNET_WIDTH)
        gr_ref[0, :, cs] = g[:n1].astype(gr_ref.dtype)
        gi_ref[0, :, cs] = g[n1:].astype(gi_ref.dtype)


def _fnet_cols_kernel(gr_ref, gi_ref, w_ref, chan_ref, o_ref, *, kb, n2):
    w = w_ref[...]
    chan = chan_ref[...]
    for kk in range(kb):
        g = jnp.concatenate([gr_ref[0, kk], gi_ref[0, kk]], axis=0)
        z = jnp.dot(w, g, preferred_element_type=F32).astype(BF16)
        stacked = jnp.concatenate(
            [jnp.concatenate([z[:n2, gs * FNET_GROUP_DIM:(gs + 1) * FNET_GROUP_DIM],
                              z[n2:, gs * FNET_GROUP_DIM:(gs + 1) * FNET_GROUP_DIM]], axis=1)
             for gs in range(FNET_GROUPS)], axis=0)
        y = jnp.dot(stacked, chan, preferred_element_type=F32)
        for gs in range(FNET_GROUPS):
            c0 = kk * FNET_WIDTH + gs * FNET_GROUP_DIM
            o_ref[0, :, c0:c0 + FNET_GROUP_DIM] = y[gs * n2:(gs + 1) * n2].astype(o_ref.dtype)


def fourier_mix(p, n1, n2):
    b, t, width = p.shape
    tb = kb = min(n2, 16)
    rows_tab, cols_tab, chan_tab = _fnet_tables(n1, n2)
    wb = width // FNET_WIDTH
    p3 = p.reshape(b, n1, n2 * width)
    x_specs = [pl.BlockSpec((1, n1, FNET_WIDTH), lambda i, s, e=e: (i, 0, wb * (tb * s + e)))
               for e in range(tb)]
    g_shape = jax.ShapeDtypeStruct((b, n1, n2 * FNET_WIDTH), BF16)
    g_spec = pl.BlockSpec((1, n1, tb * FNET_WIDTH), lambda i, s: (i, 0, s))
    gr, gi = pl.pallas_call(
        functools.partial(_fnet_rows_kernel, tb=tb, n1=n1),
        out_shape=(g_shape, g_shape),
        grid=(b, n2 // tb),
        in_specs=[pl.BlockSpec((tb, 2 * n1, n1), lambda i, s: (s, 0, 0))] + x_specs,
        out_specs=(g_spec, g_spec),
        compiler_params=_params(("arbitrary", "arbitrary"), 32 << 20),
        name="fnet_rows",
    )(rows_tab, *([p3] * tb))
    g_in = pl.BlockSpec((1, kb, n2, FNET_WIDTH), lambda i, s: (i, s, 0, 0))
    out = pl.pallas_call(
        functools.partial(_fnet_cols_kernel, kb=kb, n2=n2),
        out_shape=jax.ShapeDtypeStruct((b, n2, n1 * FNET_WIDTH), BF16),
        grid=(b, n1 // kb),
        in_specs=[g_in, g_in,
                  pl.BlockSpec((2 * n2, 2 * n2), lambda i, s: (0, 0)),
                  pl.BlockSpec((2 * FNET_GROUP_DIM, FNET_GROUP_DIM), lambda i, s: (0, 0))],
        out_specs=pl.BlockSpec((1, n2, kb * FNET_WIDTH), lambda i, s: (i, 0, s)),
        compiler_params=_params(("arbitrary", "arbitrary"), 32 << 20),
        name="fnet_cols",
    )(gr.reshape(b, n1, n2, FNET_WIDTH), gi.reshape(b, n1, n2, FNET_WIDTH), cols_tab, chan_tab)
    return out.reshape(b, t, FNET_WIDTH)


def _even_out_kernel(f_ref, a_ref, wf_ref, wa_ref, x_ref, gate_ref, o_ref):
    y = (jnp.dot(f_ref[...], wf_ref[...], preferred_element_type=F32)
         + jnp.dot(a_ref[...], wa_ref[...], preferred_element_type=F32))
    o_ref[...] = x_ref[...] + gate_ref[0] * y


def even_out(f, a, w, x, gate, rows_per_mod, tm, tn):
    m, d = x.shape
    kf = f.shape[1]
    vmem = 2 * 2 * tm * kf * 2 + 2 * 2 * kf * tn * 2 + 4 * tm * tn * 4 + tm * tn * 4 + VMEM_SLACK
    return pl.pallas_call(
        _even_out_kernel,
        out_shape=jax.ShapeDtypeStruct((m, d), F32),
        grid=(d // tn, m // tm),
        in_specs=[
            pl.BlockSpec((tm, kf), lambda j, i: (i, 0)),
            pl.BlockSpec((tm, kf), lambda j, i: (i, 0)),
            pl.BlockSpec((kf, tn), lambda j, i: (0, j)),
            pl.BlockSpec((kf, tn), lambda j, i: (1, j)),
            pl.BlockSpec((tm, tn), lambda j, i: (i, j)),
            pl.BlockSpec((1, 1, tn), lambda j, i: ((i * tm) // rows_per_mod, 0, j)),
        ],
        out_specs=pl.BlockSpec((tm, tn), lambda j, i: (i, j)),
        compiler_params=_params(("arbitrary", "arbitrary"), vmem),
        name="even_out",
    )(f, a, w, w, x, gate)


def _gated_group_norm(o, g):
    o = o.astype(F32)
    oc = o - jnp.mean(o, axis=-1, keepdims=True)
    gn = oc * lax.rsqrt(jnp.mean(oc * oc, axis=-1, keepdims=True) + EPS)
    hg = 0.5 * g.astype(F32)
    return (hg + hg * jnp.tanh(hg)) * gn


def _odd_out_kernel(of_ref, ob_ref, gf_ref, gb_ref, w_ref, x_ref, gate_ref, o_ref):
    acc = None
    for h in range(RET_HEADS):
        cs = slice(h * RET_V_DIM, (h + 1) * RET_V_DIM)
        y = (_gated_group_norm(of_ref[:, cs], gf_ref[:, cs])
             + _gated_group_norm(ob_ref[:, cs], gb_ref[:, cs])).astype(BF16)
        part = jnp.dot(y, w_ref[cs, :], preferred_element_type=F32)
        acc = part if acc is None else acc + part
    o_ref[...] = x_ref[...] + gate_ref[0] * acc


def odd_out(o_f, o_b, p, w, x, gate, rows_per_mod, tm):
    m, d = x.shape
    k = w.shape[0]
    gcol = (2 * RET_QK_WIDTH + RET_V_WIDTH) // RET_V_WIDTH
    vmem = (2 * 4 * tm * k * 2 + k * d * 2 + 4 * tm * d * 4 + 2 * tm * d * 4
            + 6 * tm * RET_V_DIM * 4 + VMEM_SLACK)
    act = lambda c: pl.BlockSpec((tm, k), lambda i, c=c: (i, c))
    return pl.pallas_call(
        _odd_out_kernel,
        out_shape=jax.ShapeDtypeStruct((m, d), F32),
        grid=(m // tm,),
        in_specs=[
            act(0), act(0), act(gcol), act(gcol + 1),
            pl.BlockSpec((k, d), lambda i: (0, 0), pipeline_mode=pl.Buffered(1)),
            pl.BlockSpec((tm, d), lambda i: (i, 0)),
            pl.BlockSpec((1, 1, d), lambda i: ((i * tm) // rows_per_mod, 0, 0)),
        ],
        out_specs=pl.BlockSpec((tm, d), lambda i: (i, 0)),
        compiler_params=_params(("arbitrary",), vmem),
        name="odd_out",
    )(o_f, o_b, p, p, w, x, gate)


def _ffn_kernel(x_ref, sh_ref, sc_ref, gate_ref, g_ref, wg_ref, wu_ref, wd_ref, o_ref, h_ref):
    j = pl.program_id(1)

    @pl.when(j == 0)
    def _():
        _modulate_into(h_ref, x_ref, g_ref, sh_ref, sc_ref)
        o_ref[...] = jnp.zeros_like(o_ref)

    h = h_ref[...]
    a = jnp.dot(h, wg_ref[0], preferred_element_type=F32)
    u = jnp.dot(h, wu_ref[0], preferred_element_type=F32)
    act = (_silu(a) * u).astype(BF16)
    o_ref[...] += jnp.dot(act, wd_ref[0], preferred_element_type=F32)

    @pl.when(j == pl.num_programs(1) - 1)
    def _():
        o_ref[...] = x_ref[...] + gate_ref[0] * o_ref[...]


def ffn_block(x, shift, scale, gate, g, w_gate, w_up, w_down, layer, rows_per_mod, tm, tf):
    m, d = x.shape
    f = w_gate.shape[2]
    vmem = 4 * tm * d * 4 + tm * d * 2 + 2 * 3 * d * tf * 2 + 4 * tm * tf * 4 + tm * d * 4 + VMEM_SLACK
    mod_idx = lambda i, j: ((i * tm) // rows_per_mod, 0, 0)
    mod = pl.BlockSpec((1, 1, d), mod_idx)
    return pl.pallas_call(
        _ffn_kernel,
        out_shape=jax.ShapeDtypeStruct((m, d), F32),
        grid=(m // tm, f // tf),
        in_specs=[
            pl.BlockSpec((tm, d), lambda i, j: (i, 0)),
            mod, mod, mod,
            pl.BlockSpec((1, d), lambda i, j: (0, 0)),
            pl.BlockSpec((1, d, tf), lambda i, j: (layer, 0, j)),
            pl.BlockSpec((1, d, tf), lambda i, j: (layer, 0, j)),
            pl.BlockSpec((1, tf, d), lambda i, j: (layer, j, 0)),
        ],
        out_specs=pl.BlockSpec((tm, d), lambda i, j: (i, 0)),
        scratch_shapes=[pltpu.VMEM((tm, d), BF16)],
        compiler_params=_params(("arbitrary", "arbitrary"), vmem),
        name="ffn_block",
    )(x, shift, scale, gate, g, w_gate, w_up, w_down)


def _retention_kernel(lg_ref, *refs, latent, n_blocks):
    if latent:
        (qf_ref, kf_ref, vf_ref, qb_ref, kb_ref, vb_ref, s0_ref,
         of_ref, ob_ref, s_ref, dec_ref, qd_ref, kd_ref) = refs
    else:
        (qf_ref, kf_ref, vf_ref, qb_ref, kb_ref, vb_ref,
         of_ref, ob_ref, sfin_ref, s_ref, dec_ref, qd_ref, kd_ref) = refs
    h0 = pl.program_id(0) * RET_HEADS_PER_STEP
    i = pl.program_id(2)
    c = RET_BLOCK
    dk, dv = RET_QK_DIM, RET_V_DIM
    scale = RET_QK_DIM ** -0.5
    scans = [(d, hh) for d in range(2) for hh in range(RET_HEADS_PER_STEP)]

    @pl.when((pl.program_id(1) == 0) & (i == 0))
    def _():
        row = lax.broadcasted_iota(jnp.int32, (c, c), 0).astype(F32)
        col = lax.broadcasted_iota(jnp.int32, (c, c), 1).astype(F32)
        t_idx = lax.broadcasted_iota(jnp.int32, (c, 1), 0).astype(F32)
        for d, hh in scans:
            lg = lg_ref[d, h0 + hh]
            diff = row - col if d == 0 else col - row
            dec_ref[d, hh] = jnp.where(diff >= 0, jnp.exp(lg * jnp.maximum(diff, 0.0)), 0.0) * scale
            pos = t_idx if d == 0 else (c - 1.0) - t_idx
            qd_ref[d, hh] = jnp.exp(lg * (pos + 1.0))
            kd_ref[d, hh] = jnp.exp(lg * ((c - 1.0) - pos)) * scale

    @pl.when(i == 0)
    def _():
        for d, hh in scans:
            s_ref[d, hh] = s0_ref[0, 0, d, hh] if latent else jnp.zeros((dk, dv), F32)

    nt = (((1,), (1,)), ((), ()))
    tn = (((0,), (0,)), ((), ()))
    chunk_refs = ((qf_ref, kf_ref, vf_ref, of_ref), (qb_ref, kb_ref, vb_ref, ob_ref))
    for d, hh in scans:
        q_ref, k_ref, v_ref, o_ref = chunk_refs[d]
        q = q_ref[0, :, hh * dk:(hh + 1) * dk]
        k = k_ref[0, :, hh * dk:(hh + 1) * dk]
        v = v_ref[0, :, hh * dv:(hh + 1) * dv]
        a = lax.dot_general(q, k, nt, preferred_element_type=F32)
        inner_w = (a * dec_ref[d, hh]).astype(BF16)
        q_dec = (q.astype(F32) * qd_ref[d, hh]).astype(BF16)
        k_dec = (k.astype(F32) * kd_ref[d, hh]).astype(BF16)
        s_old = s_ref[d, hh]
        o = (jnp.dot(inner_w, v, preferred_element_type=F32)
             + jnp.dot(q_dec, s_old.astype(BF16), preferred_element_type=F32))
        s_new = (s_old * jnp.exp(lg_ref[d, h0 + hh] * c)
                 + lax.dot_general(k_dec, v, tn, preferred_element_type=F32))
        s_ref[d, hh] = s_new
        o_ref[0, :, hh * dv:(hh + 1) * dv] = o.astype(o_ref.dtype)
        if not latent:
            @pl.when(i == n_blocks - 1)
            def _(d=d, hh=hh, s_new=s_new):
                sfin_ref[0, 0, d, hh] = s_new


def retention(p, log_gamma, state0, layer_j):
    b, t, _ = p.shape
    latent = state0 is not None
    c = RET_BLOCK
    nb = t // c
    hps = RET_HEADS_PER_STEP
    dk, dv, nh = hps * RET_QK_DIM, hps * RET_V_DIM, RET_HEADS // hps

    def chunk_specs(blk_of):
        return [
            pl.BlockSpec((1, c, dk), lambda h, bi, i, lg: (bi, blk_of(i), h)),
            pl.BlockSpec((1, c, dk), lambda h, bi, i, lg: (bi, blk_of(i), nh + h)),
            pl.BlockSpec((1, c, dv), lambda h, bi, i, lg: (bi, blk_of(i), nh + h)),
        ]

    fwd = lambda i: i
    bwd = lambda i: nb - 1 - i
    in_specs = chunk_specs(fwd) + chunk_specs(bwd)
    args = [p] * 6
    o_shape = jax.ShapeDtypeStruct((b, t, RET_V_WIDTH), BF16)
    out_shape = [o_shape, o_shape]
    out_specs = [pl.BlockSpec((1, c, dv), lambda h, bi, i, lg: (bi, fwd(i), h)),
                 pl.BlockSpec((1, c, dv), lambda h, bi, i, lg: (bi, bwd(i), h))]
    state_dims = (RET_QK_DIM, RET_V_DIM)
    state_block = (1, 1, 2, hps) + state_dims
    if latent:
        in_specs.append(pl.BlockSpec(state_block, lambda h, bi, i, lg: (bi, layer_j, 0, h, 0, 0)))
        args.append(state0)
    else:
        out_shape.append(jax.ShapeDtypeStruct((b, 1, 2, RET_HEADS) + state_dims, F32))
        out_specs.append(pl.BlockSpec(state_block, lambda h, bi, i, lg: (bi, 0, 0, h, 0, 0)))
    return pl.pallas_call(
        functools.partial(_retention_kernel, latent=latent, n_blocks=nb),
        out_shape=tuple(out_shape),
        grid_spec=pltpu.PrefetchScalarGridSpec(
            num_scalar_prefetch=1,
            grid=(nh, b, nb),
            in_specs=in_specs,
            out_specs=tuple(out_specs),
            scratch_shapes=[pltpu.VMEM((2, hps) + state_dims, F32), pltpu.VMEM((2, hps, c, c), F32),
                            pltpu.VMEM((2, hps, c, 1), F32), pltpu.VMEM((2, hps, c, 1), F32)],
        ),
        compiler_params=_params(("arbitrary",) * 3, 40 << 20),
        name="retention",
    )(log_gamma, *args)


def _trunk(x, mods, ctx_k, ctx_v, ctx_state, wts, p_dtype):
    b, t, d = x.shape
    is_ctx = ctx_k is None
    nbm = mods.shape[1]
    rows_per_mod = (b * t) // nbm
    x2 = x.reshape(b * t, d)
    new_k = new_v = new_s = None
    for i in range(DEPTH):
        mod = [mods[i, :, k][:, None, :] for k in range(6)]
        j = i // 2
        if i % 2 == 0:
            p = modproj(x2, mod[0], mod[1], wts["norm_g"][i, 0][None], wts["even_w_in"][j],
                        rows_per_mod, p_dtype, tm=1024, tn=512).reshape(b, t, EVEN_IN)
            qg, kg = wts["even_q_norm"][j][None], wts["even_k_norm"][j][None]
            if is_ctx:
                attn, new_k, new_v = ctx_attention(p, qg, kg)
            else:
                attn = na_attention(p, ctx_k, ctx_v, j, wts["na_rel_bias"][j], qg, kg)
            n2 = GRID_W if not is_ctx else math.isqrt(t)
            fm = fourier_mix(p, t // n2, n2)
            x2 = even_out(fm.reshape(b * t, FNET_WIDTH), attn.reshape(b * t, NA_WIDTH),
                          wts["even_w_out"][j], x2, mod[2], rows_per_mod, tm=512, tn=1024)
        else:
            rope = (t, 2 * RET_QK_WIDTH) if not is_ctx else None
            p = modproj(x2, mod[0], mod[1], wts["norm_g"][i, 0][None], wts["odd_w_in"][j],
                        rows_per_mod, BF16, tm=1024, tn=512, rope=rope)
            lg = jax.nn.log_sigmoid(wts["ret_decay_logit"][j].astype(F32))
            if is_ctx:
                o_f, o_b, new_s = retention(p.reshape(b, t, ODD_IN), lg, None, j)
            else:
                o_f, o_b = retention(p.reshape(b, t, ODD_IN), lg, ctx_state, j)
            x2 = odd_out(o_f.reshape(b * t, RET_V_WIDTH), o_b.reshape(b * t, RET_V_WIDTH), p,
                         wts["odd_w_out"][j], x2, mod[2], rows_per_mod, tm=256)
        x2 = ffn_block(x2, mod[3], mod[4], mod[5], wts["norm_g"][i, 1][None], wts["ffn_w_gate"],
                       wts["ffn_w_up"], wts["ffn_w_down"], i, rows_per_mod, tm=512, tf=512)
    return x2.reshape(b, t, d), new_k, new_v, new_s


def kernel(x_prompt, x_sample, cache_k, cache_v, state_ret, c, c_ctx, ada_w, ada_b, norm_g,
           even_w_in, even_q_norm, even_k_norm, na_rel_bias, even_w_out, odd_w_in, ret_decay_logit,
           odd_w_out, ffn_w_gate, ffn_w_up, ffn_w_down):
    nb_lat = c.shape[0]
    cond = jnp.concatenate(
        [c, c_ctx[None, :], jnp.zeros((MOD_ROWS - nb_lat - 1, D_MODEL), F32)], axis=0)
    mods = adaln_all(cond, ada_w, ada_b).reshape(DEPTH, MOD_ROWS, 6, D_MODEL)
    wts = dict(
        norm_g=norm_g, even_q_norm=even_q_norm, even_k_norm=even_k_norm, na_rel_bias=na_rel_bias,
        ret_decay_logit=ret_decay_logit,
        even_w_in=even_w_in.astype(BF16), even_w_out=even_w_out.astype(BF16),
        odd_w_in=odd_w_in.astype(BF16), odd_w_out=odd_w_out.astype(BF16),
        ffn_w_gate=ffn_w_gate.astype(BF16), ffn_w_up=ffn_w_up.astype(BF16),
        ffn_w_down=ffn_w_down.astype(BF16),
    )
    y_prompt, new_k, new_v, new_s = _trunk(
        x_prompt, mods[:, nb_lat:nb_lat + 1], None, None, None, wts, F32)
    y_sample, _, _, _ = _trunk(x_sample, mods[:, :nb_lat], cache_k, cache_v, state_ret, wts, BF16)
    return (y_prompt, y_sample, new_k, new_v, new_s)
```

```python
import functools
import math

import numpy as np
import jax
import jax.numpy as jnp
from jax import lax
from jax.experimental import pallas as pl
from jax.experimental.pallas import tpu as pltpu

F32 = jnp.float32
BF16 = jnp.bfloat16

D_MODEL = 2048
DEPTH = 2
GRID_W = 64
EPS = 1e-6
NEG_INF = -1e30
FNET_GROUPS = 8
FNET_GROUP_DIM = 128
FNET_WIDTH = 1024
NA_HEADS = 8
NA_HEAD_DIM = 128
NA_WIDTH = 1024
NA_KH = 8
NA_KW = 16
EVEN_IN = FNET_WIDTH + 3 * NA_WIDTH
RET_HEADS = 8
RET_QK_DIM = 256
RET_V_DIM = 512
RET_QK_WIDTH = RET_HEADS * RET_QK_DIM
RET_V_WIDTH = RET_HEADS * RET_V_DIM
ODD_IN = 2 * RET_QK_WIDTH + 3 * RET_V_WIDTH
RET_BLOCK = 256
RET_HEADS_PER_STEP = 2
ROPE_BASE = 10000.0
D_FF = 5632
MOD_ROWS = 8
MOD_CHUNK = 128

V7X_VMEM_BUDGET = 56 * 1024 * 1024
VMEM_SLACK = 8 * 1024 * 1024


def _params(semantics, vmem_bytes):
    return pltpu.CompilerParams(dimension_semantics=semantics,
                                vmem_limit_bytes=min(int(vmem_bytes), V7X_VMEM_BUDGET))


def _silu(x):
    return x * (1.0 / (1.0 + jnp.exp(-x)))


def _adaln_kernel(c_ref, w_ref, b_ref, o_ref):
    s = _silu(c_ref[...]).astype(BF16)
    w = w_ref[0].astype(BF16)
    o_ref[0] = jnp.dot(s, w, preferred_element_type=F32) + b_ref[0]


def adaln_all(cond, ada_w, ada_b):
    n = ada_w.shape[-1]
    tn = 1024
    return pl.pallas_call(
        _adaln_kernel,
        out_shape=jax.ShapeDtypeStruct((DEPTH, MOD_ROWS, n), F32),
        grid=(DEPTH, n // tn),
        in_specs=[
            pl.BlockSpec((MOD_ROWS, D_MODEL), lambda l, j: (0, 0)),
            pl.BlockSpec((1, D_MODEL, tn), lambda l, j: (l, 0, j)),
            pl.BlockSpec((1, 1, tn), lambda l, j: (l, 0, j)),
        ],
        out_specs=pl.BlockSpec((1, MOD_ROWS, tn), lambda l, j: (l, 0, j)),
        compiler_params=_params(("arbitrary", "arbitrary"), 40 << 20),
        name="adaln",
    )(cond, ada_w, ada_b.reshape(DEPTH, 1, n))


def _modulate_into(h_ref, x_ref, g_ref, sh_ref, sc_ref):
    g = g_ref[...]
    shift = sh_ref[0]
    scale1 = 1.0 + sc_ref[0]

    def body(c, carry):
        rows = pl.ds(pl.multiple_of(c * MOD_CHUNK, MOD_CHUNK), MOD_CHUNK)
        x = x_ref[rows, :]
        ms = jnp.mean(x * x, axis=-1, keepdims=True)
        h_ref[rows, :] = (x * lax.rsqrt(ms + EPS) * g * scale1 + shift).astype(h_ref.dtype)
        return carry

    lax.fori_loop(0, x_ref.shape[0] // MOD_CHUNK, body, 0)


def _rope_tables(t_len):
    half = RET_QK_DIM // 2
    nf = half // 2
    inv = ROPE_BASE ** (-jnp.arange(nf, dtype=F32) / nf)
    t = jnp.arange(t_len)
    rows = (t // GRID_W).astype(F32)
    cols = (t % GRID_W).astype(F32)

    def tabs(pos):
        ang = pos[:, None] * inv[None, :]
        c, s = jnp.cos(ang), jnp.sin(ang)
        return jnp.concatenate([c, c], axis=-1), jnp.concatenate([-s, s], axis=-1)

    cr, sr = tabs(rows)
    cc, sc = tabs(cols)
    return jnp.concatenate([cr, cc], axis=-1), jnp.concatenate([sr, sc], axis=-1)


def _rope(x, cos, sin):
    half = RET_QK_DIM // 2
    swapped = [pltpu.roll(x[:, s * half:(s + 1) * half], half // 2, axis=1) for s in range(2)]
    return x * cos + jnp.concatenate(swapped, axis=-1) * sin


def _modproj_kernel(*refs, rope_tiles):
    if rope_tiles:
        x_ref, sh_ref, sc_ref, g_ref, w_ref, cos_ref, sin_ref, o_ref, h_ref = refs
    else:
        x_ref, sh_ref, sc_ref, g_ref, w_ref, o_ref, h_ref = refs
    j = pl.program_id(1)

    @pl.when(j == 0)
    def _():
        _modulate_into(h_ref, x_ref, g_ref, sh_ref, sc_ref)

    acc = jnp.dot(h_ref[...], w_ref[...], preferred_element_type=F32)
    if not rope_tiles:
        o_ref[...] = acc.astype(o_ref.dtype)
        return

    @pl.when(j < rope_tiles)
    def _():
        cos = cos_ref[...]
        sin = sin_ref[...]
        for s in range(acc.shape[1] // RET_QK_DIM):
            cs = slice(s * RET_QK_DIM, (s + 1) * RET_QK_DIM)
            o_ref[:, cs] = _rope(acc[:, cs], cos, sin).astype(o_ref.dtype)

    @pl.when(j >= rope_tiles)
    def _():
        o_ref[...] = acc.astype(o_ref.dtype)


def modproj(x, shift, scale, g, w, rows_per_mod, out_dtype, tm, tn, rope=None):
    m, d = x.shape
    n = w.shape[1]
    osz = jnp.dtype(out_dtype).itemsize
    vmem = 2 * tm * d * 4 + 2 * d * tn * 2 + 2 * tm * tn * osz + tm * d * 2 + 2 * tm * tn * 4 + VMEM_SLACK
    mod_idx = lambda i, j: ((i * tm) // rows_per_mod, 0, 0)
    in_specs = [
        pl.BlockSpec((tm, d), lambda i, j: (i, 0)),
        pl.BlockSpec((1, 1, d), mod_idx),
        pl.BlockSpec((1, 1, d), mod_idx),
        pl.BlockSpec((1, d), lambda i, j: (0, 0)),
        pl.BlockSpec((d, tn), lambda i, j: (0, j)),
    ]
    args = [x, shift, scale, g, w]
    rope_tiles = 0
    if rope is not None:
        seq_len, n_cols = rope
        rope_tiles = n_cols // tn
        cos, sin = _rope_tables(seq_len)
        tab = pl.BlockSpec((tm, RET_QK_DIM), lambda i, j: (i % (seq_len // tm), 0))
        in_specs += [tab, tab]
        args += [cos, sin]
        vmem += 2 * 2 * tm * RET_QK_DIM * 4
    return pl.pallas_call(
        functools.partial(_modproj_kernel, rope_tiles=rope_tiles),
        out_shape=jax.ShapeDtypeStruct((m, n), out_dtype),
        grid=(m // tm, n // tn),
        in_specs=in_specs,
        out_specs=pl.BlockSpec((tm, tn), lambda i, j: (i, j)),
        scratch_shapes=[pltpu.VMEM((tm, d), BF16)],
        compiler_params=_params(("arbitrary", "arbitrary"), vmem),
        name="modproj",
    )(*args)


def _head_rms(x, g):
    return x * lax.rsqrt(jnp.mean(x * x, axis=-1, keepdims=True) + EPS) * g


def _ctx_attn_kernel(q_ref, k_ref, v_ref, qg_ref, kg_ref, o_ref, nk_ref, nv_ref):
    scale = NA_HEAD_DIM ** -0.5
    for h in range(NA_HEADS):
        cs = slice(h * NA_HEAD_DIM, (h + 1) * NA_HEAD_DIM)
        q = _head_rms(q_ref[0, :, cs].astype(F32), qg_ref[...])
        k = _head_rms(k_ref[0, :, cs].astype(F32), kg_ref[...])
        v = v_ref[0, :, cs].astype(F32)
        nk_ref[0, 0, h] = k
        nv_ref[0, 0, h] = v
        s = lax.dot_general(q.astype(BF16), k.astype(BF16), (((1,), (1,)), ((), ())),
                            preferred_element_type=F32) * scale
        m = jnp.max(s, axis=-1, keepdims=True)
        e = jnp.exp(s - m)
        l = jnp.sum(e, axis=-1, keepdims=True)
        o = jnp.dot(e.astype(BF16), v.astype(BF16), preferred_element_type=F32) / l
        o_ref[0, :, cs] = o.astype(o_ref.dtype)


def ctx_attention(p, q_gain, k_gain):
    b, t, _ = p.shape
    cache_shape = (b, 1, NA_HEADS, t, NA_HEAD_DIM)
    blk = lambda c: pl.BlockSpec((1, t, NA_WIDTH), lambda i, c=c: (i, 0, c))
    gain = pl.BlockSpec((1, NA_HEAD_DIM), lambda i: (0, 0))
    cache_spec = pl.BlockSpec((1, 1, NA_HEADS, t, NA_HEAD_DIM), lambda i: (i, 0, 0, 0, 0))
    return pl.pallas_call(
        _ctx_attn_kernel,
        out_shape=(jax.ShapeDtypeStruct((b, t, NA_WIDTH), BF16),
                   jax.ShapeDtypeStruct(cache_shape, F32),
                   jax.ShapeDtypeStruct(cache_shape, F32)),
        grid=(b,),
        in_specs=[blk(1), blk(2), blk(3), gain, gain],
        out_specs=(pl.BlockSpec((1, t, NA_WIDTH), lambda i: (i, 0, 0)), cache_spec, cache_spec),
        compiler_params=_params(("arbitrary",), 32 << 20),
        name="ctx_attention",
    )(p, p, p, q_gain, k_gain)


NA_BIAS_ROWS = 2 * NA_KH - 1
NA_BIAS_COLS = 2 * NA_KW - 1
NA_MASK_TILE = NA_BIAS_ROWS
NA_QROWS = 4
NA_KROWS = NA_KH + NA_QROWS


def _na_build_bias(rb_ref, bias_ref, head):
    shape = (GRID_W, 2 * GRID_W)
    lane = lax.broadcasted_iota(jnp.int32, shape, 1)
    qc = lax.broadcasted_iota(jnp.int32, shape, 0)
    kc = lane & (GRID_W - 1)
    start = jnp.clip(qc - NA_KW // 2, 0, GRID_W - NA_KW)
    ok = (kc >= start) & (kc < start + NA_KW)
    delta = kc - qc + (NA_KW - 1)
    left = lane < GRID_W
    base = head * (NA_BIAS_ROWS * NA_BIAS_COLS)

    def build(dr, carry):
        acc = jnp.full(shape, NEG_INF, F32)
        for dc in range(NA_BIAS_COLS):
            acc = jnp.where(delta == dc, rb_ref[base + dr * NA_BIAS_COLS + dc], acc)
        tile = jnp.where(ok, acc, NEG_INF)
        bias_ref[0, dr] = jnp.where(left, tile, 0.0)
        bias_ref[1, dr] = jnp.where(left, 0.0, tile)
        return carry

    lax.fori_loop(0, NA_BIAS_ROWS, build, 0)
    bias_ref[0, NA_MASK_TILE] = jnp.where(left, NEG_INF, 0.0)
    bias_ref[1, NA_MASK_TILE] = jnp.where(left, 0.0, NEG_INF)


def _na_kernel(rb_ref, q_ref, k_ref, v_ref, kc_ref, vc_ref, qg_ref, kg_ref, o_ref,
               qs_ref, ks_ref, vs_ref, bias_ref, *, rows):
    @pl.when(pl.program_id(1) == 0)
    def _():
        _na_build_bias(rb_ref, bias_ref, pl.program_id(0))

    scale = NA_HEAD_DIM ** -0.5
    qs_ref[...] = (_head_rms(q_ref[0].astype(F32), qg_ref[...]) * scale).astype(BF16)
    ks_ref[...] = _head_rms(k_ref[0].astype(F32), kg_ref[...]).astype(BF16)
    vs_ref[...] = v_ref[0].astype(BF16)
    kc = kc_ref[0, 0, 0].astype(BF16)
    vc = vc_ref[0, 0, 0].astype(BF16)
    nt = (((1,), (1,)), ((), ()))
    nq = NA_QROWS * GRID_W
    nk = NA_KROWS * GRID_W

    def tile_index(r, rs, kr):
        inside = (kr >= rs) & (kr < rs + NA_KH)
        return jnp.where(inside, kr - r + (NA_KH - 1), NA_MASK_TILE)

    def body(blk, carry):
        r0 = blk * NA_QROWS
        k0 = jnp.clip(r0 - NA_KH // 2, 0, rows - NA_KROWS)
        q = qs_ref[pl.ds(pl.multiple_of(r0 * GRID_W, nq), nq), :]
        kw = ks_ref[pl.ds(pl.multiple_of(k0 * GRID_W, GRID_W), nk), :]
        vw = vs_ref[pl.ds(pl.multiple_of(k0 * GRID_W, GRID_W), nk), :]
        bias_rows = []
        for i in range(NA_QROWS):
            r = r0 + i
            rs = jnp.clip(r - NA_KH // 2, 0, rows - NA_KH)
            pairs = [bias_ref[0, tile_index(r, rs, k0 + 2 * jp)]
                     + bias_ref[1, tile_index(r, rs, k0 + 2 * jp + 1)]
                     for jp in range(NA_KROWS // 2)]
            bias_rows.append(jnp.concatenate(pairs, axis=-1))
        bias = jnp.concatenate(bias_rows, axis=0)
        s_w = lax.dot_general(q, kw, nt, preferred_element_type=F32) + bias
        s_c = lax.dot_general(q, kc, nt, preferred_element_type=F32)
        m = jnp.maximum(jnp.max(s_w, axis=-1, keepdims=True), jnp.max(s_c, axis=-1, keepdims=True))
        e_w = jnp.exp(s_w - m)
        e_c = jnp.exp(s_c - m)
        l = jnp.sum(e_w, axis=-1, keepdims=True) + jnp.sum(e_c, axis=-1, keepdims=True)
        o = (jnp.dot(e_w.astype(BF16), vw, preferred_element_type=F32)
             + jnp.dot(e_c.astype(BF16), vc, preferred_element_type=F32)) / l
        o_ref[0, pl.ds(pl.multiple_of(r0 * GRID_W, nq), nq), :] = o.astype(o_ref.dtype)
        return carry

    lax.fori_loop(0, rows // NA_QROWS, body, 0, unroll=2)


def na_attention(p, cache_k, cache_v, layer_j, rel_bias, q_gain, k_gain):
    b, t, _ = p.shape
    past = cache_k.shape[3]
    rows = t // GRID_W
    hd = NA_HEAD_DIM
    col0 = FNET_WIDTH // hd
    blk = lambda c: pl.BlockSpec((1, t, hd), lambda h, i, rb, c=c: (i, 0, col0 + c * NA_HEADS + h))
    cache_spec = pl.BlockSpec((1, 1, 1, past, hd), lambda h, i, rb: (i, layer_j, h, 0, 0))
    gain = pl.BlockSpec((1, hd), lambda h, i, rb: (0, 0))
    return pl.pallas_call(
        functools.partial(_na_kernel, rows=rows),
        out_shape=jax.ShapeDtypeStruct((b, t, NA_WIDTH), BF16),
        grid_spec=pltpu.PrefetchScalarGridSpec(
            num_scalar_prefetch=1,
            grid=(NA_HEADS, b),
            in_specs=[blk(0), blk(1), blk(2), cache_spec, cache_spec, gain, gain],
            out_specs=pl.BlockSpec((1, t, hd), lambda h, i, rb: (i, 0, h)),
            scratch_shapes=[pltpu.VMEM((t, hd), BF16)] * 3
            + [pltpu.VMEM((2, NA_BIAS_ROWS + 1, GRID_W, 2 * GRID_W), F32)],
        ),
        compiler_params=_params(("arbitrary", "arbitrary"), 40 << 20),
        name="na_attention",
    )(rel_bias.astype(F32).reshape(-1), p, p, p, cache_k, cache_v, q_gain, k_gain)


FNET_BLOCK = 16


def _fnet_tables(n1, n2):
    t_len = n1 * n2
    tb = FNET_BLOCK
    eye = np.eye(tb)
    k1 = np.arange(n1)
    ang1 = (2.0 * np.pi / n1) * ((k1[:, None] * k1[None, :]) % n1)
    f1 = np.concatenate([np.cos(ang1), -np.sin(ang1)], axis=0) / math.sqrt(t_len)
    rows_mat = np.kron(f1, eye)
    t2 = np.arange(n2)
    tw = (2.0 * np.pi / t_len) * (k1[:, None] * t2[None, :])
    tw = tw.reshape(n1, n2 // tb, tb).transpose(1, 0, 2).reshape(n2 // tb, n1 * tb, 1)
    c = np.arange(FNET_GROUP_DIM)
    ang3 = (2.0 * np.pi / FNET_GROUP_DIM) * ((c[:, None] * c[None, :]) % FNET_GROUP_DIM)
    c3, s3 = np.cos(ang3), np.sin(ang3)
    chan = np.block([[c3, -s3], [s3, c3]]) / math.sqrt(FNET_GROUP_DIM)
    ang2 = (2.0 * np.pi / n2) * ((t2[:, None] * t2[None, :]) % n2)
    cols_re = np.einsum("pq,kt->kpqt", eye, np.cos(ang2)).reshape(n2 * tb, tb * n2)
    cols_im = np.einsum("pq,kt->kpqt", eye, np.sin(ang2)).reshape(n2 * tb, tb * n2)
    bf = lambda x: jnp.asarray(x, dtype=BF16)
    f32 = lambda x: jnp.asarray(x, dtype=F32)
    return bf(rows_mat), f32(np.cos(tw)), f32(np.sin(tw)), bf(chan), bf(cols_re), bf(cols_im)


def _fnet_rows_kernel(x_ref, mat_ref, twc_ref, tws_ref, chan_ref, vr_ref, vi_ref):
    _, n1, tb, width = x_ref.shape
    rows = n1 * tb
    gd = FNET_GROUP_DIM
    x = x_ref[0].reshape(rows, width).astype(BF16)
    h = jnp.dot(mat_ref[...], x, preferred_element_type=F32)
    hr, hi = h[:rows], h[rows:]
    c, s = twc_ref[0], tws_ref[0]
    gr = (hr * c + hi * s).astype(BF16)
    gi = (hi * c - hr * s).astype(BF16)
    stacked = jnp.concatenate(
        [jnp.concatenate([gr[:, g * gd:(g + 1) * gd], gi[:, g * gd:(g + 1) * gd]], axis=1)
         for g in range(FNET_GROUPS)], axis=0)
    v = jnp.dot(stacked, chan_ref[...], preferred_element_type=F32)
    for g in range(FNET_GROUPS):
        part = v[g * rows:(g + 1) * rows]
        vr_ref[0, :, :, g * gd:(g + 1) * gd] = part[:, :gd].reshape(n1, tb, gd).astype(vr_ref.dtype)
        vi_ref[0, :, :, g * gd:(g + 1) * gd] = part[:, gd:].reshape(n1, tb, gd).astype(vi_ref.dtype)


def _fnet_cols_kernel(vr_ref, vi_ref, re_ref, im_ref, o_ref):
    _, kb, n2, width = vr_ref.shape
    vr = vr_ref[0].reshape(kb * n2, width)
    vi = vi_ref[0].reshape(kb * n2, width)
    y = (jnp.dot(re_ref[...], vr, preferred_element_type=F32)
         + jnp.dot(im_ref[...], vi, preferred_element_type=F32))
    o_ref[0] = y.reshape(n2, kb, width).astype(o_ref.dtype)


def fourier_mix(p, n1, n2):
    b, t, width = p.shape
    tb = FNET_BLOCK
    w = FNET_WIDTH
    rows_mat, twc, tws, chan, cols_re, cols_im = _fnet_tables(n1, n2)
    const = lambda shape: pl.BlockSpec(shape, lambda i, s: (0,) * len(shape))
    tw_spec = pl.BlockSpec((1, n1 * tb, 1), lambda i, s: (s, 0, 0))
    v_shape = jax.ShapeDtypeStruct((b, n1, n2, w), BF16)
    v_spec = pl.BlockSpec((1, n1, tb, w), lambda i, s: (i, 0, s, 0))
    isz = p.dtype.itemsize
    vmem = (2 * n1 * tb * w * isz + 2 * rows_mat.size * 2 + 4 * n1 * tb * w * 2
            + 6 * n1 * tb * w * 4 + VMEM_SLACK)
    vr, vi = pl.pallas_call(
        _fnet_rows_kernel,
        out_shape=(v_shape, v_shape),
        grid=(b, n2 // tb),
        in_specs=[pl.BlockSpec((1, n1, tb, w), lambda i, s: (i, 0, s, 0)),
                  const(rows_mat.shape), tw_spec, tw_spec, const(chan.shape)],
        out_specs=(v_spec, v_spec),
        compiler_params=_params(("arbitrary", "arbitrary"), vmem),
        name="fnet_rows",
    )(p.reshape(b, n1, n2, width), rows_mat, twc, tws, chan)
    v_in = pl.BlockSpec((1, tb, n2, w), lambda i, s: (i, s, 0, 0))
    vmem = 2 * 2 * tb * n2 * w * 2 + 2 * 2 * cols_re.size * 2 + 2 * n2 * tb * w * 2 + 3 * n2 * tb * w * 4 + VMEM_SLACK
    out = pl.pallas_call(
        _fnet_cols_kernel,
        out_shape=jax.ShapeDtypeStruct((b, n2, n1, w), BF16),
        grid=(b, n1 // tb),
        in_specs=[v_in, v_in, const(cols_re.shape), const(cols_im.shape)],
        out_specs=pl.BlockSpec((1, n2, tb, w), lambda i, s: (i, 0, s, 0)),
        compiler_params=_params(("arbitrary", "arbitrary"), vmem),
        name="fnet_cols",
    )(vr, vi, cols_re, cols_im)
    return out.reshape(b, t, w)


def _even_out_kernel(f_ref, a_ref, wf_ref, wa_ref, x_ref, gate_ref, o_ref):
    y = (jnp.dot(f_ref[...], wf_ref[...], preferred_element_type=F32)
         + jnp.dot(a_ref[...], wa_ref[...], preferred_element_type=F32))
    o_ref[...] = x_ref[...] + gate_ref[0] * y


def even_out(f, a, w, x, gate, rows_per_mod, tm, tn):
    m, d = x.shape
    kf = f.shape[1]
    vmem = 2 * 2 * tm * kf * 2 + 2 * 2 * kf * tn * 2 + 4 * tm * tn * 4 + tm * tn * 4 + VMEM_SLACK
    return pl.pallas_call(
        _even_out_kernel,
        out_shape=jax.ShapeDtypeStruct((m, d), F32),
        grid=(d // tn, m // tm),
        in_specs=[
            pl.BlockSpec((tm, kf), lambda j, i: (i, 0)),
            pl.BlockSpec((tm, kf), lambda j, i: (i, 0)),
            pl.BlockSpec((kf, tn), lambda j, i: (0, j)),
            pl.BlockSpec((kf, tn), lambda j, i: (1, j)),
            pl.BlockSpec((tm, tn), lambda j, i: (i, j)),
            pl.BlockSpec((1, 1, tn), lambda j, i: ((i * tm) // rows_per_mod, 0, j)),
        ],
        out_specs=pl.BlockSpec((tm, tn), lambda j, i: (i, j)),
        compiler_params=_params(("arbitrary", "arbitrary"), vmem),
        name="even_out",
    )(f, a, w, w, x, gate)


def _gated_group_norm(o, g):
    o = o.astype(F32)
    oc = o - jnp.mean(o, axis=-1, keepdims=True)
    gn = oc * lax.rsqrt(jnp.mean(oc * oc, axis=-1, keepdims=True) + EPS)
    hg = 0.5 * g.astype(F32)
    return (hg + hg * jnp.tanh(hg)) * gn


def _odd_out_kernel(of_ref, ob_ref, gf_ref, gb_ref, w_ref, x_ref, gate_ref, o_ref):
    acc = None
    for h in range(RET_HEADS):
        cs = slice(h * RET_V_DIM, (h + 1) * RET_V_DIM)
        y = (_gated_group_norm(of_ref[:, cs], gf_ref[:, cs])
             + _gated_group_norm(ob_ref[:, cs], gb_ref[:, cs])).astype(BF16)
        part = jnp.dot(y, w_ref[cs, :], preferred_element_type=F32)
        acc = part if acc is None else acc + part
    o_ref[...] = x_ref[...] + gate_ref[0] * acc


def odd_out(o_f, o_b, p, w, x, gate, rows_per_mod, tm):
    m, d = x.shape
    k = w.shape[0]
    gcol = (2 * RET_QK_WIDTH + RET_V_WIDTH) // RET_V_WIDTH
    vmem = (2 * 4 * tm * k * 2 + k * d * 2 + 4 * tm * d * 4 + 2 * tm * d * 4
            + 6 * tm * RET_V_DIM * 4 + VMEM_SLACK)
    act = lambda c: pl.BlockSpec((tm, k), lambda i, c=c: (i, c))
    return pl.pallas_call(
        _odd_out_kernel,
        out_shape=jax.ShapeDtypeStruct((m, d), F32),
        grid=(m // tm,),
        in_specs=[
            act(0), act(0), act(gcol), act(gcol + 1),
            pl.BlockSpec((k, d), lambda i: (0, 0), pipeline_mode=pl.Buffered(1)),
            pl.BlockSpec((tm, d), lambda i: (i, 0)),
            pl.BlockSpec((1, 1, d), lambda i: ((i * tm) // rows_per_mod, 0, 0)),
        ],
        out_specs=pl.BlockSpec((tm, d), lambda i: (i, 0)),
        compiler_params=_params(("arbitrary",), vmem),
        name="odd_out",
    )(o_f, o_b, p, p, w, x, gate)


def _ffn_kernel(x_ref, sh_ref, sc_ref, gate_ref, g_ref, wg_ref, wu_ref, wd_ref, o_ref, h_ref):
    j = pl.program_id(1)

    @pl.when(j == 0)
    def _():
        _modulate_into(h_ref, x_ref, g_ref, sh_ref, sc_ref)
        o_ref[...] = jnp.zeros_like(o_ref)

    h = h_ref[...]
    a = jnp.dot(h, wg_ref[0], preferred_element_type=F32)
    u = jnp.dot(h, wu_ref[0], preferred_element_type=F32)
    act = (_silu(a) * u).astype(BF16)
    o_ref[...] += jnp.dot(act, wd_ref[0], preferred_element_type=F32)

    @pl.when(j == pl.num_programs(1) - 1)
    def _():
        o_ref[...] = x_ref[...] + gate_ref[0] * o_ref[...]


def ffn_block(x, shift, scale, gate, g, w_gate, w_up, w_down, layer, rows_per_mod, tm, tf):
    m, d = x.shape
    f = w_gate.shape[2]
    vmem = 4 * tm * d * 4 + tm * d * 2 + 2 * 3 * d * tf * 2 + 4 * tm * tf * 4 + tm * d * 4 + VMEM_SLACK
    mod_idx = lambda i, j: ((i * tm) // rows_per_mod, 0, 0)
    mod = pl.BlockSpec((1, 1, d), mod_idx)
    return pl.pallas_call(
        _ffn_kernel,
        out_shape=jax.ShapeDtypeStruct((m, d), F32),
        grid=(m // tm, f // tf),
        in_specs=[
            pl.BlockSpec((tm, d), lambda i, j: (i, 0)),
            mod, mod, mod,
            pl.BlockSpec((1, d), lambda i, j: (0, 0)),
            pl.BlockSpec((1, d, tf), lambda i, j: (layer, 0, j)),
            pl.BlockSpec((1, d, tf), lambda i, j: (layer, 0, j)),
            pl.BlockSpec((1, tf, d), lambda i, j: (layer, j, 0)),
        ],
        out_specs=pl.BlockSpec((tm, d), lambda i, j: (i, 0)),
        scratch_shapes=[pltpu.VMEM((tm, d), BF16)],
        compiler_params=_params(("arbitrary", "arbitrary"), vmem),
        name="ffn_block",
    )(x, shift, scale, gate, g, w_gate, w_up, w_down)


def _retention_kernel(lg_ref, *refs, latent, n_blocks):
    if latent:
        (qf_ref, kf_ref, vf_ref, qb_ref, kb_ref, vb_ref, s0_ref,
         of_ref, ob_ref, s_ref, dec_ref, qd_ref, kd_ref) = refs
    else:
        (qf_ref, kf_ref, vf_ref, qb_ref, kb_ref, vb_ref,
         of_ref, ob_ref, sfin_ref, s_ref, dec_ref, qd_ref, kd_ref) = refs
    h0 = pl.program_id(0) * RET_HEADS_PER_STEP
    i = pl.program_id(2)
    c = RET_BLOCK
    dk, dv = RET_QK_DIM, RET_V_DIM
    scale = RET_QK_DIM ** -0.5
    scans = [(d, hh) for d in range(2) for hh in range(RET_HEADS_PER_STEP)]

    @pl.when((pl.program_id(1) == 0) & (i == 0))
    def _():
        row = lax.broadcasted_iota(jnp.int32, (c, c), 0).astype(F32)
        col = lax.broadcasted_iota(jnp.int32, (c, c), 1).astype(F32)
        t_idx = lax.broadcasted_iota(jnp.int32, (c, 1), 0).astype(F32)
        for d, hh in scans:
            lg = lg_ref[d, h0 + hh]
            diff = row - col if d == 0 else col - row
            dec_ref[d, hh] = jnp.where(diff >= 0, jnp.exp(lg * jnp.maximum(diff, 0.0)), 0.0) * scale
            pos = t_idx if d == 0 else (c - 1.0) - t_idx
            qd_ref[d, hh] = jnp.exp(lg * (pos + 1.0))
            kd_ref[d, hh] = jnp.exp(lg * ((c - 1.0) - pos)) * scale

    @pl.when(i == 0)
    def _():
        for d, hh in scans:
            s_ref[d, hh] = s0_ref[0, 0, d, hh] if latent else jnp.zeros((dk, dv), F32)

    nt = (((1,), (1,)), ((), ()))
    tn = (((0,), (0,)), ((), ()))
    chunk_refs = ((qf_ref, kf_ref, vf_ref, of_ref), (qb_ref, kb_ref, vb_ref, ob_ref))
    for d, hh in scans:
        q_ref, k_ref, v_ref, o_ref = chunk_refs[d]
        q = q_ref[0, :, hh * dk:(hh + 1) * dk]
        k = k_ref[0, :, hh * dk:(hh + 1) * dk]
        v = v_ref[0, :, hh * dv:(hh + 1) * dv]
        a = lax.dot_general(q, k, nt, preferred_element_type=F32)
        inner_w = (a * dec_ref[d, hh]).astype(BF16)
        q_dec = (q.astype(F32) * qd_ref[d, hh]).astype(BF16)
        k_dec = (k.astype(F32) * kd_ref[d, hh]).astype(BF16)
        s_old = s_ref[d, hh]
        o = (jnp.dot(inner_w, v, preferred_element_type=F32)
             + jnp.dot(q_dec, s_old.astype(BF16), preferred_element_type=F32))
        s_new = (s_old * jnp.exp(lg_ref[d, h0 + hh] * c)
                 + lax.dot_general(k_dec, v, tn, preferred_element_type=F32))
        s_ref[d, hh] = s_new
        o_ref[0, :, hh * dv:(hh + 1) * dv] = o.astype(o_ref.dtype)
        if not latent:
            @pl.when(i == n_blocks - 1)
            def _(d=d, hh=hh, s_new=s_new):
                sfin_ref[0, 0, d, hh] = s_new


def retention(p, log_gamma, state0, layer_j):
    b, t, _ = p.shape
    latent = state0 is not None
    c = RET_BLOCK
    nb = t // c
    hps = RET_HEADS_PER_STEP
    dk, dv, nh = hps * RET_QK_DIM, hps * RET_V_DIM, RET_HEADS // hps

    def chunk_specs(blk_of):
        return [
            pl.BlockSpec((1, c, dk), lambda h, bi, i, lg: (bi, blk_of(i), h)),
            pl.BlockSpec((1, c, dk), lambda h, bi, i, lg: (bi, blk_of(i), nh + h)),
            pl.BlockSpec((1, c, dv), lambda h, bi, i, lg: (bi, blk_of(i), nh + h)),
        ]

    fwd = lambda i: i
    bwd = lambda i: nb - 1 - i
    in_specs = chunk_specs(fwd) + chunk_specs(bwd)
    args = [p] * 6
    o_shape = jax.ShapeDtypeStruct((b, t, RET_V_WIDTH), BF16)
    out_shape = [o_shape, o_shape]
    out_specs = [pl.BlockSpec((1, c, dv), lambda h, bi, i, lg: (bi, fwd(i), h)),
                 pl.BlockSpec((1, c, dv), lambda h, bi, i, lg: (bi, bwd(i), h))]
    state_dims = (RET_QK_DIM, RET_V_DIM)
    state_block = (1, 1, 2, hps) + state_dims
    if latent:
        in_specs.append(pl.BlockSpec(state_block, lambda h, bi, i, lg: (bi, layer_j, 0, h, 0, 0)))
        args.append(state0)
    else:
        out_shape.append(jax.ShapeDtypeStruct((b, 1, 2, RET_HEADS) + state_dims, F32))
        out_specs.append(pl.BlockSpec(state_block, lambda h, bi, i, lg: (bi, 0, 0, h, 0, 0)))
    return pl.pallas_call(
        functools.partial(_retention_kernel, latent=latent, n_blocks=nb),
        out_shape=tuple(out_shape),
        grid_spec=pltpu.PrefetchScalarGridSpec(
            num_scalar_prefetch=1,
            grid=(nh, b, nb),
            in_specs=in_specs,
            out_specs=tuple(out_specs),
            scratch_shapes=[pltpu.VMEM((2, hps) + state_dims, F32), pltpu.VMEM((2, hps, c, c), F32),
                            pltpu.VMEM((2, hps, c, 1), F32), pltpu.VMEM((2, hps, c, 1), F32)],
        ),
        compiler_params=_params(("arbitrary",) * 3, 40 << 20),
        name="retention",
    )(log_gamma, *args)


def _trunk(x, mods, ctx_k, ctx_v, ctx_state, wts, p_dtype):
    b, t, d = x.shape
    is_ctx = ctx_k is None
    nbm = mods.shape[1]
    rows_per_mod = (b * t) // nbm
    x2 = x.reshape(b * t, d)
    new_k = new_v = new_s = None
    for i in range(DEPTH):
        mod = [mods[i, :, k][:, None, :] for k in range(6)]
        j = i // 2
        if i % 2 == 0:
            p = modproj(x2, mod[0], mod[1], wts["norm_g"][i, 0][None], wts["even_w_in"][j],
                        rows_per_mod, p_dtype, tm=1024, tn=512).reshape(b, t, EVEN_IN)
            qg, kg = wts["even_q_norm"][j][None], wts["even_k_norm"][j][None]
            if is_ctx:
                attn, new_k, new_v = ctx_attention(p, qg, kg)
            else:
                attn = na_attention(p, ctx_k, ctx_v, j, wts["na_rel_bias"][j], qg, kg)
            n2 = GRID_W if not is_ctx else math.isqrt(t)
            fm = fourier_mix(p, t // n2, n2)
            x2 = even_out(fm.reshape(b * t, FNET_WIDTH), attn.reshape(b * t, NA_WIDTH),
                          wts["even_w_out"][j], x2, mod[2], rows_per_mod, tm=512, tn=1024)
        else:
            rope = (t, 2 * RET_QK_WIDTH) if not is_ctx else None
            p = modproj(x2, mod[0], mod[1], wts["norm_g"][i, 0][None], wts["odd_w_in"][j],
                        rows_per_mod, BF16, tm=1024, tn=512, rope=rope)
            lg = jax.nn.log_sigmoid(wts["ret_decay_logit"][j].astype(F32))
            if is_ctx:
                o_f, o_b, new_s = retention(p.reshape(b, t, ODD_IN), lg, None, j)
            else:
                o_f, o_b = retention(p.reshape(b, t, ODD_IN), lg, ctx_state, j)
            x2 = odd_out(o_f.reshape(b * t, RET_V_WIDTH), o_b.reshape(b * t, RET_V_WIDTH), p,
                         wts["odd_w_out"][j], x2, mod[2], rows_per_mod, tm=256)
        x2 = ffn_block(x2, mod[3], mod[4], mod[5], wts["norm_g"][i, 1][None], wts["ffn_w_gate"],
                       wts["ffn_w_up"], wts["ffn_w_down"], i, rows_per_mod, tm=512, tf=512)
    return x2.reshape(b, t, d), new_k, new_v, new_s


def kernel(x_prompt, x_sample, cache_k, cache_v, state_ret, c, c_ctx, ada_w, ada_b, norm_g,
           even_w_in, even_q_norm, even_k_norm, na_rel_bias, even_w_out, odd_w_in, ret_decay_logit,
           odd_w_out, ffn_w_gate, ffn_w_up, ffn_w_down):
    nb_lat = c.shape[0]
    cond = jnp.concatenate(
        [c, c_ctx[None, :], jnp.zeros((MOD_ROWS - nb_lat - 1, D_MODEL), F32)], axis=0)
    mods = adaln_all(cond, ada_w, ada_b).reshape(DEPTH, MOD_ROWS, 6, D_MODEL)
    wts = dict(
        norm_g=norm_g, even_q_norm=even_q_norm, even_k_norm=even_k_norm, na_rel_bias=na_rel_bias,
        ret_decay_logit=ret_decay_logit,
        even_w_in=even_w_in.astype(BF16), even_w_out=even_w_out.astype(BF16),
        odd_w_in=odd_w_in.astype(BF16), odd_w_out=odd_w_out.astype(BF16),
        ffn_w_gate=ffn_w_gate.astype(BF16), ffn_w_up=ffn_w_up.astype(BF16),
        ffn_w_down=ffn_w_down.astype(BF16),
    )
    y_prompt, new_k, new_v, new_s = _trunk(
        x_prompt, mods[:, nb_lat:nb_lat + 1], None, None, None, wts, F32)
    y_sample, _, _, _ = _trunk(x_sample, mods[:, :nb_lat], cache_k, cache_v, state_ret, wts, BF16)
    return (y_prompt, y_sample, new_k, new_v, new_s)
```

```python
import functools
import math

import numpy as np
import jax
import jax.numpy as jnp
from jax import lax
from jax.experimental import pallas as pl
from jax.experimental.pallas import tpu as pltpu

F32 = jnp.float32
BF16 = jnp.bfloat16

D_MODEL = 2048
DEPTH = 2
GRID_W = 64
EPS = 1e-6
NEG_INF = -1e30
FNET_GROUPS = 8
FNET_GROUP_DIM = 128
FNET_WIDTH = 1024
NA_HEADS = 8
NA_HEAD_DIM = 128
NA_WIDTH = 1024
NA_KH = 8
NA_KW = 16
EVEN_IN = FNET_WIDTH + 3 * NA_WIDTH
RET_HEADS = 8
RET_QK_DIM = 256
RET_V_DIM = 512
RET_QK_WIDTH = RET_HEADS * RET_QK_DIM
RET_V_WIDTH = RET_HEADS * RET_V_DIM
ODD_IN = 2 * RET_QK_WIDTH + 3 * RET_V_WIDTH
RET_BLOCK = 256
RET_HEADS_PER_STEP = 2
ROPE_BASE = 10000.0
D_FF = 5632
MOD_ROWS = 8
MOD_CHUNK = 128

V7X_VMEM_BUDGET = 56 * 1024 * 1024
VMEM_SLACK = 8 * 1024 * 1024


def _params(semantics, vmem_bytes):
    return pltpu.CompilerParams(dimension_semantics=semantics,
                                vmem_limit_bytes=min(int(vmem_bytes), V7X_VMEM_BUDGET))


def _silu(x):
    return x * (1.0 / (1.0 + jnp.exp(-x)))


def _adaln_kernel(c_ref, w_ref, b_ref, o_ref):
    s = _silu(c_ref[...]).astype(BF16)
    w = w_ref[0].astype(BF16)
    o_ref[0] = jnp.dot(s, w, preferred_element_type=F32) + b_ref[0]


def adaln_all(cond, ada_w, ada_b):
    n = ada_w.shape[-1]
    tn = 1024
    return pl.pallas_call(
        _adaln_kernel,
        out_shape=jax.ShapeDtypeStruct((DEPTH, MOD_ROWS, n), F32),
        grid=(DEPTH, n // tn),
        in_specs=[
            pl.BlockSpec((MOD_ROWS, D_MODEL), lambda l, j: (0, 0)),
            pl.BlockSpec((1, D_MODEL, tn), lambda l, j: (l, 0, j)),
            pl.BlockSpec((1, 1, tn), lambda l, j: (l, 0, j)),
        ],
        out_specs=pl.BlockSpec((1, MOD_ROWS, tn), lambda l, j: (l, 0, j)),
        compiler_params=_params(("arbitrary", "arbitrary"), 40 << 20),
        name="adaln",
    )(cond, ada_w, ada_b.reshape(DEPTH, 1, n))


def _modulate_rows(h_ref, x_ref, g_ref, sh_ref, sc_ref, row0, n_rows, unrolled=False):
    g = g_ref[...]
    shift = sh_ref[0]
    scale1 = 1.0 + sc_ref[0]
    piece = min(MOD_CHUNK, n_rows)

    def body(c, carry):
        rows = pl.ds(pl.multiple_of(row0 + c * piece, piece), piece)
        x = x_ref[rows, :]
        ms = jnp.mean(x * x, axis=-1, keepdims=True)
        h_ref[rows, :] = (x * lax.rsqrt(ms + EPS) * g * scale1 + shift).astype(h_ref.dtype)
        return carry

    if unrolled:
        for c in range(n_rows // piece):
            body(c, 0)
    else:
        lax.fori_loop(0, n_rows // piece, body, 0)


def _lookahead_start(n_col_steps):
    assert n_col_steps >= 2
    return n_col_steps - (1 << ((n_col_steps - 1).bit_length() - 1))


def _lookahead_tile(i, j, n_row_tiles, n_col_steps):
    nxt = jnp.minimum(i + 1, n_row_tiles - 1)
    return jnp.where((i == 0) & (j < _lookahead_start(n_col_steps)), 0, nxt)


def _modulate_ahead(h_ref, hn_ref, x_ref, g_ref, sh_ref, sc_ref, n_j):
    i, j = pl.program_id(0), pl.program_id(1)
    tm = x_ref.shape[0]

    @pl.when((i == 0) & (j == 0))
    def _():
        _modulate_rows(h_ref, x_ref, g_ref, sh_ref, sc_ref, 0, tm)

    @pl.when((i > 0) & (j == 0))
    def _():
        h_ref[...] = hn_ref[...]

    first = _lookahead_start(n_j)
    n_rows = tm // (n_j - first)
    _modulate_rows(hn_ref, x_ref, g_ref, sh_ref, sc_ref,
                   jnp.maximum(j - first, 0) * n_rows, n_rows, unrolled=True)


def _rope_tables(t_len):
    half = RET_QK_DIM // 2
    nf = half // 2
    inv = ROPE_BASE ** (-jnp.arange(nf, dtype=F32) / nf)
    t = jnp.arange(t_len)
    rows = (t // GRID_W).astype(F32)
    cols = (t % GRID_W).astype(F32)

    def tabs(pos):
        ang = pos[:, None] * inv[None, :]
        c, s = jnp.cos(ang), jnp.sin(ang)
        return jnp.concatenate([c, c], axis=-1), jnp.concatenate([-s, s], axis=-1)

    cr, sr = tabs(rows)
    cc, sc = tabs(cols)
    return jnp.concatenate([cr, cc], axis=-1), jnp.concatenate([sr, sc], axis=-1)


def _rope(x, cos, sin):
    half = RET_QK_DIM // 2
    swapped = [pltpu.roll(x[:, s * half:(s + 1) * half], half // 2, axis=1) for s in range(2)]
    return x * cos + jnp.concatenate(swapped, axis=-1) * sin


def _modproj_kernel(*refs, rope_tiles, n_j):
    if rope_tiles:
        x_ref, sh_ref, sc_ref, g_ref, w_ref, cos_ref, sin_ref, o_ref, h_ref, hn_ref = refs
    else:
        x_ref, sh_ref, sc_ref, g_ref, w_ref, o_ref, h_ref, hn_ref = refs
    j = pl.program_id(1)
    _modulate_ahead(h_ref, hn_ref, x_ref, g_ref, sh_ref, sc_ref, n_j)
    acc = jnp.dot(h_ref[...], w_ref[...], preferred_element_type=F32)
    if not rope_tiles:
        o_ref[...] = acc.astype(o_ref.dtype)
        return

    @pl.when(j < rope_tiles)
    def _():
        cos = cos_ref[...]
        sin = sin_ref[...]
        for s in range(acc.shape[1] // RET_QK_DIM):
            cs = slice(s * RET_QK_DIM, (s + 1) * RET_QK_DIM)
            o_ref[:, cs] = _rope(acc[:, cs], cos, sin).astype(o_ref.dtype)

    @pl.when(j >= rope_tiles)
    def _():
        o_ref[...] = acc.astype(o_ref.dtype)


def modproj(x, shift, scale, g, w, rows_per_mod, out_dtype, tm, tn, rope=None):
    m, d = x.shape
    n = w.shape[1]
    osz = jnp.dtype(out_dtype).itemsize
    n_i, n_j = m // tm, n // tn
    vmem = 2 * tm * d * 4 + 2 * d * tn * 2 + 2 * tm * tn * osz + 2 * tm * d * 2 + 2 * tm * tn * 4 + VMEM_SLACK
    tile = functools.partial(_lookahead_tile, n_row_tiles=n_i, n_col_steps=n_j)
    mod_idx = lambda i, j: ((tile(i, j) * tm) // rows_per_mod, 0, 0)
    in_specs = [
        pl.BlockSpec((tm, d), lambda i, j: (tile(i, j), 0)),
        pl.BlockSpec((1, 1, d), mod_idx),
        pl.BlockSpec((1, 1, d), mod_idx),
        pl.BlockSpec((1, d), lambda i, j: (0, 0)),
        pl.BlockSpec((d, tn), lambda i, j: (0, j)),
    ]
    args = [x, shift, scale, g, w]
    rope_tiles = 0
    if rope is not None:
        seq_len, n_cols = rope
        rope_tiles = n_cols // tn
        cos, sin = _rope_tables(seq_len)
        tab = pl.BlockSpec((tm, RET_QK_DIM), lambda i, j: (i % (seq_len // tm), 0))
        in_specs += [tab, tab]
        args += [cos, sin]
        vmem += 2 * 2 * tm * RET_QK_DIM * 4
    return pl.pallas_call(
        functools.partial(_modproj_kernel, rope_tiles=rope_tiles, n_j=n_j),
        out_shape=jax.ShapeDtypeStruct((m, n), out_dtype),
        grid=(n_i, n_j),
        in_specs=in_specs,
        out_specs=pl.BlockSpec((tm, tn), lambda i, j: (i, j)),
        scratch_shapes=[pltpu.VMEM((tm, d), BF16), pltpu.VMEM((tm, d), BF16)],
        compiler_params=_params(("arbitrary", "arbitrary"), vmem),
        name="modproj",
    )(*args)


def _head_rms(x, g):
    return x * lax.rsqrt(jnp.mean(x * x, axis=-1, keepdims=True) + EPS) * g


def _ctx_attn_kernel(q_ref, k_ref, v_ref, qg_ref, kg_ref, o_ref, nk_ref, nv_ref):
    scale = NA_HEAD_DIM ** -0.5
    for h in range(NA_HEADS):
        cs = slice(h * NA_HEAD_DIM, (h + 1) * NA_HEAD_DIM)
        q = _head_rms(q_ref[0, :, cs].astype(F32), qg_ref[...])
        k = _head_rms(k_ref[0, :, cs].astype(F32), kg_ref[...])
        v = v_ref[0, :, cs].astype(F32)
        nk_ref[0, 0, h] = k
        nv_ref[0, 0, h] = v
        s = lax.dot_general(q.astype(BF16), k.astype(BF16), (((1,), (1,)), ((), ())),
                            preferred_element_type=F32) * scale
        m = jnp.max(s, axis=-1, keepdims=True)
        e = jnp.exp(s - m)
        l = jnp.sum(e, axis=-1, keepdims=True)
        o = jnp.dot(e.astype(BF16), v.astype(BF16), preferred_element_type=F32) / l
        o_ref[0, :, cs] = o.astype(o_ref.dtype)


def ctx_attention(p, q_gain, k_gain):
    b, t, _ = p.shape
    cache_shape = (b, 1, NA_HEADS, t, NA_HEAD_DIM)
    blk = lambda c: pl.BlockSpec((1, t, NA_WIDTH), lambda i, c=c: (i, 0, c))
    gain = pl.BlockSpec((1, NA_HEAD_DIM), lambda i: (0, 0))
    cache_spec = pl.BlockSpec((1, 1, NA_HEADS, t, NA_HEAD_DIM), lambda i: (i, 0, 0, 0, 0))
    return pl.pallas_call(
        _ctx_attn_kernel,
        out_shape=(jax.ShapeDtypeStruct((b, t, NA_WIDTH), BF16),
                   jax.ShapeDtypeStruct(cache_shape, F32),
                   jax.ShapeDtypeStruct(cache_shape, F32)),
        grid=(b,),
        in_specs=[blk(1), blk(2), blk(3), gain, gain],
        out_specs=(pl.BlockSpec((1, t, NA_WIDTH), lambda i: (i, 0, 0)), cache_spec, cache_spec),
        compiler_params=_params(("arbitrary",), 32 << 20),
        name="ctx_attention",
    )(p, p, p, q_gain, k_gain)


NA_BIAS_ROWS = 2 * NA_KH - 1
NA_BIAS_COLS = 2 * NA_KW - 1
NA_MASK_TILE = NA_BIAS_ROWS
NA_QROWS = 4
NA_KROWS = NA_KH + NA_QROWS


def _na_build_bias(rb_ref, bias_ref, head):
    shape = (GRID_W, 2 * GRID_W)
    lane = lax.broadcasted_iota(jnp.int32, shape, 1)
    qc = lax.broadcasted_iota(jnp.int32, shape, 0)
    kc = lane & (GRID_W - 1)
    start = jnp.clip(qc - NA_KW // 2, 0, GRID_W - NA_KW)
    ok = (kc >= start) & (kc < start + NA_KW)
    delta = kc - qc + (NA_KW - 1)
    left = lane < GRID_W
    base = head * (NA_BIAS_ROWS * NA_BIAS_COLS)

    def build(dr, carry):
        acc = jnp.full(shape, NEG_INF, F32)
        for dc in range(NA_BIAS_COLS):
            acc = jnp.where(delta == dc, rb_ref[base + dr * NA_BIAS_COLS + dc], acc)
        tile = jnp.where(ok, acc, NEG_INF)
        bias_ref[0, dr] = jnp.where(left, tile, 0.0)
        bias_ref[1, dr] = jnp.where(left, 0.0, tile)
        return carry

    lax.fori_loop(0, NA_BIAS_ROWS, build, 0)
    bias_ref[0, NA_MASK_TILE] = jnp.where(left, NEG_INF, 0.0)
    bias_ref[1, NA_MASK_TILE] = jnp.where(left, 0.0, NEG_INF)


def _na_kernel(rb_ref, q_ref, k_ref, v_ref, kc_ref, vc_ref, qg_ref, kg_ref, o_ref,
               qs_ref, ks_ref, vs_ref, bias_ref, *, rows):
    @pl.when(pl.program_id(1) == 0)
    def _():
        _na_build_bias(rb_ref, bias_ref, pl.program_id(0))

    scale = NA_HEAD_DIM ** -0.5
    qs_ref[...] = (_head_rms(q_ref[0].astype(F32), qg_ref[...]) * scale).astype(BF16)
    ks_ref[...] = _head_rms(k_ref[0].astype(F32), kg_ref[...]).astype(BF16)
    vs_ref[...] = v_ref[0].astype(BF16)
    kc = kc_ref[0, 0, 0].astype(BF16)
    vc = vc_ref[0, 0, 0].astype(BF16)
    nt = (((1,), (1,)), ((), ()))
    nq = NA_QROWS * GRID_W
    nk = NA_KROWS * GRID_W

    def tile_index(r, rs, kr):
        inside = (kr >= rs) & (kr < rs + NA_KH)
        return jnp.where(inside, kr - r + (NA_KH - 1), NA_MASK_TILE)

    def body(blk, carry):
        r0 = blk * NA_QROWS
        k0 = jnp.clip(r0 - NA_KH // 2, 0, rows - NA_KROWS)
        q = qs_ref[pl.ds(pl.multiple_of(r0 * GRID_W, nq), nq), :]
        kw = ks_ref[pl.ds(pl.multiple_of(k0 * GRID_W, GRID_W), nk), :]
        vw = vs_ref[pl.ds(pl.multiple_of(k0 * GRID_W, GRID_W), nk), :]
        bias_rows = []
        for i in range(NA_QROWS):
            r = r0 + i
            rs = jnp.clip(r - NA_KH // 2, 0, rows - NA_KH)
            pairs = [bias_ref[0, tile_index(r, rs, k0 + 2 * jp)]
                     + bias_ref[1, tile_index(r, rs, k0 + 2 * jp + 1)]
                     for jp in range(NA_KROWS // 2)]
            bias_rows.append(jnp.concatenate(pairs, axis=-1))
        bias = jnp.concatenate(bias_rows, axis=0)
        s_w = lax.dot_general(q, kw, nt, preferred_element_type=F32) + bias
        s_c = lax.dot_general(q, kc, nt, preferred_element_type=F32)
        m = jnp.maximum(jnp.max(s_w, axis=-1, keepdims=True), jnp.max(s_c, axis=-1, keepdims=True))
        e_w = jnp.exp(s_w - m)
        e_c = jnp.exp(s_c - m)
        l = jnp.sum(e_w, axis=-1, keepdims=True) + jnp.sum(e_c, axis=-1, keepdims=True)
        o = (jnp.dot(e_w.astype(BF16), vw, preferred_element_type=F32)
             + jnp.dot(e_c.astype(BF16), vc, preferred_element_type=F32)) / l
        o_ref[0, pl.ds(pl.multiple_of(r0 * GRID_W, nq), nq), :] = o.astype(o_ref.dtype)
        return carry

    lax.fori_loop(0, rows // NA_QROWS, body, 0, unroll=2)


def na_attention(p, cache_k, cache_v, layer_j, rel_bias, q_gain, k_gain):
    b, t, _ = p.shape
    past = cache_k.shape[3]
    rows = t // GRID_W
    hd = NA_HEAD_DIM
    col0 = FNET_WIDTH // hd
    blk = lambda c: pl.BlockSpec((1, t, hd), lambda h, i, rb, c=c: (i, 0, col0 + c * NA_HEADS + h))
    cache_spec = pl.BlockSpec((1, 1, 1, past, hd), lambda h, i, rb: (i, layer_j, h, 0, 0))
    gain = pl.BlockSpec((1, hd), lambda h, i, rb: (0, 0))
    return pl.pallas_call(
        functools.partial(_na_kernel, rows=rows),
        out_shape=jax.ShapeDtypeStruct((b, t, NA_WIDTH), BF16),
        grid_spec=pltpu.PrefetchScalarGridSpec(
            num_scalar_prefetch=1,
            grid=(NA_HEADS, b),
            in_specs=[blk(0), blk(1), blk(2), cache_spec, cache_spec, gain, gain],
            out_specs=pl.BlockSpec((1, t, hd), lambda h, i, rb: (i, 0, h)),
            scratch_shapes=[pltpu.VMEM((t, hd), BF16)] * 3
            + [pltpu.VMEM((2, NA_BIAS_ROWS + 1, GRID_W, 2 * GRID_W), F32)],
        ),
        compiler_params=_params(("arbitrary", "arbitrary"), 40 << 20),
        name="na_attention",
    )(rel_bias.astype(F32).reshape(-1), p, p, p, cache_k, cache_v, q_gain, k_gain)


FNET_BLOCK = 16


def _fnet_tables(n1, n2):
    t_len = n1 * n2
    tb = FNET_BLOCK
    eye = np.eye(tb)
    k1 = np.arange(n1)
    ang1 = (2.0 * np.pi / n1) * ((k1[:, None] * k1[None, :]) % n1)
    f1 = np.concatenate([np.cos(ang1), -np.sin(ang1)], axis=0) / math.sqrt(t_len)
    rows_mat = np.kron(f1, eye)
    t2 = np.arange(n2)
    tw = (2.0 * np.pi / t_len) * (k1[:, None] * t2[None, :])
    tw = tw.reshape(n1, n2 // tb, tb).transpose(1, 0, 2).reshape(n2 // tb, n1 * tb, 1)
    c = np.arange(FNET_GROUP_DIM)
    ang3 = (2.0 * np.pi / FNET_GROUP_DIM) * ((c[:, None] * c[None, :]) % FNET_GROUP_DIM)
    c3, s3 = np.cos(ang3), np.sin(ang3)
    chan = np.block([[c3, -s3], [s3, c3]]) / math.sqrt(FNET_GROUP_DIM)
    ang2 = (2.0 * np.pi / n2) * ((t2[:, None] * t2[None, :]) % n2)
    cols_re = np.einsum("pq,kt->kpqt", eye, np.cos(ang2)).reshape(n2 * tb, tb * n2)
    cols_im = np.einsum("pq,kt->kpqt", eye, np.sin(ang2)).reshape(n2 * tb, tb * n2)
    bf = lambda x: jnp.asarray(x, dtype=BF16)
    f32 = lambda x: jnp.asarray(x, dtype=F32)
    return bf(rows_mat), f32(np.cos(tw)), f32(np.sin(tw)), bf(chan), bf(cols_re), bf(cols_im)


def _fnet_rows_kernel(x_ref, mat_ref, twc_ref, tws_ref, chan_ref, vr_ref, vi_ref):
    _, n1, tb, width = x_ref.shape
    rows = n1 * tb
    gd = FNET_GROUP_DIM
    x = x_ref[0].reshape(rows, width).astype(BF16)
    h = jnp.dot(mat_ref[...], x, preferred_element_type=F32)
    hr, hi = h[:rows], h[rows:]
    c, s = twc_ref[0], tws_ref[0]
    gr = (hr * c + hi * s).astype(BF16)
    gi = (hi * c - hr * s).astype(BF16)
    stacked = jnp.concatenate(
        [jnp.concatenate([gr[:, g * gd:(g + 1) * gd], gi[:, g * gd:(g + 1) * gd]], axis=1)
         for g in range(FNET_GROUPS)], axis=0)
    v = jnp.dot(stacked, chan_ref[...], preferred_element_type=F32)
    for g in range(FNET_GROUPS):
        part = v[g * rows:(g + 1) * rows]
        vr_ref[0, :, :, g * gd:(g + 1) * gd] = part[:, :gd].reshape(n1, tb, gd).astype(vr_ref.dtype)
        vi_ref[0, :, :, g * gd:(g + 1) * gd] = part[:, gd:].reshape(n1, tb, gd).astype(vi_ref.dtype)


def _fnet_cols_kernel(vr_ref, vi_ref, re_ref, im_ref, o_ref):
    _, kb, n2, width = vr_ref.shape
    vr = vr_ref[0].reshape(kb * n2, width)
    vi = vi_ref[0].reshape(kb * n2, width)
    y = (jnp.dot(re_ref[...], vr, preferred_element_type=F32)
         + jnp.dot(im_ref[...], vi, preferred_element_type=F32))
    o_ref[0] = y.reshape(n2, kb, width).astype(o_ref.dtype)


def fourier_mix(p, n1, n2):
    b, t, width = p.shape
    tb = FNET_BLOCK
    w = FNET_WIDTH
    rows_mat, twc, tws, chan, cols_re, cols_im = _fnet_tables(n1, n2)
    const = lambda shape: pl.BlockSpec(shape, lambda i, s: (0,) * len(shape))
    tw_spec = pl.BlockSpec((1, n1 * tb, 1), lambda i, s: (s, 0, 0))
    v_shape = jax.ShapeDtypeStruct((b, n1, n2, w), BF16)
    v_spec = pl.BlockSpec((1, n1, tb, w), lambda i, s: (i, 0, s, 0))
    isz = p.dtype.itemsize
    vmem = (2 * n1 * tb * w * isz + 2 * rows_mat.size * 2 + 4 * n1 * tb * w * 2
            + 6 * n1 * tb * w * 4 + VMEM_SLACK)
    vr, vi = pl.pallas_call(
        _fnet_rows_kernel,
        out_shape=(v_shape, v_shape),
        grid=(b, n2 // tb),
        in_specs=[pl.BlockSpec((1, n1, tb, w), lambda i, s: (i, 0, s, 0)),
                  const(rows_mat.shape), tw_spec, tw_spec, const(chan.shape)],
        out_specs=(v_spec, v_spec),
        compiler_params=_params(("arbitrary", "arbitrary"), vmem),
        name="fnet_rows",
    )(p.reshape(b, n1, n2, width), rows_mat, twc, tws, chan)
    v_in = pl.BlockSpec((1, tb, n2, w), lambda i, s: (i, s, 0, 0))
    vmem = 2 * 2 * tb * n2 * w * 2 + 2 * 2 * cols_re.size * 2 + 2 * n2 * tb * w * 2 + 3 * n2 * tb * w * 4 + VMEM_SLACK
    out = pl.pallas_call(
        _fnet_cols_kernel,
        out_shape=jax.ShapeDtypeStruct((b, n2, n1, w), BF16),
        grid=(b, n1 // tb),
        in_specs=[v_in, v_in, const(cols_re.shape), const(cols_im.shape)],
        out_specs=pl.BlockSpec((1, n2, tb, w), lambda i, s: (i, 0, s, 0)),
        compiler_params=_params(("arbitrary", "arbitrary"), vmem),
        name="fnet_cols",
    )(vr, vi, cols_re, cols_im)
    return out.reshape(b, t, w)


def _even_out_kernel(f_ref, a_ref, wf_ref, wa_ref, x_ref, gate_ref, o_ref):
    y = (jnp.dot(f_ref[...], wf_ref[...], preferred_element_type=F32)
         + jnp.dot(a_ref[...], wa_ref[...], preferred_element_type=F32))
    o_ref[...] = x_ref[...] + gate_ref[0] * y


def even_out(f, a, w, x, gate, rows_per_mod, tm, tn):
    m, d = x.shape
    kf = f.shape[1]
    vmem = 2 * 2 * tm * kf * 2 + 2 * 2 * kf * tn * 2 + 4 * tm * tn * 4 + tm * tn * 4 + VMEM_SLACK
    return pl.pallas_call(
        _even_out_kernel,
        out_shape=jax.ShapeDtypeStruct((m, d), F32),
        grid=(d // tn, m // tm),
        in_specs=[
            pl.BlockSpec((tm, kf), lambda j, i: (i, 0)),
            pl.BlockSpec((tm, kf), lambda j, i: (i, 0)),
            pl.BlockSpec((kf, tn), lambda j, i: (0, j)),
            pl.BlockSpec((kf, tn), lambda j, i: (1, j)),
            pl.BlockSpec((tm, tn), lambda j, i: (i, j)),
            pl.BlockSpec((1, 1, tn), lambda j, i: ((i * tm) // rows_per_mod, 0, j)),
        ],
        out_specs=pl.BlockSpec((tm, tn), lambda j, i: (i, j)),
        compiler_params=_params(("arbitrary", "arbitrary"), vmem),
        name="even_out",
    )(f, a, w, w, x, gate)


def _gated_group_norm(o, g):
    o = o.astype(F32)
    oc = o - jnp.mean(o, axis=-1, keepdims=True)
    gn = oc * lax.rsqrt(jnp.mean(oc * oc, axis=-1, keepdims=True) + EPS)
    hg = 0.5 * g.astype(F32)
    return (hg + hg * jnp.tanh(hg)) * gn


def _odd_out_kernel(of_ref, ob_ref, gf_ref, gb_ref, w_ref, x_ref, gate_ref, o_ref):
    acc = None
    for h in range(RET_HEADS):
        cs = slice(h * RET_V_DIM, (h + 1) * RET_V_DIM)
        y = (_gated_group_norm(of_ref[:, cs], gf_ref[:, cs])
             + _gated_group_norm(ob_ref[:, cs], gb_ref[:, cs])).astype(BF16)
        part = jnp.dot(y, w_ref[cs, :], preferred_element_type=F32)
        acc = part if acc is None else acc + part
    o_ref[...] = x_ref[...] + gate_ref[0] * acc


def odd_out(o_f, o_b, p, w, x, gate, rows_per_mod, tm):
    m, d = x.shape
    k = w.shape[0]
    gcol = (2 * RET_QK_WIDTH + RET_V_WIDTH) // RET_V_WIDTH
    vmem = (2 * 4 * tm * k * 2 + k * d * 2 + 4 * tm * d * 4 + 2 * tm * d * 4
            + 6 * tm * RET_V_DIM * 4 + VMEM_SLACK)
    act = lambda c: pl.BlockSpec((tm, k), lambda i, c=c: (i, c))
    return pl.pallas_call(
        _odd_out_kernel,
        out_shape=jax.ShapeDtypeStruct((m, d), F32),
        grid=(m // tm,),
        in_specs=[
            act(0), act(0), act(gcol), act(gcol + 1),
            pl.BlockSpec((k, d), lambda i: (0, 0), pipeline_mode=pl.Buffered(1)),
            pl.BlockSpec((tm, d), lambda i: (i, 0)),
            pl.BlockSpec((1, 1, d), lambda i: ((i * tm) // rows_per_mod, 0, 0)),
        ],
        out_specs=pl.BlockSpec((tm, d), lambda i: (i, 0)),
        compiler_params=_params(("arbitrary",), vmem),
        name="odd_out",
    )(o_f, o_b, p, p, w, x, gate)


def _ffn_kernel(xa_ref, sh_ref, sc_ref, g_ref, x_ref, gate_ref, wg_ref, wu_ref, wd_ref, o_ref,
                h_ref, hn_ref, *, n_j):
    j = pl.program_id(1)
    _modulate_ahead(h_ref, hn_ref, xa_ref, g_ref, sh_ref, sc_ref, n_j)

    @pl.when(j == 0)
    def _():
        o_ref[...] = jnp.zeros_like(o_ref)

    h = h_ref[...]
    a = jnp.dot(h, wg_ref[0], preferred_element_type=F32)
    u = jnp.dot(h, wu_ref[0], preferred_element_type=F32)
    act = (_silu(a) * u).astype(BF16)
    o_ref[...] += jnp.dot(act, wd_ref[0], preferred_element_type=F32)

    @pl.when(j == n_j - 1)
    def _():
        o_ref[...] = x_ref[...] + gate_ref[0] * o_ref[...]


def ffn_block(x, shift, scale, gate, g, w_gate, w_up, w_down, layer, rows_per_mod, tm, tf):
    m, d = x.shape
    f = w_gate.shape[2]
    n_i, n_j = m // tm, f // tf
    vmem = 6 * tm * d * 4 + 2 * tm * d * 2 + 2 * 3 * d * tf * 2 + 4 * tm * tf * 4 + tm * d * 4 + VMEM_SLACK
    tile = functools.partial(_lookahead_tile, n_row_tiles=n_i, n_col_steps=n_j)
    ahead = pl.BlockSpec((1, 1, d), lambda i, j: ((tile(i, j) * tm) // rows_per_mod, 0, 0))
    return pl.pallas_call(
        functools.partial(_ffn_kernel, n_j=n_j),
        out_shape=jax.ShapeDtypeStruct((m, d), F32),
        grid=(n_i, n_j),
        in_specs=[
            pl.BlockSpec((tm, d), lambda i, j: (tile(i, j), 0)),
            ahead, ahead,
            pl.BlockSpec((1, d), lambda i, j: (0, 0)),
            pl.BlockSpec((tm, d), lambda i, j: (i, 0)),
            pl.BlockSpec((1, 1, d), lambda i, j: ((i * tm) // rows_per_mod, 0, 0)),
            pl.BlockSpec((1, d, tf), lambda i, j: (layer, 0, j)),
            pl.BlockSpec((1, d, tf), lambda i, j: (layer, 0, j)),
            pl.BlockSpec((1, tf, d), lambda i, j: (layer, j, 0)),
        ],
        out_specs=pl.BlockSpec((tm, d), lambda i, j: (i, 0)),
        scratch_shapes=[pltpu.VMEM((tm, d), BF16), pltpu.VMEM((tm, d), BF16)],
        compiler_params=_params(("arbitrary", "arbitrary"), vmem),
        name="ffn_block",
    )(x, shift, scale, g, x, gate, w_gate, w_up, w_down)


def _retention_kernel(lg_ref, *refs, latent, n_blocks):
    if latent:
        (qf_ref, kf_ref, vf_ref, qb_ref, kb_ref, vb_ref, s0_ref,
         of_ref, ob_ref, s_ref, dec_ref, qd_ref, kd_ref) = refs
    else:
        (qf_ref, kf_ref, vf_ref, qb_ref, kb_ref, vb_ref,
         of_ref, ob_ref, sfin_ref, s_ref, dec_ref, qd_ref, kd_ref) = refs
    h0 = pl.program_id(0) * RET_HEADS_PER_STEP
    i = pl.program_id(2)
    c = RET_BLOCK
    dk, dv = RET_QK_DIM, RET_V_DIM
    scale = RET_QK_DIM ** -0.5
    scans = [(d, hh) for d in range(2) for hh in range(RET_HEADS_PER_STEP)]

    @pl.when((pl.program_id(1) == 0) & (i == 0))
    def _():
        row = lax.broadcasted_iota(jnp.int32, (c, c), 0).astype(F32)
        col = lax.broadcasted_iota(jnp.int32, (c, c), 1).astype(F32)
        t_idx = lax.broadcasted_iota(jnp.int32, (c, 1), 0).astype(F32)
        for d, hh in scans:
            lg = lg_ref[d, h0 + hh]
            diff = row - col if d == 0 else col - row
            dec_ref[d, hh] = jnp.where(diff >= 0, jnp.exp(lg * jnp.maximum(diff, 0.0)), 0.0) * scale
            pos = t_idx if d == 0 else (c - 1.0) - t_idx
            qd_ref[d, hh] = jnp.exp(lg * (pos + 1.0))
            kd_ref[d, hh] = jnp.exp(lg * ((c - 1.0) - pos)) * scale

    @pl.when(i == 0)
    def _():
        for d, hh in scans:
            s_ref[d, hh] = s0_ref[0, 0, d, hh] if latent else jnp.zeros((dk, dv), F32)

    nt = (((1,), (1,)), ((), ()))
    tn = (((0,), (0,)), ((), ()))
    chunk_refs = ((qf_ref, kf_ref, vf_ref, of_ref), (qb_ref, kb_ref, vb_ref, ob_ref))
    for d, hh in scans:
        q_ref, k_ref, v_ref, o_ref = chunk_refs[d]
        q = q_ref[0, :, hh * dk:(hh + 1) * dk]
        k = k_ref[0, :, hh * dk:(hh + 1) * dk]
        v = v_ref[0, :, hh * dv:(hh + 1) * dv]
        a = lax.dot_general(q, k, nt, preferred_element_type=F32)
        inner_w = (a * dec_ref[d, hh]).astype(BF16)
        q_dec = (q.astype(F32) * qd_ref[d, hh]).astype(BF16)
        k_dec = (k.astype(F32) * kd_ref[d, hh]).astype(BF16)
        s_old = s_ref[d, hh]
        o = (jnp.dot(inner_w, v, preferred_element_type=F32)
             + jnp.dot(q_dec, s_old.astype(BF16), preferred_element_type=F32))
        s_new = (s_old * jnp.exp(lg_ref[d, h0 + hh] * c)
                 + lax.dot_general(k_dec, v, tn, preferred_element_type=F32))
        s_ref[d, hh] = s_new
        o_ref[0, :, hh * dv:(hh + 1) * dv] = o.astype(o_ref.dtype)
        if not latent:
            @pl.when(i == n_blocks - 1)
            def _(d=d, hh=hh, s_new=s_new):
                sfin_ref[0, 0, d, hh] = s_new


def retention(p, log_gamma, state0, layer_j):
    b, t, _ = p.shape
    latent = state0 is not None
    c = RET_BLOCK
    nb = t // c
    hps = RET_HEADS_PER_STEP
    dk, dv, nh = hps * RET_QK_DIM, hps * RET_V_DIM, RET_HEADS // hps

    def chunk_specs(blk_of):
        return [
            pl.BlockSpec((1, c, dk), lambda h, bi, i, lg: (bi, blk_of(i), h)),
            pl.BlockSpec((1, c, dk), lambda h, bi, i, lg: (bi, blk_of(i), nh + h)),
            pl.BlockSpec((1, c, dv), lambda h, bi, i, lg: (bi, blk_of(i), nh + h)),
        ]

    fwd = lambda i: i
    bwd = lambda i: nb - 1 - i
    in_specs = chunk_specs(fwd) + chunk_specs(bwd)
    args = [p] * 6
    o_shape = jax.ShapeDtypeStruct((b, t, RET_V_WIDTH), BF16)
    out_shape = [o_shape, o_shape]
    out_specs = [pl.BlockSpec((1, c, dv), lambda h, bi, i, lg: (bi, fwd(i), h)),
                 pl.BlockSpec((1, c, dv), lambda h, bi, i, lg: (bi, bwd(i), h))]
    state_dims = (RET_QK_DIM, RET_V_DIM)
    state_block = (1, 1, 2, hps) + state_dims
    if latent:
        in_specs.append(pl.BlockSpec(state_block, lambda h, bi, i, lg: (bi, layer_j, 0, h, 0, 0)))
        args.append(state0)
    else:
        out_shape.append(jax.ShapeDtypeStruct((b, 1, 2, RET_HEADS) + state_dims, F32))
        out_specs.append(pl.BlockSpec(state_block, lambda h, bi, i, lg: (bi, 0, 0, h, 0, 0)))
    return pl.pallas_call(
        functools.partial(_retention_kernel, latent=latent, n_blocks=nb),
        out_shape=tuple(out_shape),
        grid_spec=pltpu.PrefetchScalarGridSpec(
            num_scalar_prefetch=1,
            grid=(nh, b, nb),
            in_specs=in_specs,
            out_specs=tuple(out_specs),
            scratch_shapes=[pltpu.VMEM((2, hps) + state_dims, F32), pltpu.VMEM((2, hps, c, c), F32),
                            pltpu.VMEM((2, hps, c, 1), F32), pltpu.VMEM((2, hps, c, 1), F32)],
        ),
        compiler_params=_params(("arbitrary",) * 3, 40 << 20),
        name="retention",
    )(log_gamma, *args)


def _trunk(x, mods, ctx_k, ctx_v, ctx_state, wts, p_dtype):
    b, t, d = x.shape
    is_ctx = ctx_k is None
    nbm = mods.shape[1]
    rows_per_mod = (b * t) // nbm
    x2 = x.reshape(b * t, d)
    new_k = new_v = new_s = None
    for i in range(DEPTH):
        mod = [mods[i, :, k][:, None, :] for k in range(6)]
        j = i // 2
        if i % 2 == 0:
            p = modproj(x2, mod[0], mod[1], wts["norm_g"][i, 0][None], wts["even_w_in"][j],
                        rows_per_mod, p_dtype, tm=1024, tn=512).reshape(b, t, EVEN_IN)
            qg, kg = wts["even_q_norm"][j][None], wts["even_k_norm"][j][None]
            if is_ctx:
                attn, new_k, new_v = ctx_attention(p, qg, kg)
            else:
                attn = na_attention(p, ctx_k, ctx_v, j, wts["na_rel_bias"][j], qg, kg)
            n2 = GRID_W if not is_ctx else math.isqrt(t)
            fm = fourier_mix(p, t // n2, n2)
            x2 = even_out(fm.reshape(b * t, FNET_WIDTH), attn.reshape(b * t, NA_WIDTH),
                          wts["even_w_out"][j], x2, mod[2], rows_per_mod, tm=512, tn=1024)
        else:
            rope = (t, 2 * RET_QK_WIDTH) if not is_ctx else None
            p = modproj(x2, mod[0], mod[1], wts["norm_g"][i, 0][None], wts["odd_w_in"][j],
                        rows_per_mod, BF16, tm=1024, tn=512, rope=rope)
            lg = jax.nn.log_sigmoid(wts["ret_decay_logit"][j].astype(F32))
            if is_ctx:
                o_f, o_b, new_s = retention(p.reshape(b, t, ODD_IN), lg, None, j)
            else:
                o_f, o_b = retention(p.reshape(b, t, ODD_IN), lg, ctx_state, j)
            x2 = odd_out(o_f.reshape(b * t, RET_V_WIDTH), o_b.reshape(b * t, RET_V_WIDTH), p,
                         wts["odd_w_out"][j], x2, mod[2], rows_per_mod, tm=256)
        x2 = ffn_block(x2, mod[3], mod[4], mod[5], wts["norm_g"][i, 1][None], wts["ffn_w_gate"],
                       wts["ffn_w_up"], wts["ffn_w_down"], i, rows_per_mod, tm=512, tf=512)
    return x2.reshape(b, t, d), new_k, new_v, new_s


def kernel(x_prompt, x_sample, cache_k, cache_v, state_ret, c, c_ctx, ada_w, ada_b, norm_g,
           even_w_in, even_q_norm, even_k_norm, na_rel_bias, even_w_out, odd_w_in, ret_decay_logit,
           odd_w_out, ffn_w_gate, ffn_w_up, ffn_w_down):
    nb_lat = c.shape[0]
    cond = jnp.concatenate(
        [c, c_ctx[None, :], jnp.zeros((MOD_ROWS - nb_lat - 1, D_MODEL), F32)], axis=0)
    mods = adaln_all(cond, ada_w, ada_b).reshape(DEPTH, MOD_ROWS, 6, D_MODEL)
    wts = dict(
        norm_g=norm_g, even_q_norm=even_q_norm, even_k_norm=even_k_norm, na_rel_bias=na_rel_bias,
        ret_decay_logit=ret_decay_logit,
        even_w_in=even_w_in.astype(BF16), even_w_out=even_w_out.astype(BF16),
        odd_w_in=odd_w_in.astype(BF16), odd_w_out=odd_w_out.astype(BF16),
        ffn_w_gate=ffn_w_gate.astype(BF16), ffn_w_up=ffn_w_up.astype(BF16),
        ffn_w_down=ffn_w_down.astype(BF16),
    )
    y_prompt, new_k, new_v, new_s = _trunk(
        x_prompt, mods[:, nb_lat:nb_lat + 1], None, None, None, wts, F32)
    y_sample, _, _, _ = _trunk(x_sample, mods[:, :nb_lat], cache_k, cache_v, state_ret, wts, BF16)
    return (y_prompt, y_sample, new_k, new_v, new_s)
```

```python
import functools
import math

import numpy as np
import jax
import jax.numpy as jnp
from jax import lax
from jax.experimental import pallas as pl
from jax.experimental.pallas import tpu as pltpu

F32 = jnp.float32
BF16 = jnp.bfloat16

D_MODEL = 2048
DEPTH = 2
GRID_W = 64
EPS = 1e-6
NEG_INF = -1e30
FNET_GROUPS = 8
FNET_GROUP_DIM = 128
FNET_WIDTH = 1024
NA_HEADS = 8
NA_HEAD_DIM = 128
NA_WIDTH = 1024
NA_KH = 8
NA_KW = 16
EVEN_IN = FNET_WIDTH + 3 * NA_WIDTH
RET_HEADS = 8
RET_QK_DIM = 256
RET_V_DIM = 512
RET_QK_WIDTH = RET_HEADS * RET_QK_DIM
RET_V_WIDTH = RET_HEADS * RET_V_DIM
ODD_IN = 2 * RET_QK_WIDTH + 3 * RET_V_WIDTH
RET_BLOCK = 256
RET_HEADS_PER_STEP = 4
ROPE_BASE = 10000.0
D_FF = 5632
MOD_ROWS = 8
MOD_CHUNK = 128

V7X_VMEM_BUDGET = 56 * 1024 * 1024
VMEM_SLACK = 8 * 1024 * 1024


def _params(semantics, vmem_bytes):
    return pltpu.CompilerParams(dimension_semantics=semantics,
                                vmem_limit_bytes=min(int(vmem_bytes), V7X_VMEM_BUDGET))


def _silu(x):
    return x * (1.0 / (1.0 + jnp.exp(-x)))


def _adaln_kernel(c_ref, w_ref, b_ref, o_ref):
    s = _silu(c_ref[...]).astype(BF16)
    w = w_ref[0].astype(BF16)
    o_ref[0] = jnp.dot(s, w, preferred_element_type=F32) + b_ref[0]


def adaln_all(cond, ada_w, ada_b):
    n = ada_w.shape[-1]
    tn = 1024
    return pl.pallas_call(
        _adaln_kernel,
        out_shape=jax.ShapeDtypeStruct((DEPTH, MOD_ROWS, n), F32),
        grid=(DEPTH, n // tn),
        in_specs=[
            pl.BlockSpec((MOD_ROWS, D_MODEL), lambda l, j: (0, 0)),
            pl.BlockSpec((1, D_MODEL, tn), lambda l, j: (l, 0, j)),
            pl.BlockSpec((1, 1, tn), lambda l, j: (l, 0, j)),
        ],
        out_specs=pl.BlockSpec((1, MOD_ROWS, tn), lambda l, j: (l, 0, j)),
        compiler_params=_params(("arbitrary", "arbitrary"), 40 << 20),
        name="adaln",
    )(cond, ada_w, ada_b.reshape(DEPTH, 1, n))


def _modulate_into(h_ref, x_ref, g_ref, sh_ref, sc_ref):
    g = g_ref[...]
    shift = sh_ref[0]
    scale1 = 1.0 + sc_ref[0]

    def body(c, carry):
        rows = pl.ds(pl.multiple_of(c * MOD_CHUNK, MOD_CHUNK), MOD_CHUNK)
        x = x_ref[rows, :]
        ms = jnp.mean(x * x, axis=-1, keepdims=True)
        h_ref[rows, :] = (x * lax.rsqrt(ms + EPS) * g * scale1 + shift).astype(h_ref.dtype)
        return carry

    lax.fori_loop(0, x_ref.shape[0] // MOD_CHUNK, body, 0)


def _rope_tables(t_len):
    half = RET_QK_DIM // 2
    nf = half // 2
    inv = ROPE_BASE ** (-jnp.arange(nf, dtype=F32) / nf)
    t = jnp.arange(t_len)
    rows = (t // GRID_W).astype(F32)
    cols = (t % GRID_W).astype(F32)

    def tabs(pos):
        ang = pos[:, None] * inv[None, :]
        c, s = jnp.cos(ang), jnp.sin(ang)
        return jnp.concatenate([c, c], axis=-1), jnp.concatenate([-s, s], axis=-1)

    cr, sr = tabs(rows)
    cc, sc = tabs(cols)
    return jnp.concatenate([cr, cc], axis=-1), jnp.concatenate([sr, sc], axis=-1)


def _rope(x, cos, sin):
    half = RET_QK_DIM // 2
    swapped = [pltpu.roll(x[:, s * half:(s + 1) * half], half // 2, axis=1) for s in range(2)]
    return x * cos + jnp.concatenate(swapped, axis=-1) * sin


def _modproj_kernel(*refs, rope_tiles):
    if rope_tiles:
        x_ref, sh_ref, sc_ref, g_ref, w_ref, cos_ref, sin_ref, o_ref, h_ref = refs
    else:
        x_ref, sh_ref, sc_ref, g_ref, w_ref, o_ref, h_ref = refs
    j = pl.program_id(1)

    @pl.when(j == 0)
    def _():
        _modulate_into(h_ref, x_ref, g_ref, sh_ref, sc_ref)

    acc = jnp.dot(h_ref[...], w_ref[...], preferred_element_type=F32)
    if not rope_tiles:
        o_ref[...] = acc.astype(o_ref.dtype)
        return

    @pl.when(j < rope_tiles)
    def _():
        cos = cos_ref[...]
        sin = sin_ref[...]
        for s in range(acc.shape[1] // RET_QK_DIM):
            cs = slice(s * RET_QK_DIM, (s + 1) * RET_QK_DIM)
            o_ref[:, cs] = _rope(acc[:, cs], cos, sin).astype(o_ref.dtype)

    @pl.when(j >= rope_tiles)
    def _():
        o_ref[...] = acc.astype(o_ref.dtype)


def modproj(x, shift, scale, g, w, rows_per_mod, out_dtype, tm, tn, rope=None):
    m, d = x.shape
    n = w.shape[1]
    osz = jnp.dtype(out_dtype).itemsize
    vmem = 2 * tm * d * 4 + 2 * d * tn * 2 + 2 * tm * tn * osz + tm * d * 2 + 2 * tm * tn * 4 + VMEM_SLACK
    mod_idx = lambda i, j: ((i * tm) // rows_per_mod, 0, 0)
    in_specs = [
        pl.BlockSpec((tm, d), lambda i, j: (i, 0)),
        pl.BlockSpec((1, 1, d), mod_idx),
        pl.BlockSpec((1, 1, d), mod_idx),
        pl.BlockSpec((1, d), lambda i, j: (0, 0)),
        pl.BlockSpec((d, tn), lambda i, j: (0, j)),
    ]
    args = [x, shift, scale, g, w]
    rope_tiles = 0
    if rope is not None:
        seq_len, n_cols = rope
        rope_tiles = n_cols // tn
        cos, sin = _rope_tables(seq_len)
        tab = pl.BlockSpec((tm, RET_QK_DIM), lambda i, j: (i % (seq_len // tm), 0))
        in_specs += [tab, tab]
        args += [cos, sin]
        vmem += 2 * 2 * tm * RET_QK_DIM * 4
    return pl.pallas_call(
        functools.partial(_modproj_kernel, rope_tiles=rope_tiles),
        out_shape=jax.ShapeDtypeStruct((m, n), out_dtype),
        grid=(m // tm, n // tn),
        in_specs=in_specs,
        out_specs=pl.BlockSpec((tm, tn), lambda i, j: (i, j)),
        scratch_shapes=[pltpu.VMEM((tm, d), BF16)],
        compiler_params=_params(("arbitrary", "arbitrary"), vmem),
        name="modproj",
    )(*args)


def _head_rms(x, g):
    return x * lax.rsqrt(jnp.mean(x * x, axis=-1, keepdims=True) + EPS) * g


def _ctx_attn_kernel(q_ref, k_ref, v_ref, qg_ref, kg_ref, o_ref, nk_ref, nv_ref):
    scale = NA_HEAD_DIM ** -0.5
    for h in range(NA_HEADS):
        cs = slice(h * NA_HEAD_DIM, (h + 1) * NA_HEAD_DIM)
        q = _head_rms(q_ref[0, :, cs].astype(F32), qg_ref[...])
        k = _head_rms(k_ref[0, :, cs].astype(F32), kg_ref[...])
        v = v_ref[0, :, cs].astype(F32)
        nk_ref[0, 0, h] = k
        nv_ref[0, 0, h] = v
        s = lax.dot_general(q.astype(BF16), k.astype(BF16), (((1,), (1,)), ((), ())),
                            preferred_element_type=F32) * scale
        m = jnp.max(s, axis=-1, keepdims=True)
        e = jnp.exp(s - m)
        l = jnp.sum(e, axis=-1, keepdims=True)
        o = jnp.dot(e.astype(BF16), v.astype(BF16), preferred_element_type=F32) / l
        o_ref[0, :, cs] = o.astype(o_ref.dtype)


def ctx_attention(p, q_gain, k_gain):
    b, t, _ = p.shape
    cache_shape = (b, 1, NA_HEADS, t, NA_HEAD_DIM)
    blk = lambda c: pl.BlockSpec((1, t, NA_WIDTH), lambda i, c=c: (i, 0, c))
    gain = pl.BlockSpec((1, NA_HEAD_DIM), lambda i: (0, 0))
    cache_spec = pl.BlockSpec((1, 1, NA_HEADS, t, NA_HEAD_DIM), lambda i: (i, 0, 0, 0, 0))
    return pl.pallas_call(
        _ctx_attn_kernel,
        out_shape=(jax.ShapeDtypeStruct((b, t, NA_WIDTH), BF16),
                   jax.ShapeDtypeStruct(cache_shape, F32),
                   jax.ShapeDtypeStruct(cache_shape, F32)),
        grid=(b,),
        in_specs=[blk(1), blk(2), blk(3), gain, gain],
        out_specs=(pl.BlockSpec((1, t, NA_WIDTH), lambda i: (i, 0, 0)), cache_spec, cache_spec),
        compiler_params=_params(("arbitrary",), 32 << 20),
        name="ctx_attention",
    )(p, p, p, q_gain, k_gain)


NA_BIAS_ROWS = 2 * NA_KH - 1
NA_BIAS_COLS = 2 * NA_KW - 1
NA_MASK_TILE = NA_BIAS_ROWS
NA_QROWS = 4
NA_KROWS = NA_KH + NA_QROWS


def _na_build_bias(rb_ref, bias_ref, head):
    shape = (GRID_W, 2 * GRID_W)
    lane = lax.broadcasted_iota(jnp.int32, shape, 1)
    qc = lax.broadcasted_iota(jnp.int32, shape, 0)
    kc = lane & (GRID_W - 1)
    start = jnp.clip(qc - NA_KW // 2, 0, GRID_W - NA_KW)
    ok = (kc >= start) & (kc < start + NA_KW)
    delta = kc - qc + (NA_KW - 1)
    left = lane < GRID_W
    base = head * (NA_BIAS_ROWS * NA_BIAS_COLS)

    def build(dr, carry):
        acc = jnp.full(shape, NEG_INF, F32)
        for dc in range(NA_BIAS_COLS):
            acc = jnp.where(delta == dc, rb_ref[base + dr * NA_BIAS_COLS + dc], acc)
        tile = jnp.where(ok, acc, NEG_INF)
        bias_ref[0, dr] = jnp.where(left, tile, 0.0)
        bias_ref[1, dr] = jnp.where(left, 0.0, tile)
        return carry

    lax.fori_loop(0, NA_BIAS_ROWS, build, 0)
    bias_ref[0, NA_MASK_TILE] = jnp.where(left, NEG_INF, 0.0)
    bias_ref[1, NA_MASK_TILE] = jnp.where(left, 0.0, NEG_INF)


def _na_kernel(rb_ref, q_ref, k_ref, v_ref, kc_ref, vc_ref, qg_ref, kg_ref, o_ref,
               qs_ref, ks_ref, vs_ref, bias_ref, *, rows):
    @pl.when(pl.program_id(1) == 0)
    def _():
        _na_build_bias(rb_ref, bias_ref, pl.program_id(0))

    scale = NA_HEAD_DIM ** -0.5
    qs_ref[...] = (_head_rms(q_ref[0].astype(F32), qg_ref[...]) * scale).astype(BF16)
    ks_ref[...] = _head_rms(k_ref[0].astype(F32), kg_ref[...]).astype(BF16)
    vs_ref[...] = v_ref[0].astype(BF16)
    kc = kc_ref[0, 0, 0].astype(BF16)
    vc = vc_ref[0, 0, 0].astype(BF16)
    nt = (((1,), (1,)), ((), ()))
    nq = NA_QROWS * GRID_W
    nk = NA_KROWS * GRID_W

    def tile_index(r, rs, kr):
        inside = (kr >= rs) & (kr < rs + NA_KH)
        return jnp.where(inside, kr - r + (NA_KH - 1), NA_MASK_TILE)

    def body(blk, carry):
        r0 = blk * NA_QROWS
        k0 = jnp.clip(r0 - NA_KH // 2, 0, rows - NA_KROWS)
        q = qs_ref[pl.ds(pl.multiple_of(r0 * GRID_W, nq), nq), :]
        kw = ks_ref[pl.ds(pl.multiple_of(k0 * GRID_W, GRID_W), nk), :]
        vw = vs_ref[pl.ds(pl.multiple_of(k0 * GRID_W, GRID_W), nk), :]
        bias_rows = []
        for i in range(NA_QROWS):
            r = r0 + i
            rs = jnp.clip(r - NA_KH // 2, 0, rows - NA_KH)
            pairs = [bias_ref[0, tile_index(r, rs, k0 + 2 * jp)]
                     + bias_ref[1, tile_index(r, rs, k0 + 2 * jp + 1)]
                     for jp in range(NA_KROWS // 2)]
            bias_rows.append(jnp.concatenate(pairs, axis=-1))
        bias = jnp.concatenate(bias_rows, axis=0)
        s_w = lax.dot_general(q, kw, nt, preferred_element_type=F32) + bias
        s_c = lax.dot_general(q, kc, nt, preferred_element_type=F32)
        m = jnp.maximum(jnp.max(s_w, axis=-1, keepdims=True), jnp.max(s_c, axis=-1, keepdims=True))
        e_w = jnp.exp(s_w - m)
        e_c = jnp.exp(s_c - m)
        l = jnp.sum(e_w, axis=-1, keepdims=True) + jnp.sum(e_c, axis=-1, keepdims=True)
        o = (jnp.dot(e_w.astype(BF16), vw, preferred_element_type=F32)
             + jnp.dot(e_c.astype(BF16), vc, preferred_element_type=F32)) / l
        o_ref[0, pl.ds(pl.multiple_of(r0 * GRID_W, nq), nq), :] = o.astype(o_ref.dtype)
        return carry

    lax.fori_loop(0, rows // NA_QROWS, body, 0, unroll=2)


def na_attention(p, cache_k, cache_v, layer_j, rel_bias, q_gain, k_gain):
    b, t, _ = p.shape
    past = cache_k.shape[3]
    rows = t // GRID_W
    hd = NA_HEAD_DIM
    col0 = FNET_WIDTH // hd
    blk = lambda c: pl.BlockSpec((1, t, hd), lambda h, i, rb, c=c: (i, 0, col0 + c * NA_HEADS + h))
    cache_spec = pl.BlockSpec((1, 1, 1, past, hd), lambda h, i, rb: (i, layer_j, h, 0, 0))
    gain = pl.BlockSpec((1, hd), lambda h, i, rb: (0, 0))
    return pl.pallas_call(
        functools.partial(_na_kernel, rows=rows),
        out_shape=jax.ShapeDtypeStruct((b, t, NA_WIDTH), BF16),
        grid_spec=pltpu.PrefetchScalarGridSpec(
            num_scalar_prefetch=1,
            grid=(NA_HEADS, b),
            in_specs=[blk(0), blk(1), blk(2), cache_spec, cache_spec, gain, gain],
            out_specs=pl.BlockSpec((1, t, hd), lambda h, i, rb: (i, 0, h)),
            scratch_shapes=[pltpu.VMEM((t, hd), BF16)] * 3
            + [pltpu.VMEM((2, NA_BIAS_ROWS + 1, GRID_W, 2 * GRID_W), F32)],
        ),
        compiler_params=_params(("arbitrary", "arbitrary"), 40 << 20),
        name="na_attention",
    )(rel_bias.astype(F32).reshape(-1), p, p, p, cache_k, cache_v, q_gain, k_gain)


FNET_BLOCK = 16


def _fnet_tables(n1, n2):
    t_len = n1 * n2
    tb = FNET_BLOCK
    eye = np.eye(tb)
    k1 = np.arange(n1)
    ang1 = (2.0 * np.pi / n1) * ((k1[:, None] * k1[None, :]) % n1)
    f1 = np.concatenate([np.cos(ang1), -np.sin(ang1)], axis=0) / math.sqrt(t_len)
    rows_mat = np.kron(f1, eye)
    t2 = np.arange(n2)
    tw = (2.0 * np.pi / t_len) * (k1[:, None] * t2[None, :])
    tw = tw.reshape(n1, n2 // tb, tb).transpose(1, 0, 2).reshape(n2 // tb, n1 * tb, 1)
    c = np.arange(FNET_GROUP_DIM)
    ang3 = (2.0 * np.pi / FNET_GROUP_DIM) * ((c[:, None] * c[None, :]) % FNET_GROUP_DIM)
    c3, s3 = np.cos(ang3), np.sin(ang3)
    chan = np.block([[c3, -s3], [s3, c3]]) / math.sqrt(FNET_GROUP_DIM)
    ang2 = (2.0 * np.pi / n2) * ((t2[:, None] * t2[None, :]) % n2)
    cols_re = np.einsum("pq,kt->kpqt", eye, np.cos(ang2)).reshape(n2 * tb, tb * n2)
    cols_im = np.einsum("pq,kt->kpqt", eye, np.sin(ang2)).reshape(n2 * tb, tb * n2)
    bf = lambda x: jnp.asarray(x, dtype=BF16)
    f32 = lambda x: jnp.asarray(x, dtype=F32)
    return bf(rows_mat), f32(np.cos(tw)), f32(np.sin(tw)), bf(chan), bf(cols_re), bf(cols_im)


def _fnet_rows_kernel(x_ref, mat_ref, twc_ref, tws_ref, chan_ref, vr_ref, vi_ref):
    _, n1, tb, width = x_ref.shape
    rows = n1 * tb
    gd = FNET_GROUP_DIM
    x = x_ref[0].reshape(rows, width).astype(BF16)
    h = jnp.dot(mat_ref[...], x, preferred_element_type=F32)
    hr, hi = h[:rows], h[rows:]
    c, s = twc_ref[0], tws_ref[0]
    gr = (hr * c + hi * s).astype(BF16)
    gi = (hi * c - hr * s).astype(BF16)
    stacked = jnp.concatenate(
        [jnp.concatenate([gr[:, g * gd:(g + 1) * gd], gi[:, g * gd:(g + 1) * gd]], axis=1)
         for g in range(FNET_GROUPS)], axis=0)
    v = jnp.dot(stacked, chan_ref[...], preferred_element_type=F32)
    for g in range(FNET_GROUPS):
        part = v[g * rows:(g + 1) * rows]
        vr_ref[0, :, :, g * gd:(g + 1) * gd] = part[:, :gd].reshape(n1, tb, gd).astype(vr_ref.dtype)
        vi_ref[0, :, :, g * gd:(g + 1) * gd] = part[:, gd:].reshape(n1, tb, gd).astype(vi_ref.dtype)


def _fnet_cols_kernel(vr_ref, vi_ref, re_ref, im_ref, o_ref):
    _, kb, n2, width = vr_ref.shape
    vr = vr_ref[0].reshape(kb * n2, width)
    vi = vi_ref[0].reshape(kb * n2, width)
    y = (jnp.dot(re_ref[...], vr, preferred_element_type=F32)
         + jnp.dot(im_ref[...], vi, preferred_element_type=F32))
    o_ref[0] = y.reshape(n2, kb, width).astype(o_ref.dtype)


def fourier_mix(p, n1, n2):
    b, t, width = p.shape
    tb = FNET_BLOCK
    w = FNET_WIDTH
    rows_mat, twc, tws, chan, cols_re, cols_im = _fnet_tables(n1, n2)
    const = lambda shape: pl.BlockSpec(shape, lambda i, s: (0,) * len(shape))
    tw_spec = pl.BlockSpec((1, n1 * tb, 1), lambda i, s: (s, 0, 0))
    v_shape = jax.ShapeDtypeStruct((b, n1, n2, w), BF16)
    v_spec = pl.BlockSpec((1, n1, tb, w), lambda i, s: (i, 0, s, 0))
    isz = p.dtype.itemsize
    vmem = (2 * n1 * tb * w * isz + 2 * rows_mat.size * 2 + 4 * n1 * tb * w * 2
            + 6 * n1 * tb * w * 4 + VMEM_SLACK)
    vr, vi = pl.pallas_call(
        _fnet_rows_kernel,
        out_shape=(v_shape, v_shape),
        grid=(b, n2 // tb),
        in_specs=[pl.BlockSpec((1, n1, tb, w), lambda i, s: (i, 0, s, 0)),
                  const(rows_mat.shape), tw_spec, tw_spec, const(chan.shape)],
        out_specs=(v_spec, v_spec),
        compiler_params=_params(("arbitrary", "arbitrary"), vmem),
        name="fnet_rows",
    )(p.reshape(b, n1, n2, width), rows_mat, twc, tws, chan)
    v_in = pl.BlockSpec((1, tb, n2, w), lambda i, s: (i, s, 0, 0))
    vmem = 2 * 2 * tb * n2 * w * 2 + 2 * 2 * cols_re.size * 2 + 2 * n2 * tb * w * 2 + 3 * n2 * tb * w * 4 + VMEM_SLACK
    out = pl.pallas_call(
        _fnet_cols_kernel,
        out_shape=jax.ShapeDtypeStruct((b, n2, n1, w), BF16),
        grid=(b, n1 // tb),
        in_specs=[v_in, v_in, const(cols_re.shape), const(cols_im.shape)],
        out_specs=pl.BlockSpec((1, n2, tb, w), lambda i, s: (i, 0, s, 0)),
        compiler_params=_params(("arbitrary", "arbitrary"), vmem),
        name="fnet_cols",
    )(vr, vi, cols_re, cols_im)
    return out.reshape(b, t, w)


def _even_out_kernel(f_ref, a_ref, wf_ref, wa_ref, x_ref, gate_ref, o_ref):
    y = (jnp.dot(f_ref[...], wf_ref[...], preferred_element_type=F32)
         + jnp.dot(a_ref[...], wa_ref[...], preferred_element_type=F32))
    o_ref[...] = x_ref[...] + gate_ref[0] * y


def even_out(f, a, w, x, gate, rows_per_mod, tm, tn):
    m, d = x.shape
    kf = f.shape[1]
    vmem = 2 * 2 * tm * kf * 2 + 2 * 2 * kf * tn * 2 + 4 * tm * tn * 4 + tm * tn * 4 + VMEM_SLACK
    return pl.pallas_call(
        _even_out_kernel,
        out_shape=jax.ShapeDtypeStruct((m, d), F32),
        grid=(d // tn, m // tm),
        in_specs=[
            pl.BlockSpec((tm, kf), lambda j, i: (i, 0)),
            pl.BlockSpec((tm, kf), lambda j, i: (i, 0)),
            pl.BlockSpec((kf, tn), lambda j, i: (0, j)),
            pl.BlockSpec((kf, tn), lambda j, i: (1, j)),
            pl.BlockSpec((tm, tn), lambda j, i: (i, j)),
            pl.BlockSpec((1, 1, tn), lambda j, i: ((i * tm) // rows_per_mod, 0, j)),
        ],
        out_specs=pl.BlockSpec((tm, tn), lambda j, i: (i, j)),
        compiler_params=_params(("arbitrary", "arbitrary"), vmem),
        name="even_out",
    )(f, a, w, w, x, gate)


def _gated_group_norm(o, g):
    o = o.astype(F32)
    oc = o - jnp.mean(o, axis=-1, keepdims=True)
    gn = oc * lax.rsqrt(jnp.mean(oc * oc, axis=-1, keepdims=True) + EPS)
    hg = 0.5 * g.astype(F32)
    return (hg + hg * jnp.tanh(hg)) * gn


def _odd_out_kernel(of_ref, ob_ref, gf_ref, gb_ref, w_ref, x_ref, gate_ref, o_ref):
    acc = None
    for h in range(RET_HEADS):
        cs = slice(h * RET_V_DIM, (h + 1) * RET_V_DIM)
        y = (_gated_group_norm(of_ref[:, cs], gf_ref[:, cs])
             + _gated_group_norm(ob_ref[:, cs], gb_ref[:, cs])).astype(BF16)
        part = jnp.dot(y, w_ref[cs, :], preferred_element_type=F32)
        acc = part if acc is None else acc + part
    o_ref[...] = x_ref[...] + gate_ref[0] * acc


def odd_out(o_f, o_b, p, w, x, gate, rows_per_mod, tm):
    m, d = x.shape
    k = w.shape[0]
    gcol = (2 * RET_QK_WIDTH + RET_V_WIDTH) // RET_V_WIDTH
    vmem = (2 * 4 * tm * k * 2 + k * d * 2 + 4 * tm * d * 4 + 2 * tm * d * 4
            + 6 * tm * RET_V_DIM * 4 + VMEM_SLACK)
    act = lambda c: pl.BlockSpec((tm, k), lambda i, c=c: (i, c))
    return pl.pallas_call(
        _odd_out_kernel,
        out_shape=jax.ShapeDtypeStruct((m, d), F32),
        grid=(m // tm,),
        in_specs=[
            act(0), act(0), act(gcol), act(gcol + 1),
            pl.BlockSpec((k, d), lambda i: (0, 0), pipeline_mode=pl.Buffered(1)),
            pl.BlockSpec((tm, d), lambda i: (i, 0)),
            pl.BlockSpec((1, 1, d), lambda i: ((i * tm) // rows_per_mod, 0, 0)),
        ],
        out_specs=pl.BlockSpec((tm, d), lambda i: (i, 0)),
        compiler_params=_params(("arbitrary",), vmem),
        name="odd_out",
    )(o_f, o_b, p, p, w, x, gate)


def _ffn_kernel(x_ref, sh_ref, sc_ref, gate_ref, g_ref, wg_ref, wu_ref, wd_ref, o_ref, h_ref):
    j = pl.program_id(1)

    @pl.when(j == 0)
    def _():
        _modulate_into(h_ref, x_ref, g_ref, sh_ref, sc_ref)
        o_ref[...] = jnp.zeros_like(o_ref)

    h = h_ref[...]
    a = jnp.dot(h, wg_ref[0], preferred_element_type=F32)
    u = jnp.dot(h, wu_ref[0], preferred_element_type=F32)
    act = (_silu(a) * u).astype(BF16)
    o_ref[...] += jnp.dot(act, wd_ref[0], preferred_element_type=F32)

    @pl.when(j == pl.num_programs(1) - 1)
    def _():
        o_ref[...] = x_ref[...] + gate_ref[0] * o_ref[...]


def ffn_block(x, shift, scale, gate, g, w_gate, w_up, w_down, layer, rows_per_mod, tm, tf):
    m, d = x.shape
    f = w_gate.shape[2]
    vmem = 4 * tm * d * 4 + tm * d * 2 + 2 * 3 * d * tf * 2 + 4 * tm * tf * 4 + tm * d * 4 + VMEM_SLACK
    mod_idx = lambda i, j: ((i * tm) // rows_per_mod, 0, 0)
    mod = pl.BlockSpec((1, 1, d), mod_idx)
    return pl.pallas_call(
        _ffn_kernel,
        out_shape=jax.ShapeDtypeStruct((m, d), F32),
        grid=(m // tm, f // tf),
        in_specs=[
            pl.BlockSpec((tm, d), lambda i, j: (i, 0)),
            mod, mod, mod,
            pl.BlockSpec((1, d), lambda i, j: (0, 0)),
            pl.BlockSpec((1, d, tf), lambda i, j: (layer, 0, j)),
            pl.BlockSpec((1, d, tf), lambda i, j: (layer, 0, j)),
            pl.BlockSpec((1, tf, d), lambda i, j: (layer, j, 0)),
        ],
        out_specs=pl.BlockSpec((tm, d), lambda i, j: (i, 0)),
        scratch_shapes=[pltpu.VMEM((tm, d), BF16)],
        compiler_params=_params(("arbitrary", "arbitrary"), vmem),
        name="ffn_block",
    )(x, shift, scale, gate, g, w_gate, w_up, w_down)


def _retention_kernel(lg_ref, *refs, latent, n_blocks):
    if latent:
        (qf_ref, kf_ref, vf_ref, qb_ref, kb_ref, vb_ref, s0_ref,
         of_ref, ob_ref, s_ref, dec_ref, qd_ref, kd_ref) = refs
    else:
        (qf_ref, kf_ref, vf_ref, qb_ref, kb_ref, vb_ref,
         of_ref, ob_ref, sfin_ref, s_ref, dec_ref, qd_ref, kd_ref) = refs
    h0 = pl.program_id(0) * RET_HEADS_PER_STEP
    i = pl.program_id(2)
    c = RET_BLOCK
    dk, dv = RET_QK_DIM, RET_V_DIM
    scale = RET_QK_DIM ** -0.5
    scans = [(d, hh) for d in range(2) for hh in range(RET_HEADS_PER_STEP)]

    @pl.when((pl.program_id(1) == 0) & (i == 0))
    def _():
        row = lax.broadcasted_iota(jnp.int32, (c, c), 0).astype(F32)
        col = lax.broadcasted_iota(jnp.int32, (c, c), 1).astype(F32)
        t_idx = lax.broadcasted_iota(jnp.int32, (c, 1), 0).astype(F32)
        for d, hh in scans:
            lg = lg_ref[d, h0 + hh]
            diff = row - col if d == 0 else col - row
            dec_ref[d, hh] = jnp.where(diff >= 0, jnp.exp(lg * jnp.maximum(diff, 0.0)), 0.0) * scale
            pos = t_idx if d == 0 else (c - 1.0) - t_idx
            qd_ref[d, hh] = jnp.exp(lg * (pos + 1.0))
            kd_ref[d, hh] = jnp.exp(lg * ((c - 1.0) - pos)) * scale

    @pl.when(i == 0)
    def _():
        for d, hh in scans:
            s_ref[d, hh] = s0_ref[0, 0, d, hh] if latent else jnp.zeros((dk, dv), F32)

    nt = (((1,), (1,)), ((), ()))
    tn = (((0,), (0,)), ((), ()))
    chunk_refs = ((qf_ref, kf_ref, vf_ref, of_ref), (qb_ref, kb_ref, vb_ref, ob_ref))
    for d, hh in scans:
        q_ref, k_ref, v_ref, o_ref = chunk_refs[d]
        q = q_ref[0, :, hh * dk:(hh + 1) * dk]
        k = k_ref[0, :, hh * dk:(hh + 1) * dk]
        v = v_ref[0, :, hh * dv:(hh + 1) * dv]
        a = lax.dot_general(q, k, nt, preferred_element_type=F32)
        inner_w = (a * dec_ref[d, hh]).astype(BF16)
        q_dec = (q.astype(F32) * qd_ref[d, hh]).astype(BF16)
        k_dec = (k.astype(F32) * kd_ref[d, hh]).astype(BF16)
        s_old = s_ref[d, hh]
        o = (jnp.dot(inner_w, v, preferred_element_type=F32)
             + jnp.dot(q_dec, s_old.astype(BF16), preferred_element_type=F32))
        s_new = (s_old * jnp.exp(lg_ref[d, h0 + hh] * c)
                 + lax.dot_general(k_dec, v, tn, preferred_element_type=F32))
        s_ref[d, hh] = s_new
        o_ref[0, :, hh * dv:(hh + 1) * dv] = o.astype(o_ref.dtype)
        if not latent:
            @pl.when(i == n_blocks - 1)
            def _(d=d, hh=hh, s_new=s_new):
                sfin_ref[0, 0, d, hh] = s_new


def retention(p, log_gamma, state0, layer_j):
    b, t, _ = p.shape
    latent = state0 is not None
    c = RET_BLOCK
    nb = t // c
    hps = RET_HEADS_PER_STEP
    dk, dv, nh = hps * RET_QK_DIM, hps * RET_V_DIM, RET_HEADS // hps

    def chunk_specs(blk_of):
        return [
            pl.BlockSpec((1, c, dk), lambda h, bi, i, lg: (bi, blk_of(i), h)),
            pl.BlockSpec((1, c, dk), lambda h, bi, i, lg: (bi, blk_of(i), nh + h)),
            pl.BlockSpec((1, c, dv), lambda h, bi, i, lg: (bi, blk_of(i), nh + h)),
        ]

    fwd = lambda i: i
    bwd = lambda i: nb - 1 - i
    in_specs = chunk_specs(fwd) + chunk_specs(bwd)
    args = [p] * 6
    o_shape = jax.ShapeDtypeStruct((b, t, RET_V_WIDTH), BF16)
    out_shape = [o_shape, o_shape]
    out_specs = [pl.BlockSpec((1, c, dv), lambda h, bi, i, lg: (bi, fwd(i), h)),
                 pl.BlockSpec((1, c, dv), lambda h, bi, i, lg: (bi, bwd(i), h))]
    state_dims = (RET_QK_DIM, RET_V_DIM)
    state_block = (1, 1, 2, hps) + state_dims
    if latent:
        in_specs.append(pl.BlockSpec(state_block, lambda h, bi, i, lg: (bi, layer_j, 0, h, 0, 0)))
        args.append(state0)
    else:
        out_shape.append(jax.ShapeDtypeStruct((b, 1, 2, RET_HEADS) + state_dims, F32))
        out_specs.append(pl.BlockSpec(state_block, lambda h, bi, i, lg: (bi, 0, 0, h, 0, 0)))
    return pl.pallas_call(
        functools.partial(_retention_kernel, latent=latent, n_blocks=nb),
        out_shape=tuple(out_shape),
        grid_spec=pltpu.PrefetchScalarGridSpec(
            num_scalar_prefetch=1,
            grid=(nh, b, nb),
            in_specs=in_specs,
            out_specs=tuple(out_specs),
            scratch_shapes=[pltpu.VMEM((2, hps) + state_dims, F32), pltpu.VMEM((2, hps, c, c), F32),
                            pltpu.VMEM((2, hps, c, 1), F32), pltpu.VMEM((2, hps, c, 1), F32)],
        ),
        compiler_params=_params(("arbitrary",) * 3, 40 << 20),
        name="retention",
    )(log_gamma, *args)


def _trunk(x, mods, ctx_k, ctx_v, ctx_state, wts, p_dtype):
    b, t, d = x.shape
    is_ctx = ctx_k is None
    nbm = mods.shape[1]
    rows_per_mod = (b * t) // nbm
    x2 = x.reshape(b * t, d)
    new_k = new_v = new_s = None
    for i in range(DEPTH):
        mod = [mods[i, :, k][:, None, :] for k in range(6)]
        j = i // 2
        if i % 2 == 0:
            p = modproj(x2, mod[0], mod[1], wts["norm_g"][i, 0][None], wts["even_w_in"][j],
                        rows_per_mod, p_dtype, tm=1024,
                        tn=1024 if p_dtype == BF16 else 512).reshape(b, t, EVEN_IN)
            qg, kg = wts["even_q_norm"][j][None], wts["even_k_norm"][j][None]
            if is_ctx:
                attn, new_k, new_v = ctx_attention(p, qg, kg)
            else:
                attn = na_attention(p, ctx_k, ctx_v, j, wts["na_rel_bias"][j], qg, kg)
            n2 = GRID_W if not is_ctx else math.isqrt(t)
            fm = fourier_mix(p, t // n2, n2)
            x2 = even_out(fm.reshape(b * t, FNET_WIDTH), attn.reshape(b * t, NA_WIDTH),
                          wts["even_w_out"][j], x2, mod[2], rows_per_mod, tm=512, tn=1024)
        else:
            rope = (t, 2 * RET_QK_WIDTH) if not is_ctx else None
            p = modproj(x2, mod[0], mod[1], wts["norm_g"][i, 0][None], wts["odd_w_in"][j],
                        rows_per_mod, BF16, tm=1024, tn=1024, rope=rope)
            lg = jax.nn.log_sigmoid(wts["ret_decay_logit"][j].astype(F32))
            if is_ctx:
                o_f, o_b, new_s = retention(p.reshape(b, t, ODD_IN), lg, None, j)
            else:
                o_f, o_b = retention(p.reshape(b, t, ODD_IN), lg, ctx_state, j)
            x2 = odd_out(o_f.reshape(b * t, RET_V_WIDTH), o_b.reshape(b * t, RET_V_WIDTH), p,
                         wts["odd_w_out"][j], x2, mod[2], rows_per_mod, tm=256)
        x2 = ffn_block(x2, mod[3], mod[4], mod[5], wts["norm_g"][i, 1][None], wts["ffn_w_gate"],
                       wts["ffn_w_up"], wts["ffn_w_down"], i, rows_per_mod, tm=512, tf=512)
    return x2.reshape(b, t, d), new_k, new_v, new_s


def kernel(x_prompt, x_sample, cache_k, cache_v, state_ret, c, c_ctx, ada_w, ada_b, norm_g,
           even_w_in, even_q_norm, even_k_norm, na_rel_bias, even_w_out, odd_w_in, ret_decay_logit,
           odd_w_out, ffn_w_gate, ffn_w_up, ffn_w_down):
    nb_lat = c.shape[0]
    cond = jnp.concatenate(
        [c, c_ctx[None, :], jnp.zeros((MOD_ROWS - nb_lat - 1, D_MODEL), F32)], axis=0)
    mods = adaln_all(cond, ada_w, ada_b).reshape(DEPTH, MOD_ROWS, 6, D_MODEL)
    wts = dict(
        norm_g=norm_g, even_q_norm=even_q_norm, even_k_norm=even_k_norm, na_rel_bias=na_rel_bias,
        ret_decay_logit=ret_decay_logit,
        even_w_in=even_w_in.astype(BF16), even_w_out=even_w_out.astype(BF16),
        odd_w_in=odd_w_in.astype(BF16), odd_w_out=odd_w_out.astype(BF16),
        ffn_w_gate=ffn_w_gate.astype(BF16), ffn_w_up=ffn_w_up.astype(BF16),
        ffn_w_down=ffn_w_down.astype(BF16),
    )
    y_prompt, new_k, new_v, new_s = _trunk(
        x_prompt, mods[:, nb_lat:nb_lat + 1], None, None, None, wts, F32)
    y_sample, _, _, _ = _trunk(x_sample, mods[:, :nb_lat], cache_k, cache_v, state_ret, wts, BF16)
    return (y_prompt, y_sample, new_k, new_v, new_s)
```

```python
import functools
import math

import numpy as np
import jax
import jax.numpy as jnp
from jax import lax
from jax.experimental import pallas as pl
from jax.experimental.pallas import tpu as pltpu

F32 = jnp.float32
BF16 = jnp.bfloat16

D_MODEL = 2048
DEPTH = 2
GRID_W = 64
EPS = 1e-6
NEG_INF = -1e30
FNET_GROUPS = 8
FNET_GROUP_DIM = 128
FNET_WIDTH = 1024
NA_HEADS = 8
NA_HEAD_DIM = 128
NA_WIDTH = 1024
NA_KH = 8
NA_KW = 16
EVEN_IN = FNET_WIDTH + 3 * NA_WIDTH
RET_HEADS = 8
RET_QK_DIM = 256
RET_V_DIM = 512
RET_QK_WIDTH = RET_HEADS * RET_QK_DIM
RET_V_WIDTH = RET_HEADS * RET_V_DIM
ODD_IN = 2 * RET_QK_WIDTH + 3 * RET_V_WIDTH
RET_BLOCK = 256
RET_HEADS_PER_STEP = 4
ROPE_BASE = 10000.0
D_FF = 5632
MOD_ROWS = 8
MOD_CHUNK = 128

V7X_VMEM_BUDGET = 56 * 1024 * 1024
VMEM_SLACK = 8 * 1024 * 1024


def _params(semantics, vmem_bytes):
    return pltpu.CompilerParams(dimension_semantics=semantics,
                                vmem_limit_bytes=min(int(vmem_bytes), V7X_VMEM_BUDGET))


def _silu(x):
    return x * (1.0 / (1.0 + jnp.exp(-x)))


def _adaln_kernel(c_ref, w_ref, b_ref, o_ref):
    s = _silu(c_ref[...]).astype(BF16)
    w = w_ref[0].astype(BF16)
    o_ref[0] = jnp.dot(s, w, preferred_element_type=F32) + b_ref[0]


def adaln_all(cond, ada_w, ada_b):
    n = ada_w.shape[-1]
    tn = 1024
    return pl.pallas_call(
        _adaln_kernel,
        out_shape=jax.ShapeDtypeStruct((DEPTH, MOD_ROWS, n), F32),
        grid=(DEPTH, n // tn),
        in_specs=[
            pl.BlockSpec((MOD_ROWS, D_MODEL), lambda l, j: (0, 0)),
            pl.BlockSpec((1, D_MODEL, tn), lambda l, j: (l, 0, j)),
            pl.BlockSpec((1, 1, tn), lambda l, j: (l, 0, j)),
        ],
        out_specs=pl.BlockSpec((1, MOD_ROWS, tn), lambda l, j: (l, 0, j)),
        compiler_params=_params(("arbitrary", "arbitrary"), 40 << 20),
        name="adaln",
    )(cond, ada_w, ada_b.reshape(DEPTH, 1, n))


def _modulate_into(h_ref, x_ref, g_ref, sh_ref, sc_ref):
    shift = sh_ref[0]
    gain = g_ref[...] * (1.0 + sc_ref[0])

    def body(c, carry):
        rows = pl.ds(pl.multiple_of(c * MOD_CHUNK, MOD_CHUNK), MOD_CHUNK)
        x = x_ref[rows, :]
        ms = jnp.mean(x * x, axis=-1, keepdims=True)
        h_ref[rows, :] = (x * lax.rsqrt(ms + EPS) * gain + shift).astype(h_ref.dtype)
        return carry

    lax.fori_loop(0, x_ref.shape[0] // MOD_CHUNK, body, 0)


ROPE_QUARTER = RET_QK_DIM // 4


def _pair_order_columns(w):
    n, d, _ = w.shape
    qk = w[..., :2 * RET_QK_WIDTH].reshape(n, d, 2 * RET_HEADS, 2, 2, ROPE_QUARTER)
    qk = qk.transpose(0, 1, 2, 4, 3, 5).reshape(n, d, 2 * RET_QK_WIDTH)
    return jnp.concatenate([qk, w[..., 2 * RET_QK_WIDTH:]], axis=-1)


def _pair_order_rows(s):
    q = ROPE_QUARTER
    return jnp.concatenate([s[:q], s[2 * q:3 * q], s[q:2 * q], s[3 * q:]], axis=0)


def _rope_tables(t_len):
    inv = ROPE_BASE ** (-jnp.arange(ROPE_QUARTER, dtype=F32) / ROPE_QUARTER)
    t = jnp.arange(t_len)
    ang = jnp.concatenate([(t // GRID_W).astype(F32)[:, None] * inv[None, :],
                           (t % GRID_W).astype(F32)[:, None] * inv[None, :]], axis=-1)
    return jnp.cos(ang), jnp.sin(ang)


def _rope(x, cos, sin):
    half = RET_QK_DIM // 2
    x1, x2 = x[:, :half], x[:, half:]
    return jnp.concatenate([x1 * cos - x2 * sin, x1 * sin + x2 * cos], axis=-1)


def _modproj_kernel(*refs, rope_tiles):
    if rope_tiles:
        x_ref, sh_ref, sc_ref, g_ref, w_ref, cos_ref, sin_ref, o_ref, h_ref = refs
    else:
        x_ref, sh_ref, sc_ref, g_ref, w_ref, o_ref, h_ref = refs
    j = pl.program_id(1)

    @pl.when(j == 0)
    def _():
        _modulate_into(h_ref, x_ref, g_ref, sh_ref, sc_ref)

    acc = jnp.dot(h_ref[...], w_ref[...], preferred_element_type=F32)
    if not rope_tiles:
        o_ref[...] = acc.astype(o_ref.dtype)
        return

    @pl.when(j < rope_tiles)
    def _():
        cos = cos_ref[...]
        sin = sin_ref[...]
        for s in range(acc.shape[1] // RET_QK_DIM):
            cs = slice(s * RET_QK_DIM, (s + 1) * RET_QK_DIM)
            o_ref[:, cs] = _rope(acc[:, cs], cos, sin).astype(o_ref.dtype)

    @pl.when(j >= rope_tiles)
    def _():
        o_ref[...] = acc.astype(o_ref.dtype)


def modproj(x, shift, scale, g, w, rows_per_mod, out_dtype, tm, tn, rope=None):
    m, d = x.shape
    n = w.shape[1]
    osz = jnp.dtype(out_dtype).itemsize
    vmem = 2 * tm * d * 4 + 2 * d * tn * 2 + 2 * tm * tn * osz + tm * d * 2 + 2 * tm * tn * 4 + VMEM_SLACK
    mod_idx = lambda i, j: ((i * tm) // rows_per_mod, 0, 0)
    in_specs = [
        pl.BlockSpec((tm, d), lambda i, j: (i, 0)),
        pl.BlockSpec((1, 1, d), mod_idx),
        pl.BlockSpec((1, 1, d), mod_idx),
        pl.BlockSpec((1, d), lambda i, j: (0, 0)),
        pl.BlockSpec((d, tn), lambda i, j: (0, j)),
    ]
    args = [x, shift, scale, g, w]
    rope_tiles = 0
    if rope is not None:
        seq_len, n_cols = rope
        rope_tiles = n_cols // tn
        cos, sin = _rope_tables(seq_len)
        tab = pl.BlockSpec((tm, RET_QK_DIM // 2), lambda i, j: (i % (seq_len // tm), 0))
        in_specs += [tab, tab]
        args += [cos, sin]
        vmem += 2 * 2 * tm * (RET_QK_DIM // 2) * 4
    return pl.pallas_call(
        functools.partial(_modproj_kernel, rope_tiles=rope_tiles),
        out_shape=jax.ShapeDtypeStruct((m, n), out_dtype),
        grid=(m // tm, n // tn),
        in_specs=in_specs,
        out_specs=pl.BlockSpec((tm, tn), lambda i, j: (i, j)),
        scratch_shapes=[pltpu.VMEM((tm, d), BF16)],
        compiler_params=_params(("arbitrary", "arbitrary"), vmem),
        name="modproj",
    )(*args)


def _head_rms(x, g):
    return x * lax.rsqrt(jnp.mean(x * x, axis=-1, keepdims=True) + EPS) * g


def _ctx_attn_kernel(q_ref, k_ref, v_ref, qg_ref, kg_ref, o_ref, nk_ref, nv_ref):
    scale = NA_HEAD_DIM ** -0.5
    for h in range(NA_HEADS):
        cs = slice(h * NA_HEAD_DIM, (h + 1) * NA_HEAD_DIM)
        q = _head_rms(q_ref[0, :, cs].astype(F32), qg_ref[...])
        k = _head_rms(k_ref[0, :, cs].astype(F32), kg_ref[...])
        v = v_ref[0, :, cs].astype(F32)
        nk_ref[0, 0, h] = k
        nv_ref[0, 0, h] = v
        s = lax.dot_general(q.astype(BF16), k.astype(BF16), (((1,), (1,)), ((), ())),
                            preferred_element_type=F32) * scale
        m = jnp.max(s, axis=-1, keepdims=True)
        e = jnp.exp(s - m)
        l = jnp.sum(e, axis=-1, keepdims=True)
        o = jnp.dot(e.astype(BF16), v.astype(BF16), preferred_element_type=F32) / l
        o_ref[0, :, cs] = o.astype(o_ref.dtype)


def ctx_attention(p, q_gain, k_gain):
    b, t, _ = p.shape
    cache_shape = (b, 1, NA_HEADS, t, NA_HEAD_DIM)
    blk = lambda c: pl.BlockSpec((1, t, NA_WIDTH), lambda i, c=c: (i, 0, c))
    gain = pl.BlockSpec((1, NA_HEAD_DIM), lambda i: (0, 0))
    cache_spec = pl.BlockSpec((1, 1, NA_HEADS, t, NA_HEAD_DIM), lambda i: (i, 0, 0, 0, 0))
    return pl.pallas_call(
        _ctx_attn_kernel,
        out_shape=(jax.ShapeDtypeStruct((b, t, NA_WIDTH), BF16),
                   jax.ShapeDtypeStruct(cache_shape, F32),
                   jax.ShapeDtypeStruct(cache_shape, F32)),
        grid=(b,),
        in_specs=[blk(1), blk(2), blk(3), gain, gain],
        out_specs=(pl.BlockSpec((1, t, NA_WIDTH), lambda i: (i, 0, 0)), cache_spec, cache_spec),
        compiler_params=_params(("arbitrary",), 32 << 20),
        name="ctx_attention",
    )(p, p, p, q_gain, k_gain)


NA_BIAS_ROWS = 2 * NA_KH - 1
NA_BIAS_COLS = 2 * NA_KW - 1
NA_MASK_TILE = NA_BIAS_ROWS
NA_QROWS = 4
NA_KROWS = NA_KH + NA_QROWS


def _na_build_bias(rb_ref, bias_ref, head):
    shape = (GRID_W, 2 * GRID_W)
    lane = lax.broadcasted_iota(jnp.int32, shape, 1)
    qc = lax.broadcasted_iota(jnp.int32, shape, 0)
    kc = lane & (GRID_W - 1)
    start = jnp.clip(qc - NA_KW // 2, 0, GRID_W - NA_KW)
    ok = (kc >= start) & (kc < start + NA_KW)
    delta = kc - qc + (NA_KW - 1)
    left = lane < GRID_W
    base = head * (NA_BIAS_ROWS * NA_BIAS_COLS)

    def build(dr, carry):
        acc = jnp.full(shape, NEG_INF, F32)
        for dc in range(NA_BIAS_COLS):
            acc = jnp.where(delta == dc, rb_ref[base + dr * NA_BIAS_COLS + dc], acc)
        tile = jnp.where(ok, acc, NEG_INF)
        bias_ref[0, dr] = jnp.where(left, tile, 0.0)
        bias_ref[1, dr] = jnp.where(left, 0.0, tile)
        return carry

    lax.fori_loop(0, NA_BIAS_ROWS, build, 0)
    bias_ref[0, NA_MASK_TILE] = jnp.where(left, NEG_INF, 0.0)
    bias_ref[1, NA_MASK_TILE] = jnp.where(left, 0.0, NEG_INF)


def _na_kernel(rb_ref, q_ref, k_ref, v_ref, kc_ref, vc_ref, qg_ref, kg_ref, o_ref,
               qs_ref, ks_ref, vs_ref, bias_ref, *, rows):
    @pl.when(pl.program_id(1) == 0)
    def _():
        _na_build_bias(rb_ref, bias_ref, pl.program_id(0))

    scale = NA_HEAD_DIM ** -0.5
    qs_ref[...] = (_head_rms(q_ref[0].astype(F32), qg_ref[...]) * scale).astype(BF16)
    ks_ref[...] = _head_rms(k_ref[0].astype(F32), kg_ref[...]).astype(BF16)
    vs_ref[...] = v_ref[0].astype(BF16)
    kc = kc_ref[0, 0, 0].astype(BF16)
    vc = vc_ref[0, 0, 0].astype(BF16)
    nt = (((1,), (1,)), ((), ()))
    nq = NA_QROWS * GRID_W
    nk = NA_KROWS * GRID_W

    def tile_index(r, rs, kr):
        inside = (kr >= rs) & (kr < rs + NA_KH)
        return jnp.where(inside, kr - r + (NA_KH - 1), NA_MASK_TILE)

    def body(blk, carry):
        r0 = blk * NA_QROWS
        k0 = jnp.clip(r0 - NA_KH // 2, 0, rows - NA_KROWS)
        q = qs_ref[pl.ds(pl.multiple_of(r0 * GRID_W, nq), nq), :]
        kw = ks_ref[pl.ds(pl.multiple_of(k0 * GRID_W, GRID_W), nk), :]
        vw = vs_ref[pl.ds(pl.multiple_of(k0 * GRID_W, GRID_W), nk), :]
        bias_rows = []
        for i in range(NA_QROWS):
            r = r0 + i
            rs = jnp.clip(r - NA_KH // 2, 0, rows - NA_KH)
            pairs = [bias_ref[0, tile_index(r, rs, k0 + 2 * jp)]
                     + bias_ref[1, tile_index(r, rs, k0 + 2 * jp + 1)]
                     for jp in range(NA_KROWS // 2)]
            bias_rows.append(jnp.concatenate(pairs, axis=-1))
        bias = jnp.concatenate(bias_rows, axis=0)
        s_w = lax.dot_general(q, kw, nt, preferred_element_type=F32) + bias
        s_c = lax.dot_general(q, kc, nt, preferred_element_type=F32)
        m = jnp.maximum(jnp.max(s_w, axis=-1, keepdims=True), jnp.max(s_c, axis=-1, keepdims=True))
        e_w = jnp.exp(s_w - m)
        e_c = jnp.exp(s_c - m)
        l = jnp.sum(e_w, axis=-1, keepdims=True) + jnp.sum(e_c, axis=-1, keepdims=True)
        o = (jnp.dot(e_w.astype(BF16), vw, preferred_element_type=F32)
             + jnp.dot(e_c.astype(BF16), vc, preferred_element_type=F32)) / l
        o_ref[0, pl.ds(pl.multiple_of(r0 * GRID_W, nq), nq), :] = o.astype(o_ref.dtype)
        return carry

    lax.fori_loop(0, rows // NA_QROWS, body, 0, unroll=2)


def na_attention(p, cache_k, cache_v, layer_j, rel_bias, q_gain, k_gain):
    b, t, _ = p.shape
    past = cache_k.shape[3]
    rows = t // GRID_W
    hd = NA_HEAD_DIM
    col0 = FNET_WIDTH // hd
    blk = lambda c: pl.BlockSpec((1, t, hd), lambda h, i, rb, c=c: (i, 0, col0 + c * NA_HEADS + h))
    cache_spec = pl.BlockSpec((1, 1, 1, past, hd), lambda h, i, rb: (i, layer_j, h, 0, 0))
    gain = pl.BlockSpec((1, hd), lambda h, i, rb: (0, 0))
    return pl.pallas_call(
        functools.partial(_na_kernel, rows=rows),
        out_shape=jax.ShapeDtypeStruct((b, t, NA_WIDTH), BF16),
        grid_spec=pltpu.PrefetchScalarGridSpec(
            num_scalar_prefetch=1,
            grid=(NA_HEADS, b),
            in_specs=[blk(0), blk(1), blk(2), cache_spec, cache_spec, gain, gain],
            out_specs=pl.BlockSpec((1, t, hd), lambda h, i, rb: (i, 0, h)),
            scratch_shapes=[pltpu.VMEM((t, hd), BF16)] * 3
            + [pltpu.VMEM((2, NA_BIAS_ROWS + 1, GRID_W, 2 * GRID_W), F32)],
        ),
        compiler_params=_params(("arbitrary", "arbitrary"), 40 << 20),
        name="na_attention",
    )(rel_bias.astype(F32).reshape(-1), p, p, p, cache_k, cache_v, q_gain, k_gain)


FNET_BLOCK = 16


def _fnet_tables(n1, n2):
    t_len = n1 * n2
    tb = FNET_BLOCK
    eye = np.eye(tb)
    k1 = np.arange(n1)
    ang1 = (2.0 * np.pi / n1) * ((k1[:, None] * k1[None, :]) % n1)
    f1 = np.concatenate([np.cos(ang1), -np.sin(ang1)], axis=0) / math.sqrt(t_len)
    rows_mat = np.kron(f1, eye)
    t2 = np.arange(n2)
    tw = (2.0 * np.pi / t_len) * (k1[:, None] * t2[None, :])
    tw = tw.reshape(n1, n2 // tb, tb).transpose(1, 0, 2).reshape(n2 // tb, n1 * tb, 1)
    c = np.arange(FNET_GROUP_DIM)
    ang3 = (2.0 * np.pi / FNET_GROUP_DIM) * ((c[:, None] * c[None, :]) % FNET_GROUP_DIM)
    c3, s3 = np.cos(ang3), np.sin(ang3)
    chan = np.block([[c3, -s3], [s3, c3]]) / math.sqrt(FNET_GROUP_DIM)
    ang2 = (2.0 * np.pi / n2) * ((t2[:, None] * t2[None, :]) % n2)
    cols_re = np.einsum("pq,kt->kpqt", eye, np.cos(ang2)).reshape(n2 * tb, tb * n2)
    cols_im = np.einsum("pq,kt->kpqt", eye, np.sin(ang2)).reshape(n2 * tb, tb * n2)
    bf = lambda x: jnp.asarray(x, dtype=BF16)
    f32 = lambda x: jnp.asarray(x, dtype=F32)
    return bf(rows_mat), f32(np.cos(tw)), f32(np.sin(tw)), bf(chan), bf(cols_re), bf(cols_im)


def _fnet_rows_kernel(x_ref, mat_ref, twc_ref, tws_ref, chan_ref, vr_ref, vi_ref):
    _, n1, tb, width = x_ref.shape
    rows = n1 * tb
    gd = FNET_GROUP_DIM
    x = x_ref[0].reshape(rows, width).astype(BF16)
    h = jnp.dot(mat_ref[...], x, preferred_element_type=F32)
    hr, hi = h[:rows], h[rows:]
    c, s = twc_ref[0], tws_ref[0]
    gr = (hr * c + hi * s).astype(BF16)
    gi = (hi * c - hr * s).astype(BF16)
    stacked = jnp.concatenate(
        [jnp.concatenate([gr[:, g * gd:(g + 1) * gd], gi[:, g * gd:(g + 1) * gd]], axis=1)
         for g in range(FNET_GROUPS)], axis=0)
    v = jnp.dot(stacked, chan_ref[...], preferred_element_type=F32)
    for g in range(FNET_GROUPS):
        part = v[g * rows:(g + 1) * rows]
        vr_ref[0, :, :, g * gd:(g + 1) * gd] = part[:, :gd].reshape(n1, tb, gd).astype(vr_ref.dtype)
        vi_ref[0, :, :, g * gd:(g + 1) * gd] = part[:, gd:].reshape(n1, tb, gd).astype(vi_ref.dtype)


def _fnet_cols_kernel(vr_ref, vi_ref, re_ref, im_ref, o_ref):
    _, kb, n2, width = vr_ref.shape
    vr = vr_ref[0].reshape(kb * n2, width)
    vi = vi_ref[0].reshape(kb * n2, width)
    y = (jnp.dot(re_ref[...], vr, preferred_element_type=F32)
         + jnp.dot(im_ref[...], vi, preferred_element_type=F32))
    o_ref[0] = y.reshape(n2, kb, width).astype(o_ref.dtype)


def fourier_mix(p, n1, n2):
    b, t, width = p.shape
    tb = FNET_BLOCK
    w = FNET_WIDTH
    rows_mat, twc, tws, chan, cols_re, cols_im = _fnet_tables(n1, n2)
    const = lambda shape: pl.BlockSpec(shape, lambda i, s: (0,) * len(shape))
    tw_spec = pl.BlockSpec((1, n1 * tb, 1), lambda i, s: (s, 0, 0))
    v_shape = jax.ShapeDtypeStruct((b, n1, n2, w), BF16)
    v_spec = pl.BlockSpec((1, n1, tb, w), lambda i, s: (i, 0, s, 0))
    isz = p.dtype.itemsize
    vmem = (2 * n1 * tb * w * isz + 2 * rows_mat.size * 2 + 4 * n1 * tb * w * 2
            + 6 * n1 * tb * w * 4 + VMEM_SLACK)
    vr, vi = pl.pallas_call(
        _fnet_rows_kernel,
        out_shape=(v_shape, v_shape),
        grid=(b, n2 // tb),
        in_specs=[pl.BlockSpec((1, n1, tb, w), lambda i, s: (i, 0, s, 0)),
                  const(rows_mat.shape), tw_spec, tw_spec, const(chan.shape)],
        out_specs=(v_spec, v_spec),
        compiler_params=_params(("arbitrary", "arbitrary"), vmem),
        name="fnet_rows",
    )(p.reshape(b, n1, n2, width), rows_mat, twc, tws, chan)
    v_in = pl.BlockSpec((1, tb, n2, w), lambda i, s: (i, s, 0, 0))
    vmem = 2 * 2 * tb * n2 * w * 2 + 2 * 2 * cols_re.size * 2 + 2 * n2 * tb * w * 2 + 3 * n2 * tb * w * 4 + VMEM_SLACK
    out = pl.pallas_call(
        _fnet_cols_kernel,
        out_shape=jax.ShapeDtypeStruct((b, n2, n1, w), BF16),
        grid=(b, n1 // tb),
        in_specs=[v_in, v_in, const(cols_re.shape), const(cols_im.shape)],
        out_specs=pl.BlockSpec((1, n2, tb, w), lambda i, s: (i, 0, s, 0)),
        compiler_params=_params(("arbitrary", "arbitrary"), vmem),
        name="fnet_cols",
    )(vr, vi, cols_re, cols_im)
    return out.reshape(b, t, w)


def _even_out_kernel(f_ref, a_ref, wf_ref, wa_ref, x_ref, gate_ref, o_ref):
    y = (jnp.dot(f_ref[...], wf_ref[...], preferred_element_type=F32)
         + jnp.dot(a_ref[...], wa_ref[...], preferred_element_type=F32))
    o_ref[...] = x_ref[...] + gate_ref[0] * y


def even_out(f, a, w, x, gate, rows_per_mod, tm, tn):
    m, d = x.shape
    kf = f.shape[1]
    vmem = 2 * 2 * tm * kf * 2 + 2 * 2 * kf * tn * 2 + 4 * tm * tn * 4 + 2 * tm * tn * 4 + VMEM_SLACK
    return pl.pallas_call(
        _even_out_kernel,
        out_shape=jax.ShapeDtypeStruct((m, d), F32),
        grid=(d // tn, m // tm),
        in_specs=[
            pl.BlockSpec((tm, kf), lambda j, i: (i, 0)),
            pl.BlockSpec((tm, kf), lambda j, i: (i, 0)),
            pl.BlockSpec((kf, tn), lambda j, i: (0, j)),
            pl.BlockSpec((kf, tn), lambda j, i: (1, j)),
            pl.BlockSpec((tm, tn), lambda j, i: (i, j)),
            pl.BlockSpec((1, 1, tn), lambda j, i: ((i * tm) // rows_per_mod, 0, j)),
        ],
        out_specs=pl.BlockSpec((tm, tn), lambda j, i: (i, j)),
        compiler_params=_params(("arbitrary", "arbitrary"), vmem),
        name="even_out",
    )(f, a, w, w, x, gate)


def _gated_group_norm(o, g):
    o = o.astype(F32)
    oc = o - jnp.mean(o, axis=-1, keepdims=True)
    gn = oc * lax.rsqrt(jnp.mean(oc * oc, axis=-1, keepdims=True) + EPS)
    hg = 0.5 * g.astype(F32)
    return (hg + hg * jnp.tanh(hg)) * gn


def _odd_out_kernel(of_ref, ob_ref, gf_ref, gb_ref, w_ref, x_ref, gate_ref, o_ref):
    acc = None
    for h in range(RET_HEADS):
        cs = slice(h * RET_V_DIM, (h + 1) * RET_V_DIM)
        y = (_gated_group_norm(of_ref[:, cs], gf_ref[:, cs])
             + _gated_group_norm(ob_ref[:, cs], gb_ref[:, cs])).astype(BF16)
        part = jnp.dot(y, w_ref[cs, :], preferred_element_type=F32)
        acc = part if acc is None else acc + part
    o_ref[...] = x_ref[...] + gate_ref[0] * acc


def odd_out(o_f, o_b, p, w, x, gate, rows_per_mod, tm):
    m, d = x.shape
    k = w.shape[0]
    gcol = (2 * RET_QK_WIDTH + RET_V_WIDTH) // RET_V_WIDTH
    vmem = (2 * 4 * tm * k * 2 + k * d * 2 + 4 * tm * d * 4 + 2 * tm * d * 4
            + 6 * tm * RET_V_DIM * 4 + VMEM_SLACK)
    act = lambda c: pl.BlockSpec((tm, k), lambda i, c=c: (i, c))
    return pl.pallas_call(
        _odd_out_kernel,
        out_shape=jax.ShapeDtypeStruct((m, d), F32),
        grid=(m // tm,),
        in_specs=[
            act(0), act(0), act(gcol), act(gcol + 1),
            pl.BlockSpec((k, d), lambda i: (0, 0), pipeline_mode=pl.Buffered(1)),
            pl.BlockSpec((tm, d), lambda i: (i, 0)),
            pl.BlockSpec((1, 1, d), lambda i: ((i * tm) // rows_per_mod, 0, 0)),
        ],
        out_specs=pl.BlockSpec((tm, d), lambda i: (i, 0)),
        compiler_params=_params(("arbitrary",), vmem),
        name="odd_out",
    )(o_f, o_b, p, p, w, x, gate)


def _ffn_kernel(x_ref, sh_ref, sc_ref, gate_ref, g_ref, wg_ref, wu_ref, wd_ref, o_ref, h_ref):
    j = pl.program_id(1)

    @pl.when(j == 0)
    def _():
        _modulate_into(h_ref, x_ref, g_ref, sh_ref, sc_ref)
        o_ref[...] = jnp.zeros_like(o_ref)

    h = h_ref[...]
    a = jnp.dot(h, wg_ref[0], preferred_element_type=F32)
    u = jnp.dot(h, wu_ref[0], preferred_element_type=F32)
    act = (_silu(a) * u).astype(BF16)
    o_ref[...] += jnp.dot(act, wd_ref[0], preferred_element_type=F32)

    @pl.when(j == pl.num_programs(1) - 1)
    def _():
        o_ref[...] = x_ref[...] + gate_ref[0] * o_ref[...]


def ffn_block(x, shift, scale, gate, g, w_gate, w_up, w_down, layer, rows_per_mod, tm, tf):
    m, d = x.shape
    f = w_gate.shape[2]
    vmem = 4 * tm * d * 4 + tm * d * 2 + 2 * 3 * d * tf * 2 + 4 * tm * tf * 4 + tm * d * 4 + VMEM_SLACK
    mod_idx = lambda i, j: ((i * tm) // rows_per_mod, 0, 0)
    mod = pl.BlockSpec((1, 1, d), mod_idx)
    return pl.pallas_call(
        _ffn_kernel,
        out_shape=jax.ShapeDtypeStruct((m, d), F32),
        grid=(m // tm, f // tf),
        in_specs=[
            pl.BlockSpec((tm, d), lambda i, j: (i, 0)),
            mod, mod, mod,
            pl.BlockSpec((1, d), lambda i, j: (0, 0)),
            pl.BlockSpec((1, d, tf), lambda i, j: (layer, 0, j)),
            pl.BlockSpec((1, d, tf), lambda i, j: (layer, 0, j)),
            pl.BlockSpec((1, tf, d), lambda i, j: (layer, j, 0)),
        ],
        out_specs=pl.BlockSpec((tm, d), lambda i, j: (i, 0)),
        scratch_shapes=[pltpu.VMEM((tm, d), BF16)],
        compiler_params=_params(("arbitrary", "arbitrary"), vmem),
        name="ffn_block",
    )(x, shift, scale, gate, g, w_gate, w_up, w_down)


def _retention_kernel(lg_ref, *refs, latent, n_blocks):
    if latent:
        (qf_ref, kf_ref, vf_ref, qb_ref, kb_ref, vb_ref, s0_ref,
         of_ref, ob_ref, s_ref, dec_ref, qd_ref, kd_ref) = refs
    else:
        (qf_ref, kf_ref, vf_ref, qb_ref, kb_ref, vb_ref,
         of_ref, ob_ref, sfin_ref, s_ref, dec_ref, qd_ref, kd_ref) = refs
    h0 = pl.program_id(0) * RET_HEADS_PER_STEP
    i = pl.program_id(2)
    c = RET_BLOCK
    dk, dv = RET_QK_DIM, RET_V_DIM
    scale = RET_QK_DIM ** -0.5
    scans = [(d, hh) for d in range(2) for hh in range(RET_HEADS_PER_STEP)]

    @pl.when((pl.program_id(1) == 0) & (i == 0))
    def _():
        row = lax.broadcasted_iota(jnp.int32, (c, c), 0).astype(F32)
        col = lax.broadcasted_iota(jnp.int32, (c, c), 1).astype(F32)
        t_idx = lax.broadcasted_iota(jnp.int32, (c, 1), 0).astype(F32)
        for d, hh in scans:
            lg = lg_ref[d, h0 + hh]
            diff = row - col if d == 0 else col - row
            dec_ref[d, hh] = jnp.where(diff >= 0, jnp.exp(lg * jnp.maximum(diff, 0.0)), 0.0) * scale
            pos = t_idx if d == 0 else (c - 1.0) - t_idx
            qd_ref[d, hh] = jnp.exp(lg * (pos + 1.0))
            kd_ref[d, hh] = jnp.exp(lg * ((c - 1.0) - pos)) * scale

    @pl.when(i == 0)
    def _():
        for d, hh in scans:
            s_ref[d, hh] = (_pair_order_rows(s0_ref[0, 0, d, hh]) if latent
                            else jnp.zeros((dk, dv), F32))

    nt = (((1,), (1,)), ((), ()))
    tn = (((0,), (0,)), ((), ()))
    chunk_refs = ((qf_ref, kf_ref, vf_ref, of_ref), (qb_ref, kb_ref, vb_ref, ob_ref))
    for d, hh in scans:
        q_ref, k_ref, v_ref, o_ref = chunk_refs[d]
        q = q_ref[0, :, hh * dk:(hh + 1) * dk]
        k = k_ref[0, :, hh * dk:(hh + 1) * dk]
        v = v_ref[0, :, hh * dv:(hh + 1) * dv]
        a = lax.dot_general(q, k, nt, preferred_element_type=F32)
        inner_w = (a * dec_ref[d, hh]).astype(BF16)
        q_dec = (q.astype(F32) * qd_ref[d, hh]).astype(BF16)
        k_dec = (k.astype(F32) * kd_ref[d, hh]).astype(BF16)
        s_old = s_ref[d, hh]
        o = (jnp.dot(inner_w, v, preferred_element_type=F32)
             + jnp.dot(q_dec, s_old.astype(BF16), preferred_element_type=F32))
        s_new = (s_old * jnp.exp(lg_ref[d, h0 + hh] * c)
                 + lax.dot_general(k_dec, v, tn, preferred_element_type=F32))
        s_ref[d, hh] = s_new
        o_ref[0, :, hh * dv:(hh + 1) * dv] = o.astype(o_ref.dtype)
        if not latent:
            @pl.when(i == n_blocks - 1)
            def _(d=d, hh=hh, s_new=s_new):
                sfin_ref[0, 0, d, hh] = _pair_order_rows(s_new)


def retention(p, log_gamma, state0, layer_j):
    b, t, _ = p.shape
    latent = state0 is not None
    c = RET_BLOCK
    nb = t // c
    hps = RET_HEADS_PER_STEP
    dk, dv, nh = hps * RET_QK_DIM, hps * RET_V_DIM, RET_HEADS // hps

    def chunk_specs(blk_of):
        return [
            pl.BlockSpec((1, c, dk), lambda h, bi, i, lg: (bi, blk_of(i), h)),
            pl.BlockSpec((1, c, dk), lambda h, bi, i, lg: (bi, blk_of(i), nh + h)),
            pl.BlockSpec((1, c, dv), lambda h, bi, i, lg: (bi, blk_of(i), nh + h)),
        ]

    fwd = lambda i: i
    bwd = lambda i: nb - 1 - i
    in_specs = chunk_specs(fwd) + chunk_specs(bwd)
    args = [p] * 6
    o_shape = jax.ShapeDtypeStruct((b, t, RET_V_WIDTH), BF16)
    out_shape = [o_shape, o_shape]
    out_specs = [pl.BlockSpec((1, c, dv), lambda h, bi, i, lg: (bi, fwd(i), h)),
                 pl.BlockSpec((1, c, dv), lambda h, bi, i, lg: (bi, bwd(i), h))]
    state_dims = (RET_QK_DIM, RET_V_DIM)
    state_block = (1, 1, 2, hps) + state_dims
    if latent:
        in_specs.append(pl.BlockSpec(state_block, lambda h, bi, i, lg: (bi, layer_j, 0, h, 0, 0)))
        args.append(state0)
    else:
        out_shape.append(jax.ShapeDtypeStruct((b, 1, 2, RET_HEADS) + state_dims, F32))
        out_specs.append(pl.BlockSpec(state_block, lambda h, bi, i, lg: (bi, 0, 0, h, 0, 0)))
    return pl.pallas_call(
        functools.partial(_retention_kernel, latent=latent, n_blocks=nb),
        out_shape=tuple(out_shape),
        grid_spec=pltpu.PrefetchScalarGridSpec(
            num_scalar_prefetch=1,
            grid=(nh, b, nb),
            in_specs=in_specs,
            out_specs=tuple(out_specs),
            scratch_shapes=[pltpu.VMEM((2, hps) + state_dims, F32), pltpu.VMEM((2, hps, c, c), F32),
                            pltpu.VMEM((2, hps, c, 1), F32), pltpu.VMEM((2, hps, c, 1), F32)],
        ),
        compiler_params=_params(("arbitrary",) * 3, 40 << 20),
        name="retention",
    )(log_gamma, *args)


def _trunk(x, mods, ctx_k, ctx_v, ctx_state, wts, p_dtype):
    b, t, d = x.shape
    is_ctx = ctx_k is None
    nbm = mods.shape[1]
    rows_per_mod = (b * t) // nbm
    x2 = x.reshape(b * t, d)
    new_k = new_v = new_s = None
    for i in range(DEPTH):
        mod = [mods[i, :, k][:, None, :] for k in range(6)]
        j = i // 2
        if i % 2 == 0:
            p = modproj(x2, mod[0], mod[1], wts["norm_g"][i, 0][None], wts["even_w_in"][j],
                        rows_per_mod, p_dtype, tm=1024, tn=1024).reshape(b, t, EVEN_IN)
            qg, kg = wts["even_q_norm"][j][None], wts["even_k_norm"][j][None]
            if is_ctx:
                attn, new_k, new_v = ctx_attention(p, qg, kg)
            else:
                attn = na_attention(p, ctx_k, ctx_v, j, wts["na_rel_bias"][j], qg, kg)
            n2 = GRID_W if not is_ctx else math.isqrt(t)
            fm = fourier_mix(p, t // n2, n2)
            x2 = even_out(fm.reshape(b * t, FNET_WIDTH), attn.reshape(b * t, NA_WIDTH),
                          wts["even_w_out"][j], x2, mod[2], rows_per_mod, tm=1024, tn=1024)
        else:
            rope = (t, 2 * RET_QK_WIDTH) if not is_ctx else None
            p = modproj(x2, mod[0], mod[1], wts["norm_g"][i, 0][None], wts["odd_w_in"][j],
                        rows_per_mod, BF16, tm=1024, tn=1024, rope=rope)
            lg = jax.nn.log_sigmoid(wts["ret_decay_logit"][j].astype(F32))
            if is_ctx:
                o_f, o_b, new_s = retention(p.reshape(b, t, ODD_IN), lg, None, j)
            else:
                o_f, o_b = retention(p.reshape(b, t, ODD_IN), lg, ctx_state, j)
            x2 = odd_out(o_f.reshape(b * t, RET_V_WIDTH), o_b.reshape(b * t, RET_V_WIDTH), p,
                         wts["odd_w_out"][j], x2, mod[2], rows_per_mod, tm=256)
        x2 = ffn_block(x2, mod[3], mod[4], mod[5], wts["norm_g"][i, 1][None], wts["ffn_w_gate"],
                       wts["ffn_w_up"], wts["ffn_w_down"], i, rows_per_mod, tm=512, tf=512)
    return x2.reshape(b, t, d), new_k, new_v, new_s


def kernel(x_prompt, x_sample, cache_k, cache_v, state_ret, c, c_ctx, ada_w, ada_b, norm_g,
           even_w_in, even_q_norm, even_k_norm, na_rel_bias, even_w_out, odd_w_in, ret_decay_logit,
           odd_w_out, ffn_w_gate, ffn_w_up, ffn_w_down):
    nb_lat = c.shape[0]
    cond = jnp.concatenate(
        [c, c_ctx[None, :], jnp.zeros((MOD_ROWS - nb_lat - 1, D_MODEL), F32)], axis=0)
    mods = adaln_all(cond, ada_w, ada_b).reshape(DEPTH, MOD_ROWS, 6, D_MODEL)
    wts = dict(
        norm_g=norm_g, even_q_norm=even_q_norm, even_k_norm=even_k_norm, na_rel_bias=na_rel_bias,
        ret_decay_logit=ret_decay_logit,
        even_w_in=even_w_in.astype(BF16), even_w_out=even_w_out.astype(BF16),
        odd_w_in=_pair_order_columns(odd_w_in.astype(BF16)), odd_w_out=odd_w_out.astype(BF16),
        ffn_w_gate=ffn_w_gate.astype(BF16), ffn_w_up=ffn_w_up.astype(BF16),
        ffn_w_down=ffn_w_down.astype(BF16),
    )
    y_prompt, new_k, new_v, new_s = _trunk(
        x_prompt, mods[:, nb_lat:nb_lat + 1], None, None, None, wts, F32)
    y_sample, _, _, _ = _trunk(x_sample, mods[:, :nb_lat], cache_k, cache_v, state_ret, wts, BF16)
    return (y_prompt, y_sample, new_k, new_v, new_s)
```

```python
import functools
import math

import numpy as np
import jax
import jax.numpy as jnp
from jax import lax
from jax.experimental import pallas as pl
from jax.experimental.pallas import tpu as pltpu

F32 = jnp.float32
BF16 = jnp.bfloat16

D_MODEL = 2048
DEPTH = 2
GRID_W = 64
EPS = 1e-6
NEG_INF = -1e30
FNET_GROUPS = 8
FNET_GROUP_DIM = 128
FNET_WIDTH = 1024
NA_HEADS = 8
NA_HEAD_DIM = 128
NA_WIDTH = 1024
NA_KH = 8
NA_KW = 16
EVEN_IN = FNET_WIDTH + 3 * NA_WIDTH
RET_HEADS = 8
RET_QK_DIM = 256
RET_V_DIM = 512
RET_QK_WIDTH = RET_HEADS * RET_QK_DIM
RET_V_WIDTH = RET_HEADS * RET_V_DIM
ODD_IN = 2 * RET_QK_WIDTH + 3 * RET_V_WIDTH
RET_BLOCK = 256
RET_HEADS_PER_STEP = 4
ROPE_BASE = 10000.0
D_FF = 5632
MOD_ROWS = 8
MOD_CHUNK = 128

V7X_VMEM_BUDGET = 56 * 1024 * 1024
VMEM_SLACK = 8 * 1024 * 1024


def _params(semantics, vmem_bytes):
    return pltpu.CompilerParams(dimension_semantics=semantics,
                                vmem_limit_bytes=min(int(vmem_bytes), V7X_VMEM_BUDGET))


def _silu(x):
    return x * (1.0 / (1.0 + jnp.exp(-x)))


def _adaln_kernel(c_ref, w_ref, b_ref, o_ref):
    s = _silu(c_ref[...]).astype(BF16)
    w = w_ref[0].astype(BF16)
    o_ref[0] = jnp.dot(s, w, preferred_element_type=F32) + b_ref[0]


def adaln_all(cond, ada_w, ada_b):
    n = ada_w.shape[-1]
    tn = 1024
    return pl.pallas_call(
        _adaln_kernel,
        out_shape=jax.ShapeDtypeStruct((DEPTH, MOD_ROWS, n), F32),
        grid=(DEPTH, n // tn),
        in_specs=[
            pl.BlockSpec((MOD_ROWS, D_MODEL), lambda l, j: (0, 0)),
            pl.BlockSpec((1, D_MODEL, tn), lambda l, j: (l, 0, j)),
            pl.BlockSpec((1, 1, tn), lambda l, j: (l, 0, j)),
        ],
        out_specs=pl.BlockSpec((1, MOD_ROWS, tn), lambda l, j: (l, 0, j)),
        compiler_params=_params(("arbitrary", "arbitrary"), 40 << 20),
        name="adaln",
    )(cond, ada_w, ada_b.reshape(DEPTH, 1, n))


def _modulate_into(h_ref, x_ref, g_ref, sh_ref, sc_ref):
    shift = sh_ref[0]
    gain = g_ref[...] * (1.0 + sc_ref[0])

    def body(c, carry):
        rows = pl.ds(pl.multiple_of(c * MOD_CHUNK, MOD_CHUNK), MOD_CHUNK)
        x = x_ref[rows, :]
        ms = jnp.mean(x * x, axis=-1, keepdims=True)
        h_ref[rows, :] = (x * lax.rsqrt(ms + EPS) * gain + shift).astype(h_ref.dtype)
        return carry

    lax.fori_loop(0, x_ref.shape[0] // MOD_CHUNK, body, 0)


ROPE_QUARTER = RET_QK_DIM // 4


def _pair_order(x, axis):
    q = ROPE_QUARTER
    parts = [lax.slice_in_dim(x, a * q, (a + 1) * q, axis=axis) for a in (0, 2, 1, 3)]
    return jnp.concatenate(parts, axis=axis)


def _pair_order_rows(s):
    return _pair_order(s, 0)


def _odd_w_prep_kernel(w_ref, o_ref):
    for head in range(2 * RET_HEADS):
        cs = slice(head * RET_QK_DIM, (head + 1) * RET_QK_DIM)
        o_ref[0, :, cs] = _pair_order(w_ref[0, :, cs], 1).astype(o_ref.dtype)
    rest = 2 * RET_QK_WIDTH
    o_ref[0, :, rest:] = w_ref[0, :, rest:].astype(o_ref.dtype)


def odd_w_prep(w):
    n, d, width = w.shape
    tr = 128
    return pl.pallas_call(
        _odd_w_prep_kernel,
        out_shape=jax.ShapeDtypeStruct(w.shape, BF16),
        grid=(n, d // tr),
        in_specs=[pl.BlockSpec((1, tr, width), lambda l, i: (l, i, 0))],
        out_specs=pl.BlockSpec((1, tr, width), lambda l, i: (l, i, 0)),
        compiler_params=_params(("arbitrary", "arbitrary"), 2 * tr * width * 6 + 3 * tr * width * 4 + VMEM_SLACK),
        name="odd_w_prep",
    )(w)


def _rope_tables(t_len):
    inv = ROPE_BASE ** (-jnp.arange(ROPE_QUARTER, dtype=F32) / ROPE_QUARTER)
    t = jnp.arange(t_len)
    ang = jnp.concatenate([(t // GRID_W).astype(F32)[:, None] * inv[None, :],
                           (t % GRID_W).astype(F32)[:, None] * inv[None, :]], axis=-1)
    return jnp.cos(ang), jnp.sin(ang)


def _rope(x, cos, sin):
    half = RET_QK_DIM // 2
    x1, x2 = x[:, :half], x[:, half:]
    return jnp.concatenate([x1 * cos - x2 * sin, x1 * sin + x2 * cos], axis=-1)


def _modproj_kernel(*refs, rope_tiles):
    if rope_tiles:
        x_ref, sh_ref, sc_ref, g_ref, w_ref, cos_ref, sin_ref, o_ref, h_ref = refs
    else:
        x_ref, sh_ref, sc_ref, g_ref, w_ref, o_ref, h_ref = refs
    j = pl.program_id(1)

    @pl.when(j == 0)
    def _():
        _modulate_into(h_ref, x_ref, g_ref, sh_ref, sc_ref)

    acc = jnp.dot(h_ref[...], w_ref[...], preferred_element_type=F32)
    if not rope_tiles:
        o_ref[...] = acc.astype(o_ref.dtype)
        return

    @pl.when(j < rope_tiles)
    def _():
        cos = cos_ref[...]
        sin = sin_ref[...]
        for s in range(acc.shape[1] // RET_QK_DIM):
            cs = slice(s * RET_QK_DIM, (s + 1) * RET_QK_DIM)
            o_ref[:, cs] = _rope(acc[:, cs], cos, sin).astype(o_ref.dtype)

    @pl.when(j >= rope_tiles)
    def _():
        o_ref[...] = acc.astype(o_ref.dtype)


def modproj(x, shift, scale, g, w, rows_per_mod, out_dtype, tm, tn, rope=None):
    m, d = x.shape
    n = w.shape[1]
    osz = jnp.dtype(out_dtype).itemsize
    vmem = 2 * tm * d * 4 + 2 * d * tn * 2 + 2 * tm * tn * osz + tm * d * 2 + 2 * tm * tn * 4 + VMEM_SLACK
    mod_idx = lambda i, j: ((i * tm) // rows_per_mod, 0, 0)
    in_specs = [
        pl.BlockSpec((tm, d), lambda i, j: (i, 0)),
        pl.BlockSpec((1, 1, d), mod_idx),
        pl.BlockSpec((1, 1, d), mod_idx),
        pl.BlockSpec((1, d), lambda i, j: (0, 0)),
        pl.BlockSpec((d, tn), lambda i, j: (0, j)),
    ]
    args = [x, shift, scale, g, w]
    rope_tiles = 0
    if rope is not None:
        seq_len, n_cols = rope
        rope_tiles = n_cols // tn
        cos, sin = _rope_tables(seq_len)
        tab = pl.BlockSpec((tm, RET_QK_DIM // 2), lambda i, j: (i % (seq_len // tm), 0))
        in_specs += [tab, tab]
        args += [cos, sin]
        vmem += 2 * 2 * tm * (RET_QK_DIM // 2) * 4
    return pl.pallas_call(
        functools.partial(_modproj_kernel, rope_tiles=rope_tiles),
        out_shape=jax.ShapeDtypeStruct((m, n), out_dtype),
        grid=(m // tm, n // tn),
        in_specs=in_specs,
        out_specs=pl.BlockSpec((tm, tn), lambda i, j: (i, j)),
        scratch_shapes=[pltpu.VMEM((tm, d), BF16)],
        compiler_params=_params(("arbitrary", "arbitrary"), vmem),
        name="modproj",
    )(*args)


def _head_rms(x, g):
    return x * lax.rsqrt(jnp.mean(x * x, axis=-1, keepdims=True) + EPS) * g


def _ctx_attn_kernel(q_ref, k_ref, v_ref, qg_ref, kg_ref, o_ref, nk_ref, nv_ref):
    scale = NA_HEAD_DIM ** -0.5
    for h in range(NA_HEADS):
        cs = slice(h * NA_HEAD_DIM, (h + 1) * NA_HEAD_DIM)
        q = _head_rms(q_ref[0, :, cs].astype(F32), qg_ref[...])
        k = _head_rms(k_ref[0, :, cs].astype(F32), kg_ref[...])
        v = v_ref[0, :, cs].astype(F32)
        nk_ref[0, 0, h] = k
        nv_ref[0, 0, h] = v
        s = lax.dot_general(q.astype(BF16), k.astype(BF16), (((1,), (1,)), ((), ())),
                            preferred_element_type=F32) * scale
        m = jnp.max(s, axis=-1, keepdims=True)
        e = jnp.exp(s - m)
        l = jnp.sum(e, axis=-1, keepdims=True)
        o = jnp.dot(e.astype(BF16), v.astype(BF16), preferred_element_type=F32) / l
        o_ref[0, :, cs] = o.astype(o_ref.dtype)


def ctx_attention(p, q_gain, k_gain):
    b, t, _ = p.shape
    cache_shape = (b, 1, NA_HEADS, t, NA_HEAD_DIM)
    blk = lambda c: pl.BlockSpec((1, t, NA_WIDTH), lambda i, c=c: (i, 0, c))
    gain = pl.BlockSpec((1, NA_HEAD_DIM), lambda i: (0, 0))
    cache_spec = pl.BlockSpec((1, 1, NA_HEADS, t, NA_HEAD_DIM), lambda i: (i, 0, 0, 0, 0))
    return pl.pallas_call(
        _ctx_attn_kernel,
        out_shape=(jax.ShapeDtypeStruct((b, t, NA_WIDTH), BF16),
                   jax.ShapeDtypeStruct(cache_shape, F32),
                   jax.ShapeDtypeStruct(cache_shape, F32)),
        grid=(b,),
        in_specs=[blk(1), blk(2), blk(3), gain, gain],
        out_specs=(pl.BlockSpec((1, t, NA_WIDTH), lambda i: (i, 0, 0)), cache_spec, cache_spec),
        compiler_params=_params(("arbitrary",), 32 << 20),
        name="ctx_attention",
    )(p, p, p, q_gain, k_gain)


NA_BIAS_ROWS = 2 * NA_KH - 1
NA_BIAS_COLS = 2 * NA_KW - 1
NA_MASK_TILE = NA_BIAS_ROWS
NA_QROWS = 4
NA_KROWS = NA_KH + NA_QROWS


def _na_build_bias(rb_ref, bias_ref, head):
    shape = (GRID_W, 2 * GRID_W)
    lane = lax.broadcasted_iota(jnp.int32, shape, 1)
    qc = lax.broadcasted_iota(jnp.int32, shape, 0)
    kc = lane & (GRID_W - 1)
    start = jnp.clip(qc - NA_KW // 2, 0, GRID_W - NA_KW)
    ok = (kc >= start) & (kc < start + NA_KW)
    delta = kc - qc + (NA_KW - 1)
    left = lane < GRID_W
    base = head * (NA_BIAS_ROWS * NA_BIAS_COLS)

    def build(dr, carry):
        acc = jnp.full(shape, NEG_INF, F32)
        for dc in range(NA_BIAS_COLS):
            acc = jnp.where(delta == dc, rb_ref[base + dr * NA_BIAS_COLS + dc], acc)
        tile = jnp.where(ok, acc, NEG_INF)
        bias_ref[0, dr] = jnp.where(left, tile, 0.0)
        bias_ref[1, dr] = jnp.where(left, 0.0, tile)
        return carry

    lax.fori_loop(0, NA_BIAS_ROWS, build, 0)
    bias_ref[0, NA_MASK_TILE] = jnp.where(left, NEG_INF, 0.0)
    bias_ref[1, NA_MASK_TILE] = jnp.where(left, 0.0, NEG_INF)


def _na_kernel(rb_ref, q_ref, k_ref, v_ref, kc_ref, vc_ref, qg_ref, kg_ref, o_ref,
               qs_ref, ks_ref, vs_ref, bias_ref, *, rows):
    @pl.when(pl.program_id(1) == 0)
    def _():
        _na_build_bias(rb_ref, bias_ref, pl.program_id(0))

    scale = NA_HEAD_DIM ** -0.5
    qs_ref[...] = (_head_rms(q_ref[0].astype(F32), qg_ref[...]) * scale).astype(BF16)
    ks_ref[...] = _head_rms(k_ref[0].astype(F32), kg_ref[...]).astype(BF16)
    vs_ref[...] = v_ref[0].astype(BF16)
    kc = kc_ref[0, 0, 0].astype(BF16)
    vc = vc_ref[0, 0, 0].astype(BF16)
    nt = (((1,), (1,)), ((), ()))
    nq = NA_QROWS * GRID_W
    nk = NA_KROWS * GRID_W

    def tile_index(r, rs, kr):
        inside = (kr >= rs) & (kr < rs + NA_KH)
        return jnp.where(inside, kr - r + (NA_KH - 1), NA_MASK_TILE)

    def body(blk, carry):
        r0 = blk * NA_QROWS
        k0 = jnp.clip(r0 - NA_KH // 2, 0, rows - NA_KROWS)
        q = qs_ref[pl.ds(pl.multiple_of(r0 * GRID_W, nq), nq), :]
        kw = ks_ref[pl.ds(pl.multiple_of(k0 * GRID_W, GRID_W), nk), :]
        vw = vs_ref[pl.ds(pl.multiple_of(k0 * GRID_W, GRID_W), nk), :]
        bias_rows = []
        for i in range(NA_QROWS):
            r = r0 + i
            rs = jnp.clip(r - NA_KH // 2, 0, rows - NA_KH)
            pairs = [bias_ref[0, tile_index(r, rs, k0 + 2 * jp)]
                     + bias_ref[1, tile_index(r, rs, k0 + 2 * jp + 1)]
                     for jp in range(NA_KROWS // 2)]
            bias_rows.append(jnp.concatenate(pairs, axis=-1))
        bias = jnp.concatenate(bias_rows, axis=0)
        s_w = lax.dot_general(q, kw, nt, preferred_element_type=F32) + bias
        s_c = lax.dot_general(q, kc, nt, preferred_element_type=F32)
        m = jnp.maximum(jnp.max(s_w, axis=-1, keepdims=True), jnp.max(s_c, axis=-1, keepdims=True))
        e_w = jnp.exp(s_w - m)
        e_c = jnp.exp(s_c - m)
        l = jnp.sum(e_w, axis=-1, keepdims=True) + jnp.sum(e_c, axis=-1, keepdims=True)
        o = (jnp.dot(e_w.astype(BF16), vw, preferred_element_type=F32)
             + jnp.dot(e_c.astype(BF16), vc, preferred_element_type=F32)) / l
        o_ref[0, pl.ds(pl.multiple_of(r0 * GRID_W, nq), nq), :] = o.astype(o_ref.dtype)
        return carry

    lax.fori_loop(0, rows // NA_QROWS, body, 0, unroll=2)


def na_attention(p, cache_k, cache_v, layer_j, rel_bias, q_gain, k_gain):
    b, t, _ = p.shape
    past = cache_k.shape[3]
    rows = t // GRID_W
    hd = NA_HEAD_DIM
    col0 = FNET_WIDTH // hd
    blk = lambda c: pl.BlockSpec((1, t, hd), lambda h, i, rb, c=c: (i, 0, col0 + c * NA_HEADS + h))
    cache_spec = pl.BlockSpec((1, 1, 1, past, hd), lambda h, i, rb: (i, layer_j, h, 0, 0))
    gain = pl.BlockSpec((1, hd), lambda h, i, rb: (0, 0))
    return pl.pallas_call(
        functools.partial(_na_kernel, rows=rows),
        out_shape=jax.ShapeDtypeStruct((b, t, NA_WIDTH), BF16),
        grid_spec=pltpu.PrefetchScalarGridSpec(
            num_scalar_prefetch=1,
            grid=(NA_HEADS, b),
            in_specs=[blk(0), blk(1), blk(2), cache_spec, cache_spec, gain, gain],
            out_specs=pl.BlockSpec((1, t, hd), lambda h, i, rb: (i, 0, h)),
            scratch_shapes=[pltpu.VMEM((t, hd), BF16)] * 3
            + [pltpu.VMEM((2, NA_BIAS_ROWS + 1, GRID_W, 2 * GRID_W), F32)],
        ),
        compiler_params=_params(("arbitrary", "arbitrary"), 40 << 20),
        name="na_attention",
    )(rel_bias.astype(F32).reshape(-1), p, p, p, cache_k, cache_v, q_gain, k_gain)


FNET_BLOCK = 16


def _fnet_tables(n1, n2):
    t_len = n1 * n2
    tb = FNET_BLOCK
    eye = np.eye(tb)
    k1 = np.arange(n1)
    ang1 = (2.0 * np.pi / n1) * ((k1[:, None] * k1[None, :]) % n1)
    f1 = np.concatenate([np.cos(ang1), -np.sin(ang1)], axis=0) / math.sqrt(t_len)
    rows_mat = np.kron(f1, eye)
    t2 = np.arange(n2)
    tw = (2.0 * np.pi / t_len) * (k1[:, None] * t2[None, :])
    tw = tw.reshape(n1, n2 // tb, tb).transpose(1, 0, 2).reshape(n2 // tb, n1 * tb, 1)
    c = np.arange(FNET_GROUP_DIM)
    ang3 = (2.0 * np.pi / FNET_GROUP_DIM) * ((c[:, None] * c[None, :]) % FNET_GROUP_DIM)
    c3, s3 = np.cos(ang3), np.sin(ang3)
    chan = np.block([[c3, -s3], [s3, c3]]) / math.sqrt(FNET_GROUP_DIM)
    ang2 = (2.0 * np.pi / n2) * ((t2[:, None] * t2[None, :]) % n2)
    cols_re = np.einsum("pq,kt->kpqt", eye, np.cos(ang2)).reshape(n2 * tb, tb * n2)
    cols_im = np.einsum("pq,kt->kpqt", eye, np.sin(ang2)).reshape(n2 * tb, tb * n2)
    bf = lambda x: jnp.asarray(x, dtype=BF16)
    f32 = lambda x: jnp.asarray(x, dtype=F32)
    return bf(rows_mat), f32(np.cos(tw)), f32(np.sin(tw)), bf(chan), bf(cols_re), bf(cols_im)


def _fnet_rows_kernel(x_ref, mat_ref, twc_ref, tws_ref, chan_ref, vr_ref, vi_ref):
    _, n1, tb, width = x_ref.shape
    rows = n1 * tb
    gd = FNET_GROUP_DIM
    x = x_ref[0].reshape(rows, width).astype(BF16)
    h = jnp.dot(mat_ref[...], x, preferred_element_type=F32)
    hr, hi = h[:rows], h[rows:]
    c, s = twc_ref[0], tws_ref[0]
    gr = (hr * c + hi * s).astype(BF16)
    gi = (hi * c - hr * s).astype(BF16)
    stacked = jnp.concatenate(
        [jnp.concatenate([gr[:, g * gd:(g + 1) * gd], gi[:, g * gd:(g + 1) * gd]], axis=1)
         for g in range(FNET_GROUPS)], axis=0)
    v = jnp.dot(stacked, chan_ref[...], preferred_element_type=F32)
    for g in range(FNET_GROUPS):
        part = v[g * rows:(g + 1) * rows]
        vr_ref[0, :, :, g * gd:(g + 1) * gd] = part[:, :gd].reshape(n1, tb, gd).astype(vr_ref.dtype)
        vi_ref[0, :, :, g * gd:(g + 1) * gd] = part[:, gd:].reshape(n1, tb, gd).astype(vi_ref.dtype)


def _fnet_cols_kernel(vr_ref, vi_ref, re_ref, im_ref, o_ref):
    _, kb, n2, width = vr_ref.shape
    vr = vr_ref[0].reshape(kb * n2, width)
    vi = vi_ref[0].reshape(kb * n2, width)
    y = (jnp.dot(re_ref[...], vr, preferred_element_type=F32)
         + jnp.dot(im_ref[...], vi, preferred_element_type=F32))
    o_ref[0] = y.reshape(n2, kb, width).astype(o_ref.dtype)


def fourier_mix(p, n1, n2):
    b, t, width = p.shape
    tb = FNET_BLOCK
    w = FNET_WIDTH
    rows_mat, twc, tws, chan, cols_re, cols_im = _fnet_tables(n1, n2)
    const = lambda shape: pl.BlockSpec(shape, lambda i, s: (0,) * len(shape))
    tw_spec = pl.BlockSpec((1, n1 * tb, 1), lambda i, s: (s, 0, 0))
    v_shape = jax.ShapeDtypeStruct((b, n1, n2, w), BF16)
    v_spec = pl.BlockSpec((1, n1, tb, w), lambda i, s: (i, 0, s, 0))
    isz = p.dtype.itemsize
    vmem = (2 * n1 * tb * w * isz + 2 * rows_mat.size * 2 + 4 * n1 * tb * w * 2
            + 6 * n1 * tb * w * 4 + VMEM_SLACK)
    vr, vi = pl.pallas_call(
        _fnet_rows_kernel,
        out_shape=(v_shape, v_shape),
        grid=(b, n2 // tb),
        in_specs=[pl.BlockSpec((1, n1, tb, w), lambda i, s: (i, 0, s, 0)),
                  const(rows_mat.shape), tw_spec, tw_spec, const(chan.shape)],
        out_specs=(v_spec, v_spec),
        compiler_params=_params(("arbitrary", "arbitrary"), vmem),
        name="fnet_rows",
    )(p.reshape(b, n1, n2, width), rows_mat, twc, tws, chan)
    v_in = pl.BlockSpec((1, tb, n2, w), lambda i, s: (i, s, 0, 0))
    vmem = 2 * 2 * tb * n2 * w * 2 + 2 * 2 * cols_re.size * 2 + 2 * n2 * tb * w * 2 + 3 * n2 * tb * w * 4 + VMEM_SLACK
    out = pl.pallas_call(
        _fnet_cols_kernel,
        out_shape=jax.ShapeDtypeStruct((b, n2, n1, w), BF16),
        grid=(b, n1 // tb),
        in_specs=[v_in, v_in, const(cols_re.shape), const(cols_im.shape)],
        out_specs=pl.BlockSpec((1, n2, tb, w), lambda i, s: (i, 0, s, 0)),
        compiler_params=_params(("arbitrary", "arbitrary"), vmem),
        name="fnet_cols",
    )(vr, vi, cols_re, cols_im)
    return out.reshape(b, t, w)


def _even_out_kernel(f_ref, a_ref, wf_ref, wa_ref, x_ref, gate_ref, o_ref):
    y = (jnp.dot(f_ref[...], wf_ref[...], preferred_element_type=F32)
         + jnp.dot(a_ref[...], wa_ref[...], preferred_element_type=F32))
    o_ref[...] = x_ref[...] + gate_ref[0] * y


def even_out(f, a, w, x, gate, rows_per_mod, tm, tn):
    m, d = x.shape
    kf = f.shape[1]
    vmem = 2 * 2 * tm * kf * 2 + 2 * 2 * kf * tn * 2 + 4 * tm * tn * 4 + 2 * tm * tn * 4 + VMEM_SLACK
    return pl.pallas_call(
        _even_out_kernel,
        out_shape=jax.ShapeDtypeStruct((m, d), F32),
        grid=(d // tn, m // tm),
        in_specs=[
            pl.BlockSpec((tm, kf), lambda j, i: (i, 0)),
            pl.BlockSpec((tm, kf), lambda j, i: (i, 0)),
            pl.BlockSpec((kf, tn), lambda j, i: (0, j)),
            pl.BlockSpec((kf, tn), lambda j, i: (1, j)),
            pl.BlockSpec((tm, tn), lambda j, i: (i, j)),
            pl.BlockSpec((1, 1, tn), lambda j, i: ((i * tm) // rows_per_mod, 0, j)),
        ],
        out_specs=pl.BlockSpec((tm, tn), lambda j, i: (i, j)),
        compiler_params=_params(("arbitrary", "arbitrary"), vmem),
        name="even_out",
    )(f, a, w, w, x, gate)


def _gated_group_norm(o, g):
    o = o.astype(F32)
    oc = o - jnp.mean(o, axis=-1, keepdims=True)
    gn = oc * lax.rsqrt(jnp.mean(oc * oc, axis=-1, keepdims=True) + EPS)
    hg = 0.5 * g.astype(F32)
    return (hg + hg * jnp.tanh(hg)) * gn


def _odd_out_kernel(of_ref, ob_ref, gf_ref, gb_ref, w_ref, x_ref, gate_ref, o_ref):
    acc = None
    for h in range(RET_HEADS):
        cs = slice(h * RET_V_DIM, (h + 1) * RET_V_DIM)
        y = (_gated_group_norm(of_ref[:, cs], gf_ref[:, cs])
             + _gated_group_norm(ob_ref[:, cs], gb_ref[:, cs])).astype(BF16)
        part = jnp.dot(y, w_ref[cs, :], preferred_element_type=F32)
        acc = part if acc is None else acc + part
    o_ref[...] = x_ref[...] + gate_ref[0] * acc


def odd_out(o_f, o_b, p, w, x, gate, rows_per_mod, tm):
    m, d = x.shape
    k = w.shape[0]
    gcol = (2 * RET_QK_WIDTH + RET_V_WIDTH) // RET_V_WIDTH
    vmem = (2 * 4 * tm * k * 2 + k * d * 2 + 4 * tm * d * 4 + 2 * tm * d * 4
            + 6 * tm * RET_V_DIM * 4 + VMEM_SLACK)
    act = lambda c: pl.BlockSpec((tm, k), lambda i, c=c: (i, c))
    return pl.pallas_call(
        _odd_out_kernel,
        out_shape=jax.ShapeDtypeStruct((m, d), F32),
        grid=(m // tm,),
        in_specs=[
            act(0), act(0), act(gcol), act(gcol + 1),
            pl.BlockSpec((k, d), lambda i: (0, 0), pipeline_mode=pl.Buffered(1)),
            pl.BlockSpec((tm, d), lambda i: (i, 0)),
            pl.BlockSpec((1, 1, d), lambda i: ((i * tm) // rows_per_mod, 0, 0)),
        ],
        out_specs=pl.BlockSpec((tm, d), lambda i: (i, 0)),
        compiler_params=_params(("arbitrary",), vmem),
        name="odd_out",
    )(o_f, o_b, p, p, w, x, gate)


def _ffn_kernel(x_ref, sh_ref, sc_ref, gate_ref, g_ref, wg_ref, wu_ref, wd_ref, o_ref, h_ref):
    j = pl.program_id(1)

    @pl.when(j == 0)
    def _():
        _modulate_into(h_ref, x_ref, g_ref, sh_ref, sc_ref)
        o_ref[...] = jnp.zeros_like(o_ref)

    h = h_ref[...]
    a = jnp.dot(h, wg_ref[0], preferred_element_type=F32)
    u = jnp.dot(h, wu_ref[0], preferred_element_type=F32)
    act = (_silu(a) * u).astype(BF16)
    o_ref[...] += jnp.dot(act, wd_ref[0], preferred_element_type=F32)

    @pl.when(j == pl.num_programs(1) - 1)
    def _():
        o_ref[...] = x_ref[...] + gate_ref[0] * o_ref[...]


def ffn_block(x, shift, scale, gate, g, w_gate, w_up, w_down, layer, rows_per_mod, tm, tf):
    m, d = x.shape
    f = w_gate.shape[2]
    vmem = 4 * tm * d * 4 + tm * d * 2 + 2 * 3 * d * tf * 2 + 4 * tm * tf * 4 + tm * d * 4 + VMEM_SLACK
    mod_idx = lambda i, j: ((i * tm) // rows_per_mod, 0, 0)
    mod = pl.BlockSpec((1, 1, d), mod_idx)
    return pl.pallas_call(
        _ffn_kernel,
        out_shape=jax.ShapeDtypeStruct((m, d), F32),
        grid=(m // tm, f // tf),
        in_specs=[
            pl.BlockSpec((tm, d), lambda i, j: (i, 0)),
            mod, mod, mod,
            pl.BlockSpec((1, d), lambda i, j: (0, 0)),
            pl.BlockSpec((1, d, tf), lambda i, j: (layer, 0, j)),
            pl.BlockSpec((1, d, tf), lambda i, j: (layer, 0, j)),
            pl.BlockSpec((1, tf, d), lambda i, j: (layer, j, 0)),
        ],
        out_specs=pl.BlockSpec((tm, d), lambda i, j: (i, 0)),
        scratch_shapes=[pltpu.VMEM((tm, d), BF16)],
        compiler_params=_params(("arbitrary", "arbitrary"), vmem),
        name="ffn_block",
    )(x, shift, scale, gate, g, w_gate, w_up, w_down)


def _retention_kernel(lg_ref, *refs, latent, n_blocks):
    if latent:
        (qf_ref, kf_ref, vf_ref, qb_ref, kb_ref, vb_ref, s0_ref,
         of_ref, ob_ref, s_ref, dec_ref, qd_ref, kd_ref) = refs
    else:
        (qf_ref, kf_ref, vf_ref, qb_ref, kb_ref, vb_ref,
         of_ref, ob_ref, sfin_ref, s_ref, dec_ref, qd_ref, kd_ref) = refs
    h0 = pl.program_id(0) * RET_HEADS_PER_STEP
    i = pl.program_id(2)
    c = RET_BLOCK
    dk, dv = RET_QK_DIM, RET_V_DIM
    scale = RET_QK_DIM ** -0.5
    scans = [(d, hh) for d in range(2) for hh in range(RET_HEADS_PER_STEP)]

    @pl.when((pl.program_id(1) == 0) & (i == 0))
    def _():
        row = lax.broadcasted_iota(jnp.int32, (c, c), 0).astype(F32)
        col = lax.broadcasted_iota(jnp.int32, (c, c), 1).astype(F32)
        t_idx = lax.broadcasted_iota(jnp.int32, (c, 1), 0).astype(F32)
        for d, hh in scans:
            lg = lg_ref[d, h0 + hh]
            diff = row - col if d == 0 else col - row
            dec_ref[d, hh] = jnp.where(diff >= 0, jnp.exp(lg * jnp.maximum(diff, 0.0)), 0.0) * scale
            pos = t_idx if d == 0 else (c - 1.0) - t_idx
            qd_ref[d, hh] = jnp.exp(lg * (pos + 1.0))
            kd_ref[d, hh] = jnp.exp(lg * ((c - 1.0) - pos)) * scale

    @pl.when(i == 0)
    def _():
        for d, hh in scans:
            s_ref[d, hh] = (_pair_order_rows(s0_ref[0, 0, d, hh]) if latent
                            else jnp.zeros((dk, dv), F32))

    nt = (((1,), (1,)), ((), ()))
    tn = (((0,), (0,)), ((), ()))
    chunk_refs = ((qf_ref, kf_ref, vf_ref, of_ref), (qb_ref, kb_ref, vb_ref, ob_ref))
    for d, hh in scans:
        q_ref, k_ref, v_ref, o_ref = chunk_refs[d]
        q = q_ref[0, :, hh * dk:(hh + 1) * dk]
        k = k_ref[0, :, hh * dk:(hh + 1) * dk]
        v = v_ref[0, :, hh * dv:(hh + 1) * dv]
        a = lax.dot_general(q, k, nt, preferred_element_type=F32)
        inner_w = (a * dec_ref[d, hh]).astype(BF16)
        q_dec = (q.astype(F32) * qd_ref[d, hh]).astype(BF16)
        k_dec = (k.astype(F32) * kd_ref[d, hh]).astype(BF16)
        s_old = s_ref[d, hh]
        o = (jnp.dot(inner_w, v, preferred_element_type=F32)
             + jnp.dot(q_dec, s_old.astype(BF16), preferred_element_type=F32))
        s_new = (s_old * jnp.exp(lg_ref[d, h0 + hh] * c)
                 + lax.dot_general(k_dec, v, tn, preferred_element_type=F32))
        s_ref[d, hh] = s_new
        o_ref[0, :, hh * dv:(hh + 1) * dv] = o.astype(o_ref.dtype)
        if not latent:
            @pl.when(i == n_blocks - 1)
            def _(d=d, hh=hh, s_new=s_new):
                sfin_ref[0, 0, d, hh] = _pair_order_rows(s_new)


def retention(p, log_gamma, state0, layer_j):
    b, t, _ = p.shape
    latent = state0 is not None
    c = RET_BLOCK
    nb = t // c
    hps = RET_HEADS_PER_STEP
    dk, dv, nh = hps * RET_QK_DIM, hps * RET_V_DIM, RET_HEADS // hps

    def chunk_specs(blk_of):
        return [
            pl.BlockSpec((1, c, dk), lambda h, bi, i, lg: (bi, blk_of(i), h)),
            pl.BlockSpec((1, c, dk), lambda h, bi, i, lg: (bi, blk_of(i), nh + h)),
            pl.BlockSpec((1, c, dv), lambda h, bi, i, lg: (bi, blk_of(i), nh + h)),
        ]

    fwd = lambda i: i
    bwd = lambda i: nb - 1 - i
    in_specs = chunk_specs(fwd) + chunk_specs(bwd)
    args = [p] * 6
    o_shape = jax.ShapeDtypeStruct((b, t, RET_V_WIDTH), BF16)
    out_shape = [o_shape, o_shape]
    out_specs = [pl.BlockSpec((1, c, dv), lambda h, bi, i, lg: (bi, fwd(i), h)),
                 pl.BlockSpec((1, c, dv), lambda h, bi, i, lg: (bi, bwd(i), h))]
    state_dims = (RET_QK_DIM, RET_V_DIM)
    state_block = (1, 1, 2, hps) + state_dims
    if latent:
        in_specs.append(pl.BlockSpec(state_block, lambda h, bi, i, lg: (bi, layer_j, 0, h, 0, 0)))
        args.append(state0)
    else:
        out_shape.append(jax.ShapeDtypeStruct((b, 1, 2, RET_HEADS) + state_dims, F32))
        out_specs.append(pl.BlockSpec(state_block, lambda h, bi, i, lg: (bi, 0, 0, h, 0, 0)))
    return pl.pallas_call(
        functools.partial(_retention_kernel, latent=latent, n_blocks=nb),
        out_shape=tuple(out_shape),
        grid_spec=pltpu.PrefetchScalarGridSpec(
            num_scalar_prefetch=1,
            grid=(nh, b, nb),
            in_specs=in_specs,
            out_specs=tuple(out_specs),
            scratch_shapes=[pltpu.VMEM((2, hps) + state_dims, F32), pltpu.VMEM((2, hps, c, c), F32),
                            pltpu.VMEM((2, hps, c, 1), F32), pltpu.VMEM((2, hps, c, 1), F32)],
        ),
        compiler_params=_params(("arbitrary",) * 3, 40 << 20),
        name="retention",
    )(log_gamma, *args)


def _trunk(x, mods, ctx_k, ctx_v, ctx_state, wts, p_dtype):
    b, t, d = x.shape
    is_ctx = ctx_k is None
    nbm = mods.shape[1]
    rows_per_mod = (b * t) // nbm
    x2 = x.reshape(b * t, d)
    new_k = new_v = new_s = None
    for i in range(DEPTH):
        mod = [mods[i, :, k][:, None, :] for k in range(6)]
        j = i // 2
        if i % 2 == 0:
            p = modproj(x2, mod[0], mod[1], wts["norm_g"][i, 0][None], wts["even_w_in"][j],
                        rows_per_mod, p_dtype, tm=1024, tn=1024).reshape(b, t, EVEN_IN)
            qg, kg = wts["even_q_norm"][j][None], wts["even_k_norm"][j][None]
            if is_ctx:
                attn, new_k, new_v = ctx_attention(p, qg, kg)
            else:
                attn = na_attention(p, ctx_k, ctx_v, j, wts["na_rel_bias"][j], qg, kg)
            n2 = GRID_W if not is_ctx else math.isqrt(t)
            fm = fourier_mix(p, t // n2, n2)
            x2 = even_out(fm.reshape(b * t, FNET_WIDTH), attn.reshape(b * t, NA_WIDTH),
                          wts["even_w_out"][j], x2, mod[2], rows_per_mod, tm=1024, tn=1024)
        else:
            rope = (t, 2 * RET_QK_WIDTH) if not is_ctx else None
            p = modproj(x2, mod[0], mod[1], wts["norm_g"][i, 0][None], wts["odd_w_in"][j],
                        rows_per_mod, BF16, tm=1024, tn=1024, rope=rope)
            lg = jax.nn.log_sigmoid(wts["ret_decay_logit"][j].astype(F32))
            if is_ctx:
                o_f, o_b, new_s = retention(p.reshape(b, t, ODD_IN), lg, None, j)
            else:
                o_f, o_b = retention(p.reshape(b, t, ODD_IN), lg, ctx_state, j)
            x2 = odd_out(o_f.reshape(b * t, RET_V_WIDTH), o_b.reshape(b * t, RET_V_WIDTH), p,
                         wts["odd_w_out"][j], x2, mod[2], rows_per_mod, tm=256)
        x2 = ffn_block(x2, mod[3], mod[4], mod[5], wts["norm_g"][i, 1][None], wts["ffn_w_gate"],
                       wts["ffn_w_up"], wts["ffn_w_down"], i, rows_per_mod, tm=512, tf=512)
    return x2.reshape(b, t, d), new_k, new_v, new_s


def kernel(x_prompt, x_sample, cache_k, cache_v, state_ret, c, c_ctx, ada_w, ada_b, norm_g,
           even_w_in, even_q_norm, even_k_norm, na_rel_bias, even_w_out, odd_w_in, ret_decay_logit,
           odd_w_out, ffn_w_gate, ffn_w_up, ffn_w_down):
    nb_lat = c.shape[0]
    cond = jnp.concatenate(
        [c, c_ctx[None, :], jnp.zeros((MOD_ROWS - nb_lat - 1, D_MODEL), F32)], axis=0)
    mods = adaln_all(cond, ada_w, ada_b).reshape(DEPTH, MOD_ROWS, 6, D_MODEL)
    wts = dict(
        norm_g=norm_g, even_q_norm=even_q_norm, even_k_norm=even_k_norm, na_rel_bias=na_rel_bias,
        ret_decay_logit=ret_decay_logit,
        even_w_in=even_w_in.astype(BF16), even_w_out=even_w_out.astype(BF16),
        odd_w_in=odd_w_prep(odd_w_in), odd_w_out=odd_w_out.astype(BF16),
        ffn_w_gate=ffn_w_gate.astype(BF16), ffn_w_up=ffn_w_up.astype(BF16),
        ffn_w_down=ffn_w_down.astype(BF16),
    )
    y_prompt, new_k, new_v, new_s = _trunk(
        x_prompt, mods[:, nb_lat:nb_lat + 1], None, None, None, wts, F32)
    y_sample, _, _, _ = _trunk(x_sample, mods[:, :nb_lat], cache_k, cache_v, state_ret, wts, BF16)
    return (y_prompt, y_sample, new_k, new_v, new_s)
```

```python
import functools
import math

import numpy as np
import jax
import jax.numpy as jnp
from jax import lax
from jax.experimental import pallas as pl
from jax.experimental.pallas import tpu as pltpu

F32 = jnp.float32
BF16 = jnp.bfloat16

D_MODEL = 2048
DEPTH = 2
GRID_W = 64
EPS = 1e-6
NEG_INF = -1e30
LOG2_E = math.log2(math.e)
FNET_GROUPS = 8
FNET_GROUP_DIM = 128
FNET_WIDTH = 1024
NA_HEADS = 8
NA_HEAD_DIM = 128
NA_WIDTH = 1024
NA_KH = 8
NA_KW = 16
EVEN_IN = FNET_WIDTH + 3 * NA_WIDTH
RET_HEADS = 8
RET_QK_DIM = 256
RET_V_DIM = 512
RET_QK_WIDTH = RET_HEADS * RET_QK_DIM
RET_V_WIDTH = RET_HEADS * RET_V_DIM
ODD_IN = 2 * RET_QK_WIDTH + 3 * RET_V_WIDTH
RET_BLOCK = 256
RET_HEADS_PER_STEP = 4
ROPE_BASE = 10000.0
D_FF = 5632
MOD_ROWS = 8
MOD_CHUNK = 128

V7X_VMEM_BUDGET = 56 * 1024 * 1024
VMEM_SLACK = 8 * 1024 * 1024


def _params(semantics, vmem_bytes):
    return pltpu.CompilerParams(dimension_semantics=semantics,
                                vmem_limit_bytes=min(int(vmem_bytes), V7X_VMEM_BUDGET))


def _silu(x):
    return x * (1.0 / (1.0 + jnp.exp(-x)))


def _adaln_kernel(c_ref, w_ref, b_ref, o_ref):
    s = _silu(c_ref[...]).astype(BF16)
    w = w_ref[0].astype(BF16)
    o_ref[0] = jnp.dot(s, w, preferred_element_type=F32) + b_ref[0]


def adaln_all(cond, ada_w, ada_b):
    n = ada_w.shape[-1]
    tn = 1024
    return pl.pallas_call(
        _adaln_kernel,
        out_shape=jax.ShapeDtypeStruct((DEPTH, MOD_ROWS, n), F32),
        grid=(DEPTH, n // tn),
        in_specs=[
            pl.BlockSpec((MOD_ROWS, D_MODEL), lambda l, j: (0, 0)),
            pl.BlockSpec((1, D_MODEL, tn), lambda l, j: (l, 0, j)),
            pl.BlockSpec((1, 1, tn), lambda l, j: (l, 0, j)),
        ],
        out_specs=pl.BlockSpec((1, MOD_ROWS, tn), lambda l, j: (l, 0, j)),
        compiler_params=_params(("arbitrary", "arbitrary"), 40 << 20),
        name="adaln",
    )(cond, ada_w, ada_b.reshape(DEPTH, 1, n))


def _modulate_into(h_ref, x_ref, g_ref, sh_ref, sc_ref):
    shift = sh_ref[0]
    gain = g_ref[...] * (1.0 + sc_ref[0])

    def body(c, carry):
        rows = pl.ds(pl.multiple_of(c * MOD_CHUNK, MOD_CHUNK), MOD_CHUNK)
        x = x_ref[rows, :]
        ms = jnp.mean(x * x, axis=-1, keepdims=True)
        h_ref[rows, :] = (x * lax.rsqrt(ms + EPS) * gain + shift).astype(h_ref.dtype)
        return carry

    lax.fori_loop(0, x_ref.shape[0] // MOD_CHUNK, body, 0, unroll=2)


ROPE_QUARTER = RET_QK_DIM // 4


def _pair_order(x, axis):
    q = ROPE_QUARTER
    parts = [lax.slice_in_dim(x, a * q, (a + 1) * q, axis=axis) for a in (0, 2, 1, 3)]
    return jnp.concatenate(parts, axis=axis)


def _pair_order_rows(s):
    return _pair_order(s, 0)


def _odd_w_prep_kernel(w_ref, o_ref):
    for head in range(2 * RET_HEADS):
        cs = slice(head * RET_QK_DIM, (head + 1) * RET_QK_DIM)
        o_ref[0, :, cs] = _pair_order(w_ref[0, :, cs], 1).astype(o_ref.dtype)
    rest = 2 * RET_QK_WIDTH
    o_ref[0, :, rest:] = w_ref[0, :, rest:].astype(o_ref.dtype)


def odd_w_prep(w):
    n, d, width = w.shape
    tr = 128
    return pl.pallas_call(
        _odd_w_prep_kernel,
        out_shape=jax.ShapeDtypeStruct(w.shape, BF16),
        grid=(n, d // tr),
        in_specs=[pl.BlockSpec((1, tr, width), lambda l, i: (l, i, 0))],
        out_specs=pl.BlockSpec((1, tr, width), lambda l, i: (l, i, 0)),
        compiler_params=_params(("arbitrary", "arbitrary"), 2 * tr * width * 6 + 3 * tr * width * 4 + VMEM_SLACK),
        name="odd_w_prep",
    )(w)


def _rope_tables(t_len):
    inv = ROPE_BASE ** (-jnp.arange(ROPE_QUARTER, dtype=F32) / ROPE_QUARTER)
    t = jnp.arange(t_len)
    ang = jnp.concatenate([(t // GRID_W).astype(F32)[:, None] * inv[None, :],
                           (t % GRID_W).astype(F32)[:, None] * inv[None, :]], axis=-1)
    return jnp.cos(ang), jnp.sin(ang)


def _rope(x, cos, sin):
    half = RET_QK_DIM // 2
    x1, x2 = x[:, :half], x[:, half:]
    return jnp.concatenate([x1 * cos - x2 * sin, x1 * sin + x2 * cos], axis=-1)


def _modproj_kernel(*refs, rope_tiles):
    if rope_tiles:
        x_ref, sh_ref, sc_ref, g_ref, w_ref, cos_ref, sin_ref, o_ref, h_ref = refs
    else:
        x_ref, sh_ref, sc_ref, g_ref, w_ref, o_ref, h_ref = refs
    j = pl.program_id(1)

    @pl.when(j == 0)
    def _():
        _modulate_into(h_ref, x_ref, g_ref, sh_ref, sc_ref)

    acc = jnp.dot(h_ref[...], w_ref[...], preferred_element_type=F32)
    if not rope_tiles:
        o_ref[...] = acc.astype(o_ref.dtype)
        return

    @pl.when(j < rope_tiles)
    def _():
        cos = cos_ref[...]
        sin = sin_ref[...]
        for s in range(acc.shape[1] // RET_QK_DIM):
            cs = slice(s * RET_QK_DIM, (s + 1) * RET_QK_DIM)
            o_ref[:, cs] = _rope(acc[:, cs], cos, sin).astype(o_ref.dtype)

    @pl.when(j >= rope_tiles)
    def _():
        o_ref[...] = acc.astype(o_ref.dtype)


def modproj(x, shift, scale, g, w, rows_per_mod, out_dtype, tm, tn, rope=None):
    m, d = x.shape
    n = w.shape[1]
    osz = jnp.dtype(out_dtype).itemsize
    vmem = 2 * tm * d * 4 + 2 * d * tn * 2 + 2 * tm * tn * osz + tm * d * 2 + 2 * tm * tn * 4 + VMEM_SLACK
    mod_idx = lambda i, j: ((i * tm) // rows_per_mod, 0, 0)
    in_specs = [
        pl.BlockSpec((tm, d), lambda i, j: (i, 0)),
        pl.BlockSpec((1, 1, d), mod_idx),
        pl.BlockSpec((1, 1, d), mod_idx),
        pl.BlockSpec((1, d), lambda i, j: (0, 0)),
        pl.BlockSpec((d, tn), lambda i, j: (0, j)),
    ]
    args = [x, shift, scale, g, w]
    rope_tiles = 0
    if rope is not None:
        seq_len, n_cols = rope
        rope_tiles = n_cols // tn
        cos, sin = _rope_tables(seq_len)
        tab = pl.BlockSpec((tm, RET_QK_DIM // 2), lambda i, j: (i % (seq_len // tm), 0))
        in_specs += [tab, tab]
        args += [cos, sin]
        vmem += 2 * 2 * tm * (RET_QK_DIM // 2) * 4
    return pl.pallas_call(
        functools.partial(_modproj_kernel, rope_tiles=rope_tiles),
        out_shape=jax.ShapeDtypeStruct((m, n), out_dtype),
        grid=(m // tm, n // tn),
        in_specs=in_specs,
        out_specs=pl.BlockSpec((tm, tn), lambda i, j: (i, j)),
        scratch_shapes=[pltpu.VMEM((tm, d), BF16)],
        compiler_params=_params(("arbitrary", "arbitrary"), vmem),
        name="modproj",
    )(*args)


def _head_rms(x, g):
    return x * lax.rsqrt(jnp.mean(x * x, axis=-1, keepdims=True) + EPS) * g


def _ctx_attn_kernel(q_ref, k_ref, v_ref, qg_ref, kg_ref, o_ref, nk_ref, nv_ref):
    scale = NA_HEAD_DIM ** -0.5
    for h in range(NA_HEADS):
        cs = slice(h * NA_HEAD_DIM, (h + 1) * NA_HEAD_DIM)
        q = _head_rms(q_ref[0, :, cs].astype(F32), qg_ref[...])
        k = _head_rms(k_ref[0, :, cs].astype(F32), kg_ref[...])
        v = v_ref[0, :, cs].astype(F32)
        nk_ref[0, 0, h] = k
        nv_ref[0, 0, h] = v
        s = lax.dot_general(q.astype(BF16), k.astype(BF16), (((1,), (1,)), ((), ())),
                            preferred_element_type=F32) * scale
        m = jnp.max(s, axis=-1, keepdims=True)
        e = jnp.exp(s - m)
        l = jnp.sum(e, axis=-1, keepdims=True)
        o = jnp.dot(e.astype(BF16), v.astype(BF16), preferred_element_type=F32) / l
        o_ref[0, :, cs] = o.astype(o_ref.dtype)


def ctx_attention(p, q_gain, k_gain):
    b, t, _ = p.shape
    cache_shape = (b, 1, NA_HEADS, t, NA_HEAD_DIM)
    blk = lambda c: pl.BlockSpec((1, t, NA_WIDTH), lambda i, c=c: (i, 0, c))
    gain = pl.BlockSpec((1, NA_HEAD_DIM), lambda i: (0, 0))
    cache_spec = pl.BlockSpec((1, 1, NA_HEADS, t, NA_HEAD_DIM), lambda i: (i, 0, 0, 0, 0))
    return pl.pallas_call(
        _ctx_attn_kernel,
        out_shape=(jax.ShapeDtypeStruct((b, t, NA_WIDTH), BF16),
                   jax.ShapeDtypeStruct(cache_shape, F32),
                   jax.ShapeDtypeStruct(cache_shape, F32)),
        grid=(b,),
        in_specs=[blk(1), blk(2), blk(3), gain, gain],
        out_specs=(pl.BlockSpec((1, t, NA_WIDTH), lambda i: (i, 0, 0)), cache_spec, cache_spec),
        compiler_params=_params(("arbitrary",), 32 << 20),
        name="ctx_attention",
    )(p, p, p, q_gain, k_gain)


NA_BIAS_ROWS = 2 * NA_KH - 1
NA_BIAS_COLS = 2 * NA_KW - 1
NA_MASK_TILE = NA_BIAS_ROWS
NA_QROWS = 4
NA_KROWS = NA_KH + NA_QROWS


def _na_build_bias(rb_ref, bias_ref, head):
    shape = (GRID_W, 2 * GRID_W)
    lane = lax.broadcasted_iota(jnp.int32, shape, 1)
    qc = lax.broadcasted_iota(jnp.int32, shape, 0)
    kc = lane & (GRID_W - 1)
    start = jnp.clip(qc - NA_KW // 2, 0, GRID_W - NA_KW)
    ok = (kc >= start) & (kc < start + NA_KW)
    delta = kc - qc + (NA_KW - 1)
    left = lane < GRID_W
    base = head * (NA_BIAS_ROWS * NA_BIAS_COLS)

    def build(dr, carry):
        acc = jnp.full(shape, NEG_INF, F32)
        for dc in range(NA_BIAS_COLS):
            acc = jnp.where(delta == dc, rb_ref[base + dr * NA_BIAS_COLS + dc], acc)
        tile = jnp.where(ok, acc * LOG2_E, NEG_INF)
        bias_ref[0, dr] = jnp.where(left, tile, 0.0)
        bias_ref[1, dr] = jnp.where(left, 0.0, tile)
        return carry

    lax.fori_loop(0, NA_BIAS_ROWS, build, 0)
    bias_ref[0, NA_MASK_TILE] = jnp.where(left, NEG_INF, 0.0)
    bias_ref[1, NA_MASK_TILE] = jnp.where(left, 0.0, NEG_INF)


def _na_kernel(rb_ref, q_ref, k_ref, v_ref, kc_ref, vc_ref, qg_ref, kg_ref, o_ref,
               qs_ref, ks_ref, vs_ref, bias_ref, *, rows):
    @pl.when(pl.program_id(1) == 0)
    def _():
        _na_build_bias(rb_ref, bias_ref, pl.program_id(0))

    scale = NA_HEAD_DIM ** -0.5 * LOG2_E
    qs_ref[...] = (_head_rms(q_ref[0].astype(F32), qg_ref[...]) * scale).astype(BF16)
    ks_ref[...] = _head_rms(k_ref[0].astype(F32), kg_ref[...]).astype(BF16)
    vs_ref[...] = v_ref[0].astype(BF16)
    kc = kc_ref[0, 0, 0].astype(BF16)
    vc = vc_ref[0, 0, 0].astype(BF16)
    nt = (((1,), (1,)), ((), ()))
    nq = NA_QROWS * GRID_W
    nk = NA_KROWS * GRID_W

    def tile_index(r, rs, kr):
        inside = (kr >= rs) & (kr < rs + NA_KH)
        return jnp.where(inside, kr - r + (NA_KH - 1), NA_MASK_TILE)

    def body(blk, carry):
        r0 = blk * NA_QROWS
        k0 = jnp.clip(r0 - NA_KH // 2, 0, rows - NA_KROWS)
        q = qs_ref[pl.ds(pl.multiple_of(r0 * GRID_W, nq), nq), :]
        kw = ks_ref[pl.ds(pl.multiple_of(k0 * GRID_W, GRID_W), nk), :]
        vw = vs_ref[pl.ds(pl.multiple_of(k0 * GRID_W, GRID_W), nk), :]
        bias_rows = []
        for i in range(NA_QROWS):
            r = r0 + i
            rs = jnp.clip(r - NA_KH // 2, 0, rows - NA_KH)
            pairs = [bias_ref[0, tile_index(r, rs, k0 + 2 * jp)]
                     + bias_ref[1, tile_index(r, rs, k0 + 2 * jp + 1)]
                     for jp in range(NA_KROWS // 2)]
            bias_rows.append(jnp.concatenate(pairs, axis=-1))
        bias = jnp.concatenate(bias_rows, axis=0)
        s_w = lax.dot_general(q, kw, nt, preferred_element_type=F32) + bias
        s_c = lax.dot_general(q, kc, nt, preferred_element_type=F32)
        m = jnp.maximum(jnp.max(s_w, axis=-1, keepdims=True), jnp.max(s_c, axis=-1, keepdims=True))
        e_w = jnp.exp2(s_w - m)
        e_c = jnp.exp2(s_c - m)
        l = jnp.sum(e_w, axis=-1, keepdims=True) + jnp.sum(e_c, axis=-1, keepdims=True)
        o = (jnp.dot(e_w.astype(BF16), vw, preferred_element_type=F32)
             + jnp.dot(e_c.astype(BF16), vc, preferred_element_type=F32)) / l
        o_ref[0, pl.ds(pl.multiple_of(r0 * GRID_W, nq), nq), :] = o.astype(o_ref.dtype)
        return carry

    lax.fori_loop(0, rows // NA_QROWS, body, 0, unroll=2)


def na_attention(p, cache_k, cache_v, layer_j, rel_bias, q_gain, k_gain):
    b, t, _ = p.shape
    past = cache_k.shape[3]
    rows = t // GRID_W
    hd = NA_HEAD_DIM
    col0 = FNET_WIDTH // hd
    blk = lambda c: pl.BlockSpec((1, t, hd), lambda h, i, rb, c=c: (i, 0, col0 + c * NA_HEADS + h))
    cache_spec = pl.BlockSpec((1, 1, 1, past, hd), lambda h, i, rb: (i, layer_j, h, 0, 0))
    gain = pl.BlockSpec((1, hd), lambda h, i, rb: (0, 0))
    return pl.pallas_call(
        functools.partial(_na_kernel, rows=rows),
        out_shape=jax.ShapeDtypeStruct((b, t, NA_WIDTH), BF16),
        grid_spec=pltpu.PrefetchScalarGridSpec(
            num_scalar_prefetch=1,
            grid=(NA_HEADS, b),
            in_specs=[blk(0), blk(1), blk(2), cache_spec, cache_spec, gain, gain],
            out_specs=pl.BlockSpec((1, t, hd), lambda h, i, rb: (i, 0, h)),
            scratch_shapes=[pltpu.VMEM((t, hd), BF16)] * 3
            + [pltpu.VMEM((2, NA_BIAS_ROWS + 1, GRID_W, 2 * GRID_W), F32)],
        ),
        compiler_params=_params(("arbitrary", "arbitrary"), 40 << 20),
        name="na_attention",
    )(rel_bias.astype(F32).reshape(-1), p, p, p, cache_k, cache_v, q_gain, k_gain)


FNET_BLOCK = 16


def _fnet_tables(n1, n2):
    t_len = n1 * n2
    tb = FNET_BLOCK
    eye = np.eye(tb)
    k1 = np.arange(n1)
    ang1 = (2.0 * np.pi / n1) * ((k1[:, None] * k1[None, :]) % n1)
    f1 = np.concatenate([np.cos(ang1), -np.sin(ang1)], axis=0) / math.sqrt(t_len)
    rows_mat = np.kron(f1, eye)
    t2 = np.arange(n2)
    tw = (2.0 * np.pi / t_len) * (k1[:, None] * t2[None, :])
    tw = tw.reshape(n1, n2 // tb, tb).transpose(1, 0, 2).reshape(n2 // tb, n1 * tb, 1)
    c = np.arange(FNET_GROUP_DIM)
    ang3 = (2.0 * np.pi / FNET_GROUP_DIM) * ((c[:, None] * c[None, :]) % FNET_GROUP_DIM)
    c3, s3 = np.cos(ang3), np.sin(ang3)
    chan = np.block([[c3, -s3], [s3, c3]]) / math.sqrt(FNET_GROUP_DIM)
    ang2 = (2.0 * np.pi / n2) * ((t2[:, None] * t2[None, :]) % n2)
    cols_re = np.einsum("pq,kt->kpqt", eye, np.cos(ang2)).reshape(n2 * tb, tb * n2)
    cols_im = np.einsum("pq,kt->kpqt", eye, np.sin(ang2)).reshape(n2 * tb, tb * n2)
    f32 = lambda x: jnp.asarray(x, dtype=F32)
    bf = lambda x: f32(x).astype(BF16)
    return bf(rows_mat), f32(np.cos(tw)), f32(np.sin(tw)), bf(chan), bf(cols_re), bf(cols_im)


def _fnet_rows_kernel(x_ref, mat_ref, twc_ref, tws_ref, chan_ref, vr_ref, vi_ref):
    _, n1, tb, width = x_ref.shape
    rows = n1 * tb
    gd = FNET_GROUP_DIM
    x = x_ref[0].reshape(rows, width).astype(BF16)
    h = jnp.dot(mat_ref[...], x, preferred_element_type=F32)
    hr, hi = h[:rows], h[rows:]
    c, s = twc_ref[0], tws_ref[0]
    gr = (hr * c + hi * s).astype(BF16)
    gi = (hi * c - hr * s).astype(BF16)
    stacked = jnp.concatenate(
        [jnp.concatenate([gr[:, g * gd:(g + 1) * gd], gi[:, g * gd:(g + 1) * gd]], axis=1)
         for g in range(FNET_GROUPS)], axis=0)
    v = jnp.dot(stacked, chan_ref[...], preferred_element_type=F32)
    for g in range(FNET_GROUPS):
        part = v[g * rows:(g + 1) * rows]
        vr_ref[0, :, :, g * gd:(g + 1) * gd] = part[:, :gd].reshape(n1, tb, gd).astype(vr_ref.dtype)
        vi_ref[0, :, :, g * gd:(g + 1) * gd] = part[:, gd:].reshape(n1, tb, gd).astype(vi_ref.dtype)


def _fnet_cols_kernel(vr_ref, vi_ref, re_ref, im_ref, o_ref):
    _, kb, n2, width = vr_ref.shape
    vr = vr_ref[0].reshape(kb * n2, width)
    vi = vi_ref[0].reshape(kb * n2, width)
    y = (jnp.dot(re_ref[...], vr, preferred_element_type=F32)
         + jnp.dot(im_ref[...], vi, preferred_element_type=F32))
    o_ref[0] = y.reshape(n2, kb, width).astype(o_ref.dtype)


def fourier_mix(p, n1, n2):
    b, t, width = p.shape
    tb = FNET_BLOCK
    w = FNET_WIDTH
    rows_mat, twc, tws, chan, cols_re, cols_im = _fnet_tables(n1, n2)
    const = lambda shape: pl.BlockSpec(shape, lambda i, s: (0,) * len(shape))
    tw_spec = pl.BlockSpec((1, n1 * tb, 1), lambda i, s: (s, 0, 0))
    v_shape = jax.ShapeDtypeStruct((b, n1, n2, w), BF16)
    v_spec = pl.BlockSpec((1, n1, tb, w), lambda i, s: (i, 0, s, 0))
    isz = p.dtype.itemsize
    vmem = (2 * n1 * tb * w * isz + 2 * rows_mat.size * 2 + 4 * n1 * tb * w * 2
            + 6 * n1 * tb * w * 4 + VMEM_SLACK)
    vr, vi = pl.pallas_call(
        _fnet_rows_kernel,
        out_shape=(v_shape, v_shape),
        grid=(b, n2 // tb),
        in_specs=[pl.BlockSpec((1, n1, tb, w), lambda i, s: (i, 0, s, 0)),
                  const(rows_mat.shape), tw_spec, tw_spec, const(chan.shape)],
        out_specs=(v_spec, v_spec),
        compiler_params=_params(("arbitrary", "arbitrary"), vmem),
        name="fnet_rows",
    )(p.reshape(b, n1, n2, width), rows_mat, twc, tws, chan)
    v_in = pl.BlockSpec((1, tb, n2, w), lambda i, s: (i, s, 0, 0))
    vmem = 2 * 2 * tb * n2 * w * 2 + 2 * 2 * cols_re.size * 2 + 2 * n2 * tb * w * 2 + 3 * n2 * tb * w * 4 + VMEM_SLACK
    out = pl.pallas_call(
        _fnet_cols_kernel,
        out_shape=jax.ShapeDtypeStruct((b, n2, n1, w), BF16),
        grid=(b, n1 // tb),
        in_specs=[v_in, v_in, const(cols_re.shape), const(cols_im.shape)],
        out_specs=pl.BlockSpec((1, n2, tb, w), lambda i, s: (i, 0, s, 0)),
        compiler_params=_params(("arbitrary", "arbitrary"), vmem),
        name="fnet_cols",
    )(vr, vi, cols_re, cols_im)
    return out.reshape(b, t, w)


def _even_out_kernel(f_ref, a_ref, wf_ref, wa_ref, x_ref, gate_ref, o_ref):
    y = (jnp.dot(f_ref[...], wf_ref[...], preferred_element_type=F32)
         + jnp.dot(a_ref[...], wa_ref[...], preferred_element_type=F32))
    o_ref[...] = x_ref[...] + gate_ref[0] * y


def even_out(f, a, w, x, gate, rows_per_mod, tm, tn):
    m, d = x.shape
    kf = f.shape[1]
    vmem = 2 * 2 * tm * kf * 2 + 2 * 2 * kf * tn * 2 + 4 * tm * tn * 4 + 2 * tm * tn * 4 + VMEM_SLACK
    return pl.pallas_call(
        _even_out_kernel,
        out_shape=jax.ShapeDtypeStruct((m, d), F32),
        grid=(d // tn, m // tm),
        in_specs=[
            pl.BlockSpec((tm, kf), lambda j, i: (i, 0)),
            pl.BlockSpec((tm, kf), lambda j, i: (i, 0)),
            pl.BlockSpec((kf, tn), lambda j, i: (0, j)),
            pl.BlockSpec((kf, tn), lambda j, i: (1, j)),
            pl.BlockSpec((tm, tn), lambda j, i: (i, j)),
            pl.BlockSpec((1, 1, tn), lambda j, i: ((i * tm) // rows_per_mod, 0, j)),
        ],
        out_specs=pl.BlockSpec((tm, tn), lambda j, i: (i, j)),
        compiler_params=_params(("arbitrary", "arbitrary"), vmem),
        name="even_out",
    )(f, a, w, w, x, gate)


def _gated_group_norm(o, g):
    o = o.astype(F32)
    oc = o - jnp.mean(o, axis=-1, keepdims=True)
    gn = oc * lax.rsqrt(jnp.mean(oc * oc, axis=-1, keepdims=True) + EPS)
    hg = 0.5 * g.astype(F32)
    return (hg + hg * jnp.tanh(hg)) * gn


def _odd_out_kernel(of_ref, ob_ref, gf_ref, gb_ref, w_ref, x_ref, gate_ref, o_ref):
    acc = None
    for h in range(RET_HEADS):
        cs = slice(h * RET_V_DIM, (h + 1) * RET_V_DIM)
        y = (_gated_group_norm(of_ref[:, cs], gf_ref[:, cs])
             + _gated_group_norm(ob_ref[:, cs], gb_ref[:, cs])).astype(BF16)
        part = jnp.dot(y, w_ref[cs, :], preferred_element_type=F32)
        acc = part if acc is None else acc + part
    o_ref[...] = x_ref[...] + gate_ref[0] * acc


def odd_out(o_f, o_b, p, w, x, gate, rows_per_mod, tm):
    m, d = x.shape
    k = w.shape[0]
    gcol = (2 * RET_QK_WIDTH + RET_V_WIDTH) // RET_V_WIDTH
    vmem = (2 * 4 * tm * k * 2 + k * d * 2 + 4 * tm * d * 4 + 2 * tm * d * 4
            + 6 * tm * RET_V_DIM * 4 + VMEM_SLACK)
    act = lambda c: pl.BlockSpec((tm, k), lambda i, c=c: (i, c))
    return pl.pallas_call(
        _odd_out_kernel,
        out_shape=jax.ShapeDtypeStruct((m, d), F32),
        grid=(m // tm,),
        in_specs=[
            act(0), act(0), act(gcol), act(gcol + 1),
            pl.BlockSpec((k, d), lambda i: (0, 0), pipeline_mode=pl.Buffered(1)),
            pl.BlockSpec((tm, d), lambda i: (i, 0)),
            pl.BlockSpec((1, 1, d), lambda i: ((i * tm) // rows_per_mod, 0, 0)),
        ],
        out_specs=pl.BlockSpec((tm, d), lambda i: (i, 0)),
        compiler_params=_params(("arbitrary",), vmem),
        name="odd_out",
    )(o_f, o_b, p, p, w, x, gate)


FFN_TILE = 512


def _ffn_w_prep_kernel(wg_ref, wu_ref, o_ref):
    o_ref[0, 0, :, :FFN_TILE] = wg_ref[0].astype(o_ref.dtype)
    o_ref[0, 0, :, FFN_TILE:] = wu_ref[0].astype(o_ref.dtype)


def ffn_w_prep(w_gate, w_up):
    n, d, f = w_gate.shape
    tf = FFN_TILE
    src = pl.BlockSpec((1, d, tf), lambda l, j: (l, 0, j))
    return pl.pallas_call(
        _ffn_w_prep_kernel,
        out_shape=jax.ShapeDtypeStruct((n, f // tf, d, 2 * tf), BF16),
        grid=(n, f // tf),
        in_specs=[src, src],
        out_specs=pl.BlockSpec((1, 1, d, 2 * tf), lambda l, j: (l, j, 0, 0)),
        compiler_params=_params(("arbitrary", "arbitrary"), 2 * 3 * d * tf * 4 + VMEM_SLACK),
        name="ffn_w_prep",
    )(w_gate, w_up)


def _ffn_kernel(x_ref, sh_ref, sc_ref, gate_ref, g_ref, wgu_ref, wd_ref, o_ref, h_ref):
    j = pl.program_id(1)

    @pl.when(j == 0)
    def _():
        _modulate_into(h_ref, x_ref, g_ref, sh_ref, sc_ref)
        o_ref[...] = jnp.zeros_like(o_ref)

    gu = jnp.dot(h_ref[...], wgu_ref[0, 0], preferred_element_type=F32)
    act = (_silu(gu[:, :FFN_TILE]) * gu[:, FFN_TILE:]).astype(BF16)
    o_ref[...] += jnp.dot(act, wd_ref[0], preferred_element_type=F32)

    @pl.when(j == pl.num_programs(1) - 1)
    def _():
        o_ref[...] = x_ref[...] + gate_ref[0] * o_ref[...]


def ffn_block(x, shift, scale, gate, g, w_gu, w_down, layer, rows_per_mod, tm):
    m, d = x.shape
    f = w_down.shape[1]
    tf = FFN_TILE
    vmem = 4 * tm * d * 4 + tm * d * 2 + 2 * 3 * d * tf * 2 + 4 * tm * tf * 4 + tm * d * 4 + VMEM_SLACK
    mod_idx = lambda i, j: ((i * tm) // rows_per_mod, 0, 0)
    mod = pl.BlockSpec((1, 1, d), mod_idx)
    return pl.pallas_call(
        _ffn_kernel,
        out_shape=jax.ShapeDtypeStruct((m, d), F32),
        grid=(m // tm, f // tf),
        in_specs=[
            pl.BlockSpec((tm, d), lambda i, j: (i, 0)),
            mod, mod, mod,
            pl.BlockSpec((1, d), lambda i, j: (0, 0)),
            pl.BlockSpec((1, 1, d, 2 * tf), lambda i, j: (layer, j, 0, 0)),
            pl.BlockSpec((1, tf, d), lambda i, j: (layer, j, 0)),
        ],
        out_specs=pl.BlockSpec((tm, d), lambda i, j: (i, 0)),
        scratch_shapes=[pltpu.VMEM((tm, d), BF16)],
        compiler_params=_params(("arbitrary", "arbitrary"), vmem),
        name="ffn_block",
    )(x, shift, scale, gate, g, w_gu, w_down)


def _retention_kernel(lg_ref, *refs, latent, n_blocks):
    if latent:
        (qf_ref, kf_ref, vf_ref, qb_ref, kb_ref, vb_ref, s0_ref,
         of_ref, ob_ref, s_ref, dec_ref, qd_ref, kd_ref) = refs
    else:
        (qf_ref, kf_ref, vf_ref, qb_ref, kb_ref, vb_ref,
         of_ref, ob_ref, sfin_ref, s_ref, dec_ref, qd_ref, kd_ref) = refs
    h0 = pl.program_id(0) * RET_HEADS_PER_STEP
    i = pl.program_id(2)
    c = RET_BLOCK
    dk, dv = RET_QK_DIM, RET_V_DIM
    scale = RET_QK_DIM ** -0.5
    scans = [(d, hh) for d in range(2) for hh in range(RET_HEADS_PER_STEP)]

    @pl.when((pl.program_id(1) == 0) & (i == 0))
    def _():
        row = lax.broadcasted_iota(jnp.int32, (c, c), 0).astype(F32)
        col = lax.broadcasted_iota(jnp.int32, (c, c), 1).astype(F32)
        t_idx = lax.broadcasted_iota(jnp.int32, (c, 1), 0).astype(F32)
        for d, hh in scans:
            lg = lg_ref[d, h0 + hh]
            diff = row - col if d == 0 else col - row
            dec_ref[d, hh] = jnp.where(diff >= 0, jnp.exp(lg * jnp.maximum(diff, 0.0)), 0.0) * scale
            pos = t_idx if d == 0 else (c - 1.0) - t_idx
            qd_ref[d, hh] = jnp.exp(lg * (pos + 1.0))
            kd_ref[d, hh] = jnp.exp(lg * ((c - 1.0) - pos)) * scale

    @pl.when(i == 0)
    def _():
        for d, hh in scans:
            s_ref[d, hh] = (_pair_order_rows(s0_ref[0, 0, d, hh]) if latent
                            else jnp.zeros((dk, dv), F32))

    nt = (((1,), (1,)), ((), ()))
    tn = (((0,), (0,)), ((), ()))
    chunk_refs = ((qf_ref, kf_ref, vf_ref, of_ref), (qb_ref, kb_ref, vb_ref, ob_ref))
    for d, hh in scans:
        q_ref, k_ref, v_ref, o_ref = chunk_refs[d]
        q = q_ref[0, :, hh * dk:(hh + 1) * dk]
        k = k_ref[0, :, hh * dk:(hh + 1) * dk]
        v = v_ref[0, :, hh * dv:(hh + 1) * dv]
        a = lax.dot_general(q, k, nt, preferred_element_type=F32)
        inner_w = (a * dec_ref[d, hh]).astype(BF16)
        q_dec = (q.astype(F32) * qd_ref[d, hh]).astype(BF16)
        k_dec = (k.astype(F32) * kd_ref[d, hh]).astype(BF16)
        s_old = s_ref[d, hh]
        o = (jnp.dot(inner_w, v, preferred_element_type=F32)
             + jnp.dot(q_dec, s_old.astype(BF16), preferred_element_type=F32))
        s_new = (s_old * jnp.exp(lg_ref[d, h0 + hh] * c)
                 + lax.dot_general(k_dec, v, tn, preferred_element_type=F32))
        s_ref[d, hh] = s_new
        o_ref[0, :, hh * dv:(hh + 1) * dv] = o.astype(o_ref.dtype)
        if not latent:
            @pl.when(i == n_blocks - 1)
            def _(d=d, hh=hh, s_new=s_new):
                sfin_ref[0, 0, d, hh] = _pair_order_rows(s_new)


def retention(p, log_gamma, state0, layer_j):
    b, t, _ = p.shape
    latent = state0 is not None
    c = RET_BLOCK
    nb = t // c
    hps = RET_HEADS_PER_STEP
    dk, dv, nh = hps * RET_QK_DIM, hps * RET_V_DIM, RET_HEADS // hps

    def chunk_specs(blk_of):
        return [
            pl.BlockSpec((1, c, dk), lambda h, bi, i, lg: (bi, blk_of(i), h)),
            pl.BlockSpec((1, c, dk), lambda h, bi, i, lg: (bi, blk_of(i), nh + h)),
            pl.BlockSpec((1, c, dv), lambda h, bi, i, lg: (bi, blk_of(i), nh + h)),
        ]

    fwd = lambda i: i
    bwd = lambda i: nb - 1 - i
    in_specs = chunk_specs(fwd) + chunk_specs(bwd)
    args = [p] * 6
    o_shape = jax.ShapeDtypeStruct((b, t, RET_V_WIDTH), BF16)
    out_shape = [o_shape, o_shape]
    out_specs = [pl.BlockSpec((1, c, dv), lambda h, bi, i, lg: (bi, fwd(i), h)),
                 pl.BlockSpec((1, c, dv), lambda h, bi, i, lg: (bi, bwd(i), h))]
    state_dims = (RET_QK_DIM, RET_V_DIM)
    state_block = (1, 1, 2, hps) + state_dims
    if latent:
        in_specs.append(pl.BlockSpec(state_block, lambda h, bi, i, lg: (bi, layer_j, 0, h, 0, 0)))
        args.append(state0)
    else:
        out_shape.append(jax.ShapeDtypeStruct((b, 1, 2, RET_HEADS) + state_dims, F32))
        out_specs.append(pl.BlockSpec(state_block, lambda h, bi, i, lg: (bi, 0, 0, h, 0, 0)))
    return pl.pallas_call(
        functools.partial(_retention_kernel, latent=latent, n_blocks=nb),
        out_shape=tuple(out_shape),
        grid_spec=pltpu.PrefetchScalarGridSpec(
            num_scalar_prefetch=1,
            grid=(nh, b, nb),
            in_specs=in_specs,
            out_specs=tuple(out_specs),
            scratch_shapes=[pltpu.VMEM((2, hps) + state_dims, F32), pltpu.VMEM((2, hps, c, c), F32),
                            pltpu.VMEM((2, hps, c, 1), F32), pltpu.VMEM((2, hps, c, 1), F32)],
        ),
        compiler_params=_params(("arbitrary",) * 3, 40 << 20),
        name="retention",
    )(log_gamma, *args)


def _trunk(x, mods, ctx_k, ctx_v, ctx_state, wts, p_dtype):
    b, t, d = x.shape
    is_ctx = ctx_k is None
    nbm = mods.shape[1]
    rows_per_mod = (b * t) // nbm
    x2 = x.reshape(b * t, d)
    new_k = new_v = new_s = None
    for i in range(DEPTH):
        mod = [mods[i, :, k][:, None, :] for k in range(6)]
        j = i // 2
        if i % 2 == 0:
            p = modproj(x2, mod[0], mod[1], wts["norm_g"][i, 0][None], wts["even_w_in"][j],
                        rows_per_mod, p_dtype, tm=1024, tn=1024).reshape(b, t, EVEN_IN)
            qg, kg = wts["even_q_norm"][j][None], wts["even_k_norm"][j][None]
            if is_ctx:
                attn, new_k, new_v = ctx_attention(p, qg, kg)
            else:
                attn = na_attention(p, ctx_k, ctx_v, j, wts["na_rel_bias"][j], qg, kg)
            n2 = GRID_W if not is_ctx else math.isqrt(t)
            fm = fourier_mix(p, t // n2, n2)
            x2 = even_out(fm.reshape(b * t, FNET_WIDTH), attn.reshape(b * t, NA_WIDTH),
                          wts["even_w_out"][j], x2, mod[2], rows_per_mod, tm=1024, tn=1024)
        else:
            rope = (t, 2 * RET_QK_WIDTH) if not is_ctx else None
            p = modproj(x2, mod[0], mod[1], wts["norm_g"][i, 0][None], wts["odd_w_in"][j],
                        rows_per_mod, BF16, tm=1024, tn=1024, rope=rope)
            lg = jax.nn.log_sigmoid(wts["ret_decay_logit"][j].astype(F32))
            if is_ctx:
                o_f, o_b, new_s = retention(p.reshape(b, t, ODD_IN), lg, None, j)
            else:
                o_f, o_b = retention(p.reshape(b, t, ODD_IN), lg, ctx_state, j)
            x2 = odd_out(o_f.reshape(b * t, RET_V_WIDTH), o_b.reshape(b * t, RET_V_WIDTH), p,
                         wts["odd_w_out"][j], x2, mod[2], rows_per_mod, tm=256)
        x2 = ffn_block(x2, mod[3], mod[4], mod[5], wts["norm_g"][i, 1][None], wts["ffn_w_gu"],
                       wts["ffn_w_down"], i, rows_per_mod, tm=512)
    return x2.reshape(b, t, d), new_k, new_v, new_s


def kernel(x_prompt, x_sample, cache_k, cache_v, state_ret, c, c_ctx, ada_w, ada_b, norm_g,
           even_w_in, even_q_norm, even_k_norm, na_rel_bias, even_w_out, odd_w_in, ret_decay_logit,
           odd_w_out, ffn_w_gate, ffn_w_up, ffn_w_down):
    nb_lat = c.shape[0]
    cond = jnp.concatenate(
        [c, c_ctx[None, :], jnp.zeros((MOD_ROWS - nb_lat - 1, D_MODEL), F32)], axis=0)
    mods = adaln_all(cond, ada_w, ada_b).reshape(DEPTH, MOD_ROWS, 6, D_MODEL)
    wts = dict(
        norm_g=norm_g, even_q_norm=even_q_norm, even_k_norm=even_k_norm, na_rel_bias=na_rel_bias,
        ret_decay_logit=ret_decay_logit,
        even_w_in=even_w_in.astype(BF16), even_w_out=even_w_out.astype(BF16),
        odd_w_in=odd_w_prep(odd_w_in), odd_w_out=odd_w_out.astype(BF16),
        ffn_w_gu=ffn_w_prep(ffn_w_gate, ffn_w_up), ffn_w_down=ffn_w_down.astype(BF16),
    )
    y_prompt, new_k, new_v, new_s = _trunk(
        x_prompt, mods[:, nb_lat:nb_lat + 1], None, None, None, wts, F32)
    y_sample, _, _, _ = _trunk(x_sample, mods[:, :nb_lat], cache_k, cache_v, state_ret, wts, BF16)
    return (y_prompt, y_sample, new_k, new_v, new_s)
```

```python
import functools
import math

import numpy as np
import jax
import jax.numpy as jnp
from jax import lax
from jax.experimental import pallas as pl
from jax.experimental.pallas import tpu as pltpu

F32 = jnp.float32
BF16 = jnp.bfloat16

D_MODEL = 2048
DEPTH = 2
GRID_W = 64
EPS = 1e-6
NEG_INF = -1e30
LOG2_E = math.log2(math.e)
FNET_GROUPS = 8
FNET_GROUP_DIM = 128
FNET_WIDTH = 1024
NA_HEADS = 8
NA_HEAD_DIM = 128
NA_WIDTH = 1024
NA_KH = 8
NA_KW = 16
EVEN_IN = FNET_WIDTH + 3 * NA_WIDTH
RET_HEADS = 8
RET_QK_DIM = 256
RET_V_DIM = 512
RET_QK_WIDTH = RET_HEADS * RET_QK_DIM
RET_V_WIDTH = RET_HEADS * RET_V_DIM
ODD_IN = 2 * RET_QK_WIDTH + 3 * RET_V_WIDTH
RET_BLOCK = 256
RET_HEADS_PER_STEP = 4
ROPE_BASE = 10000.0
D_FF = 5632
MOD_ROWS = 8
MOD_CHUNK = 128

V7X_VMEM_BUDGET = 56 * 1024 * 1024
VMEM_SLACK = 8 * 1024 * 1024


def _params(semantics, vmem_bytes):
    return pltpu.CompilerParams(dimension_semantics=semantics,
                                vmem_limit_bytes=min(int(vmem_bytes), V7X_VMEM_BUDGET))


def _silu(x):
    return x * (1.0 / (1.0 + jnp.exp(-x)))


def _adaln_kernel(c_ref, w_ref, b_ref, o_ref):
    s = _silu(c_ref[...]).astype(BF16)
    w = w_ref[0].astype(BF16)
    o_ref[0] = jnp.dot(s, w, preferred_element_type=F32) + b_ref[0]


def adaln_all(cond, ada_w, ada_b):
    n = ada_w.shape[-1]
    tn = 1024
    return pl.pallas_call(
        _adaln_kernel,
        out_shape=jax.ShapeDtypeStruct((DEPTH, MOD_ROWS, n), F32),
        grid=(DEPTH, n // tn),
        in_specs=[
            pl.BlockSpec((MOD_ROWS, D_MODEL), lambda l, j: (0, 0)),
            pl.BlockSpec((1, D_MODEL, tn), lambda l, j: (l, 0, j)),
            pl.BlockSpec((1, 1, tn), lambda l, j: (l, 0, j)),
        ],
        out_specs=pl.BlockSpec((1, MOD_ROWS, tn), lambda l, j: (l, 0, j)),
        compiler_params=_params(("arbitrary", "arbitrary"), 40 << 20),
        name="adaln",
    )(cond, ada_w, ada_b.reshape(DEPTH, 1, n))


def _modulate_into(h_ref, x_ref, g_ref, sh_ref, sc_ref):
    shift = sh_ref[0]
    gain = g_ref[...] * (1.0 + sc_ref[0])

    def body(c, carry):
        rows = pl.ds(pl.multiple_of(c * MOD_CHUNK, MOD_CHUNK), MOD_CHUNK)
        x = x_ref[rows, :]
        ms = jnp.mean(x * x, axis=-1, keepdims=True)
        h_ref[rows, :] = (x * lax.rsqrt(ms + EPS) * gain + shift).astype(h_ref.dtype)
        return carry

    lax.fori_loop(0, x_ref.shape[0] // MOD_CHUNK, body, 0, unroll=2)


ROPE_QUARTER = RET_QK_DIM // 4


def _pair_order(x, axis):
    q = ROPE_QUARTER
    parts = [lax.slice_in_dim(x, a * q, (a + 1) * q, axis=axis) for a in (0, 2, 1, 3)]
    return jnp.concatenate(parts, axis=axis)


def _pair_order_rows(s):
    return _pair_order(s, 0)


def _odd_w_prep_kernel(w_ref, o_ref):
    for head in range(2 * RET_HEADS):
        cs = slice(head * RET_QK_DIM, (head + 1) * RET_QK_DIM)
        o_ref[0, :, cs] = _pair_order(w_ref[0, :, cs], 1).astype(o_ref.dtype)
    rest = 2 * RET_QK_WIDTH
    o_ref[0, :, rest:] = w_ref[0, :, rest:].astype(o_ref.dtype)


def odd_w_prep(w):
    n, d, width = w.shape
    tr = 128
    return pl.pallas_call(
        _odd_w_prep_kernel,
        out_shape=jax.ShapeDtypeStruct(w.shape, BF16),
        grid=(n, d // tr),
        in_specs=[pl.BlockSpec((1, tr, width), lambda l, i: (l, i, 0))],
        out_specs=pl.BlockSpec((1, tr, width), lambda l, i: (l, i, 0)),
        compiler_params=_params(("arbitrary", "arbitrary"), 2 * tr * width * 6 + 3 * tr * width * 4 + VMEM_SLACK),
        name="odd_w_prep",
    )(w)


def _rope_tables(t_len):
    inv = ROPE_BASE ** (-jnp.arange(ROPE_QUARTER, dtype=F32) / ROPE_QUARTER)
    t = jnp.arange(t_len)
    ang = jnp.concatenate([(t // GRID_W).astype(F32)[:, None] * inv[None, :],
                           (t % GRID_W).astype(F32)[:, None] * inv[None, :]], axis=-1)
    return jnp.cos(ang), jnp.sin(ang)


def _rope(x, cos, sin):
    half = RET_QK_DIM // 2
    x1, x2 = x[:, :half], x[:, half:]
    return jnp.concatenate([x1 * cos - x2 * sin, x1 * sin + x2 * cos], axis=-1)


def _modproj_kernel(*refs, rope_tiles):
    if rope_tiles:
        x_ref, sh_ref, sc_ref, g_ref, w_ref, cos_ref, sin_ref, o_ref, h_ref = refs
    else:
        x_ref, sh_ref, sc_ref, g_ref, w_ref, o_ref, h_ref = refs
    j = pl.program_id(1)

    @pl.when(j == 0)
    def _():
        _modulate_into(h_ref, x_ref, g_ref, sh_ref, sc_ref)

    acc = jnp.dot(h_ref[...], w_ref[...], preferred_element_type=F32)
    if not rope_tiles:
        o_ref[...] = acc.astype(o_ref.dtype)
        return

    @pl.when(j < rope_tiles)
    def _():
        cos = cos_ref[...]
        sin = sin_ref[...]
        for s in range(acc.shape[1] // RET_QK_DIM):
            cs = slice(s * RET_QK_DIM, (s + 1) * RET_QK_DIM)
            o_ref[:, cs] = _rope(acc[:, cs], cos, sin).astype(o_ref.dtype)

    @pl.when(j >= rope_tiles)
    def _():
        o_ref[...] = acc.astype(o_ref.dtype)


def modproj(x, shift, scale, g, w, rows_per_mod, out_dtype, tm, tn, rope=None):
    m, d = x.shape
    n = w.shape[1]
    osz = jnp.dtype(out_dtype).itemsize
    vmem = 2 * tm * d * 4 + 2 * d * tn * 2 + 2 * tm * tn * osz + tm * d * 2 + 2 * tm * tn * 4 + VMEM_SLACK
    mod_idx = lambda i, j: ((i * tm) // rows_per_mod, 0, 0)
    in_specs = [
        pl.BlockSpec((tm, d), lambda i, j: (i, 0)),
        pl.BlockSpec((1, 1, d), mod_idx),
        pl.BlockSpec((1, 1, d), mod_idx),
        pl.BlockSpec((1, d), lambda i, j: (0, 0)),
        pl.BlockSpec((d, tn), lambda i, j: (0, j)),
    ]
    args = [x, shift, scale, g, w]
    rope_tiles = 0
    if rope is not None:
        seq_len, n_cols = rope
        rope_tiles = n_cols // tn
        cos, sin = _rope_tables(seq_len)
        tab = pl.BlockSpec((tm, RET_QK_DIM // 2), lambda i, j: (i % (seq_len // tm), 0))
        in_specs += [tab, tab]
        args += [cos, sin]
        vmem += 2 * 2 * tm * (RET_QK_DIM // 2) * 4
    return pl.pallas_call(
        functools.partial(_modproj_kernel, rope_tiles=rope_tiles),
        out_shape=jax.ShapeDtypeStruct((m, n), out_dtype),
        grid=(m // tm, n // tn),
        in_specs=in_specs,
        out_specs=pl.BlockSpec((tm, tn), lambda i, j: (i, j)),
        scratch_shapes=[pltpu.VMEM((tm, d), BF16)],
        compiler_params=_params(("arbitrary", "arbitrary"), vmem),
        name="modproj",
    )(*args)


def _head_rms(x, g):
    return x * lax.rsqrt(jnp.mean(x * x, axis=-1, keepdims=True) + EPS) * g


def _ctx_attn_kernel(q_ref, k_ref, v_ref, qg_ref, kg_ref, o_ref, nk_ref, nv_ref):
    scale = NA_HEAD_DIM ** -0.5
    for h in range(NA_HEADS):
        cs = slice(h * NA_HEAD_DIM, (h + 1) * NA_HEAD_DIM)
        q = _head_rms(q_ref[0, :, cs].astype(F32), qg_ref[...])
        k = _head_rms(k_ref[0, :, cs].astype(F32), kg_ref[...])
        v = v_ref[0, :, cs].astype(F32)
        nk_ref[0, 0, h] = k
        nv_ref[0, 0, h] = v
        s = lax.dot_general(q.astype(BF16), k.astype(BF16), (((1,), (1,)), ((), ())),
                            preferred_element_type=F32) * scale
        m = jnp.max(s, axis=-1, keepdims=True)
        e = jnp.exp(s - m)
        l = jnp.sum(e, axis=-1, keepdims=True)
        o = jnp.dot(e.astype(BF16), v.astype(BF16), preferred_element_type=F32) / l
        o_ref[0, :, cs] = o.astype(o_ref.dtype)


def ctx_attention(p, q_gain, k_gain):
    b, t, _ = p.shape
    cache_shape = (b, 1, NA_HEADS, t, NA_HEAD_DIM)
    blk = lambda c: pl.BlockSpec((1, t, NA_WIDTH), lambda i, c=c: (i, 0, c))
    gain = pl.BlockSpec((1, NA_HEAD_DIM), lambda i: (0, 0))
    cache_spec = pl.BlockSpec((1, 1, NA_HEADS, t, NA_HEAD_DIM), lambda i: (i, 0, 0, 0, 0))
    return pl.pallas_call(
        _ctx_attn_kernel,
        out_shape=(jax.ShapeDtypeStruct((b, t, NA_WIDTH), BF16),
                   jax.ShapeDtypeStruct(cache_shape, F32),
                   jax.ShapeDtypeStruct(cache_shape, F32)),
        grid=(b,),
        in_specs=[blk(1), blk(2), blk(3), gain, gain],
        out_specs=(pl.BlockSpec((1, t, NA_WIDTH), lambda i: (i, 0, 0)), cache_spec, cache_spec),
        compiler_params=_params(("arbitrary",), 32 << 20),
        name="ctx_attention",
    )(p, p, p, q_gain, k_gain)


NA_BIAS_ROWS = 2 * NA_KH - 1
NA_BIAS_COLS = 2 * NA_KW - 1
NA_MASK_TILE = NA_BIAS_ROWS
NA_QROWS = 4
NA_KROWS = NA_KH + NA_QROWS


def _na_build_bias(rb_ref, bias_ref, head):
    shape = (GRID_W, 2 * GRID_W)
    lane = lax.broadcasted_iota(jnp.int32, shape, 1)
    qc = lax.broadcasted_iota(jnp.int32, shape, 0)
    kc = lane & (GRID_W - 1)
    start = jnp.clip(qc - NA_KW // 2, 0, GRID_W - NA_KW)
    ok = (kc >= start) & (kc < start + NA_KW)
    delta = kc - qc + (NA_KW - 1)
    left = lane < GRID_W
    base = head * (NA_BIAS_ROWS * NA_BIAS_COLS)

    def build(dr, carry):
        acc = jnp.full(shape, NEG_INF, F32)
        for dc in range(NA_BIAS_COLS):
            acc = jnp.where(delta == dc, rb_ref[base + dr * NA_BIAS_COLS + dc], acc)
        tile = jnp.where(ok, acc * LOG2_E, NEG_INF)
        bias_ref[0, dr] = jnp.where(left, tile, 0.0)
        bias_ref[1, dr] = jnp.where(left, 0.0, tile)
        return carry

    lax.fori_loop(0, NA_BIAS_ROWS, build, 0)
    bias_ref[0, NA_MASK_TILE] = jnp.where(left, NEG_INF, 0.0)
    bias_ref[1, NA_MASK_TILE] = jnp.where(left, 0.0, NEG_INF)


def _na_kernel(rb_ref, q_ref, k_ref, v_ref, kc_ref, vc_ref, qg_ref, kg_ref, o_ref,
               qs_ref, ks_ref, vs_ref, bias_ref, *, rows):
    @pl.when(pl.program_id(1) == 0)
    def _():
        _na_build_bias(rb_ref, bias_ref, pl.program_id(0))

    scale = NA_HEAD_DIM ** -0.5 * LOG2_E
    qs_ref[...] = (_head_rms(q_ref[0].astype(F32), qg_ref[...]) * scale).astype(BF16)
    ks_ref[...] = _head_rms(k_ref[0].astype(F32), kg_ref[...]).astype(BF16)
    vs_ref[...] = v_ref[0].astype(BF16)
    kc = kc_ref[0, 0, 0].astype(BF16)
    vc = vc_ref[0, 0, 0].astype(BF16)
    nt = (((1,), (1,)), ((), ()))
    nq = NA_QROWS * GRID_W
    nk = NA_KROWS * GRID_W

    def tile_index(r, rs, kr):
        inside = (kr >= rs) & (kr < rs + NA_KH)
        return jnp.where(inside, kr - r + (NA_KH - 1), NA_MASK_TILE)

    def body(blk, carry):
        r0 = blk * NA_QROWS
        k0 = jnp.clip(r0 - NA_KH // 2, 0, rows - NA_KROWS)
        q = qs_ref[pl.ds(pl.multiple_of(r0 * GRID_W, nq), nq), :]
        kw = ks_ref[pl.ds(pl.multiple_of(k0 * GRID_W, GRID_W), nk), :]
        vw = vs_ref[pl.ds(pl.multiple_of(k0 * GRID_W, GRID_W), nk), :]
        bias_rows = []
        for i in range(NA_QROWS):
            r = r0 + i
            rs = jnp.clip(r - NA_KH // 2, 0, rows - NA_KH)
            pairs = [bias_ref[0, tile_index(r, rs, k0 + 2 * jp)]
                     + bias_ref[1, tile_index(r, rs, k0 + 2 * jp + 1)]
                     for jp in range(NA_KROWS // 2)]
            bias_rows.append(jnp.concatenate(pairs, axis=-1))
        bias = jnp.concatenate(bias_rows, axis=0)
        s_w = lax.dot_general(q, kw, nt, preferred_element_type=F32) + bias
        s_c = lax.dot_general(q, kc, nt, preferred_element_type=F32)
        m = jnp.maximum(jnp.max(s_w, axis=-1, keepdims=True), jnp.max(s_c, axis=-1, keepdims=True))
        e_w = jnp.exp2(s_w - m)
        e_c = jnp.exp2(s_c - m)
        l = jnp.sum(e_w, axis=-1, keepdims=True) + jnp.sum(e_c, axis=-1, keepdims=True)
        o = (jnp.dot(e_w.astype(BF16), vw, preferred_element_type=F32)
             + jnp.dot(e_c.astype(BF16), vc, preferred_element_type=F32)) / l
        o_ref[0, pl.ds(pl.multiple_of(r0 * GRID_W, nq), nq), :] = o.astype(o_ref.dtype)
        return carry

    lax.fori_loop(0, rows // NA_QROWS, body, 0, unroll=2)


def na_attention(p, cache_k, cache_v, layer_j, rel_bias, q_gain, k_gain):
    b, t, _ = p.shape
    past = cache_k.shape[3]
    rows = t // GRID_W
    hd = NA_HEAD_DIM
    col0 = FNET_WIDTH // hd
    blk = lambda c: pl.BlockSpec((1, t, hd), lambda h, i, rb, c=c: (i, 0, col0 + c * NA_HEADS + h))
    cache_spec = pl.BlockSpec((1, 1, 1, past, hd), lambda h, i, rb: (i, layer_j, h, 0, 0))
    gain = pl.BlockSpec((1, hd), lambda h, i, rb: (0, 0))
    return pl.pallas_call(
        functools.partial(_na_kernel, rows=rows),
        out_shape=jax.ShapeDtypeStruct((b, t, NA_WIDTH), BF16),
        grid_spec=pltpu.PrefetchScalarGridSpec(
            num_scalar_prefetch=1,
            grid=(NA_HEADS, b),
            in_specs=[blk(0), blk(1), blk(2), cache_spec, cache_spec, gain, gain],
            out_specs=pl.BlockSpec((1, t, hd), lambda h, i, rb: (i, 0, h)),
            scratch_shapes=[pltpu.VMEM((t, hd), BF16)] * 3
            + [pltpu.VMEM((2, NA_BIAS_ROWS + 1, GRID_W, 2 * GRID_W), F32)],
        ),
        compiler_params=_params(("arbitrary", "arbitrary"), 40 << 20),
        name="na_attention",
    )(rel_bias.astype(F32).reshape(-1), p, p, p, cache_k, cache_v, q_gain, k_gain)


FNET_BLOCK = 16


def _fnet_tables(n1, n2):
    t_len = n1 * n2
    tb = FNET_BLOCK
    eye = np.eye(tb)
    k1 = np.arange(n1)
    ang1 = (2.0 * np.pi / n1) * ((k1[:, None] * k1[None, :]) % n1)
    f1 = np.concatenate([np.cos(ang1), -np.sin(ang1)], axis=0) / math.sqrt(t_len)
    rows_mat = np.kron(f1, eye)
    t2 = np.arange(n2)
    tw = (2.0 * np.pi / t_len) * (k1[:, None] * t2[None, :])
    tw = tw.reshape(n1, n2 // tb, tb).transpose(1, 0, 2).reshape(n2 // tb, n1 * tb, 1)
    c = np.arange(FNET_GROUP_DIM)
    ang3 = (2.0 * np.pi / FNET_GROUP_DIM) * ((c[:, None] * c[None, :]) % FNET_GROUP_DIM)
    c3, s3 = np.cos(ang3), np.sin(ang3)
    chan = np.block([[c3, -s3], [s3, c3]]) / math.sqrt(FNET_GROUP_DIM)
    ang2 = (2.0 * np.pi / n2) * ((t2[:, None] * t2[None, :]) % n2)
    cols_re = np.einsum("pq,kt->kpqt", eye, np.cos(ang2)).reshape(n2 * tb, tb * n2)
    cols_im = np.einsum("pq,kt->kpqt", eye, np.sin(ang2)).reshape(n2 * tb, tb * n2)
    f32 = lambda x: jnp.asarray(x, dtype=F32)
    bf = lambda x: f32(x).astype(BF16)
    return bf(rows_mat), f32(np.cos(tw)), f32(np.sin(tw)), bf(chan), bf(cols_re), bf(cols_im)


def _fnet_rows_kernel(x_ref, mat_ref, twc_ref, tws_ref, chan_ref, vr_ref, vi_ref):
    _, n1, tb, width = x_ref.shape
    rows = n1 * tb
    gd = FNET_GROUP_DIM
    x = x_ref[0].reshape(rows, width).astype(BF16)
    h = jnp.dot(mat_ref[...], x, preferred_element_type=F32)
    hr, hi = h[:rows], h[rows:]
    c, s = twc_ref[0], tws_ref[0]
    gr = (hr * c + hi * s).astype(BF16)
    gi = (hi * c - hr * s).astype(BF16)
    stacked = jnp.concatenate(
        [jnp.concatenate([gr[:, g * gd:(g + 1) * gd], gi[:, g * gd:(g + 1) * gd]], axis=1)
         for g in range(FNET_GROUPS)], axis=0)
    v = jnp.dot(stacked, chan_ref[...], preferred_element_type=F32)
    for g in range(FNET_GROUPS):
        part = v[g * rows:(g + 1) * rows]
        vr_ref[0, :, :, g * gd:(g + 1) * gd] = part[:, :gd].reshape(n1, tb, gd).astype(vr_ref.dtype)
        vi_ref[0, :, :, g * gd:(g + 1) * gd] = part[:, gd:].reshape(n1, tb, gd).astype(vi_ref.dtype)


def _fnet_cols_kernel(vr_ref, vi_ref, re_ref, im_ref, o_ref):
    _, kb, n2, width = vr_ref.shape
    vr = vr_ref[0].reshape(kb * n2, width)
    vi = vi_ref[0].reshape(kb * n2, width)
    y = (jnp.dot(re_ref[...], vr, preferred_element_type=F32)
         + jnp.dot(im_ref[...], vi, preferred_element_type=F32))
    o_ref[0] = y.reshape(n2, kb, width).astype(o_ref.dtype)


def fourier_mix(p, n1, n2):
    b, t, width = p.shape
    tb = FNET_BLOCK
    w = FNET_WIDTH
    rows_mat, twc, tws, chan, cols_re, cols_im = _fnet_tables(n1, n2)
    const = lambda shape: pl.BlockSpec(shape, lambda i, s: (0,) * len(shape))
    tw_spec = pl.BlockSpec((1, n1 * tb, 1), lambda i, s: (s, 0, 0))
    v_shape = jax.ShapeDtypeStruct((b, n1, n2, w), BF16)
    v_spec = pl.BlockSpec((1, n1, tb, w), lambda i, s: (i, 0, s, 0))
    isz = p.dtype.itemsize
    vmem = (2 * n1 * tb * w * isz + 2 * rows_mat.size * 2 + 4 * n1 * tb * w * 2
            + 6 * n1 * tb * w * 4 + VMEM_SLACK)
    vr, vi = pl.pallas_call(
        _fnet_rows_kernel,
        out_shape=(v_shape, v_shape),
        grid=(b, n2 // tb),
        in_specs=[pl.BlockSpec((1, n1, tb, w), lambda i, s: (i, 0, s, 0)),
                  const(rows_mat.shape), tw_spec, tw_spec, const(chan.shape)],
        out_specs=(v_spec, v_spec),
        compiler_params=_params(("arbitrary", "arbitrary"), vmem),
        name="fnet_rows",
    )(p.reshape(b, n1, n2, width), rows_mat, twc, tws, chan)
    v_in = pl.BlockSpec((1, tb, n2, w), lambda i, s: (i, s, 0, 0))
    vmem = 2 * 2 * tb * n2 * w * 2 + 2 * 2 * cols_re.size * 2 + 2 * n2 * tb * w * 2 + 3 * n2 * tb * w * 4 + VMEM_SLACK
    out = pl.pallas_call(
        _fnet_cols_kernel,
        out_shape=jax.ShapeDtypeStruct((b, n2, n1, w), BF16),
        grid=(b, n1 // tb),
        in_specs=[v_in, v_in, const(cols_re.shape), const(cols_im.shape)],
        out_specs=pl.BlockSpec((1, n2, tb, w), lambda i, s: (i, 0, s, 0)),
        compiler_params=_params(("arbitrary", "arbitrary"), vmem),
        name="fnet_cols",
    )(vr, vi, cols_re, cols_im)
    return out.reshape(b, t, w)


def _even_out_kernel(f_ref, a_ref, wf_ref, wa_ref, x_ref, gate_ref, o_ref):
    y = (jnp.dot(f_ref[...], wf_ref[...], preferred_element_type=F32)
         + jnp.dot(a_ref[...], wa_ref[...], preferred_element_type=F32))
    o_ref[...] = x_ref[...] + gate_ref[0] * y


def even_out(f, a, w, x, gate, rows_per_mod, tm, tn):
    m, d = x.shape
    kf = f.shape[1]
    vmem = 2 * 2 * tm * kf * 2 + 2 * 2 * kf * tn * 2 + 4 * tm * tn * 4 + 2 * tm * tn * 4 + VMEM_SLACK
    return pl.pallas_call(
        _even_out_kernel,
        out_shape=jax.ShapeDtypeStruct((m, d), F32),
        grid=(d // tn, m // tm),
        in_specs=[
            pl.BlockSpec((tm, kf), lambda j, i: (i, 0)),
            pl.BlockSpec((tm, kf), lambda j, i: (i, 0)),
            pl.BlockSpec((kf, tn), lambda j, i: (0, j)),
            pl.BlockSpec((kf, tn), lambda j, i: (1, j)),
            pl.BlockSpec((tm, tn), lambda j, i: (i, j)),
            pl.BlockSpec((1, 1, tn), lambda j, i: ((i * tm) // rows_per_mod, 0, j)),
        ],
        out_specs=pl.BlockSpec((tm, tn), lambda j, i: (i, j)),
        compiler_params=_params(("arbitrary", "arbitrary"), vmem),
        name="even_out",
    )(f, a, w, w, x, gate)


def _gated_group_norm(o, g):
    o = o.astype(F32)
    oc = o - jnp.mean(o, axis=-1, keepdims=True)
    gn = oc * lax.rsqrt(jnp.mean(oc * oc, axis=-1, keepdims=True) + EPS)
    hg = 0.5 * g.astype(F32)
    return (hg + hg * jnp.tanh(hg)) * gn


def _odd_out_kernel(of_ref, ob_ref, gf_ref, gb_ref, w_ref, x_ref, gate_ref, o_ref):
    acc = None
    for h in range(RET_HEADS):
        cs = slice(h * RET_V_DIM, (h + 1) * RET_V_DIM)
        y = (_gated_group_norm(of_ref[:, cs], gf_ref[:, cs])
             + _gated_group_norm(ob_ref[:, cs], gb_ref[:, cs])).astype(BF16)
        part = jnp.dot(y, w_ref[cs, :], preferred_element_type=F32)
        acc = part if acc is None else acc + part
    o_ref[...] = x_ref[...] + gate_ref[0] * acc


def odd_out(o_f, o_b, p, w, x, gate, rows_per_mod, tm):
    m, d = x.shape
    k = w.shape[0]
    gcol = (2 * RET_QK_WIDTH + RET_V_WIDTH) // RET_V_WIDTH
    vmem = (2 * 4 * tm * k * 2 + k * d * 2 + 4 * tm * d * 4 + 2 * tm * d * 4
            + 6 * tm * RET_V_DIM * 4 + VMEM_SLACK)
    act = lambda c: pl.BlockSpec((tm, k), lambda i, c=c: (i, c))
    return pl.pallas_call(
        _odd_out_kernel,
        out_shape=jax.ShapeDtypeStruct((m, d), F32),
        grid=(m // tm,),
        in_specs=[
            act(0), act(0), act(gcol), act(gcol + 1),
            pl.BlockSpec((k, d), lambda i: (0, 0), pipeline_mode=pl.Buffered(1)),
            pl.BlockSpec((tm, d), lambda i: (i, 0)),
            pl.BlockSpec((1, 1, d), lambda i: ((i * tm) // rows_per_mod, 0, 0)),
        ],
        out_specs=pl.BlockSpec((tm, d), lambda i: (i, 0)),
        compiler_params=_params(("arbitrary",), vmem),
        name="odd_out",
    )(o_f, o_b, p, p, w, x, gate)


def _ffn_kernel(x_ref, sh_ref, sc_ref, gate_ref, g_ref, wg_ref, wu_ref, wd_ref, o_ref, h_ref):
    j = pl.program_id(1)

    @pl.when(j == 0)
    def _():
        _modulate_into(h_ref, x_ref, g_ref, sh_ref, sc_ref)
        o_ref[...] = x_ref[...]

    h = h_ref[...]
    a = jnp.dot(h, wg_ref[0], preferred_element_type=F32)
    u = jnp.dot(h, wu_ref[0], preferred_element_type=F32)
    act = (_silu(a) * u).astype(BF16)
    gate = gate_ref[0]
    half = o_ref.shape[1] // 2
    for c in range(2):
        cs = slice(c * half, (c + 1) * half)
        o_ref[:, cs] += gate[:, cs] * jnp.dot(act, wd_ref[0, :, cs], preferred_element_type=F32)


def ffn_block(x, shift, scale, gate, g, w_gate, w_up, w_down, layer, rows_per_mod, tm, tf):
    m, d = x.shape
    f = w_gate.shape[2]
    vmem = 3 * tm * d * 4 + tm * d * 2 + 2 * 3 * d * tf * 2 + 3 * tm * tf * 4 + tm * d * 4 + VMEM_SLACK
    mod_idx = lambda i, j: ((i * tm) // rows_per_mod, 0, 0)
    mod = pl.BlockSpec((1, 1, d), mod_idx)
    return pl.pallas_call(
        _ffn_kernel,
        out_shape=jax.ShapeDtypeStruct((m, d), F32),
        grid=(m // tm, f // tf),
        in_specs=[
            pl.BlockSpec((tm, d), lambda i, j: (i, 0), pipeline_mode=pl.Buffered(1)),
            mod, mod, mod,
            pl.BlockSpec((1, d), lambda i, j: (0, 0)),
            pl.BlockSpec((1, d, tf), lambda i, j: (layer, 0, j)),
            pl.BlockSpec((1, d, tf), lambda i, j: (layer, 0, j)),
            pl.BlockSpec((1, tf, d), lambda i, j: (layer, j, 0)),
        ],
        out_specs=pl.BlockSpec((tm, d), lambda i, j: (i, 0)),
        scratch_shapes=[pltpu.VMEM((tm, d), BF16)],
        compiler_params=_params(("arbitrary", "arbitrary"), vmem),
        name="ffn_block",
    )(x, shift, scale, gate, g, w_gate, w_up, w_down)


def _retention_kernel(lg_ref, *refs, latent, n_blocks):
    if latent:
        (qf_ref, kf_ref, vf_ref, qb_ref, kb_ref, vb_ref, s0_ref,
         of_ref, ob_ref, s_ref, dec_ref, qd_ref, kd_ref) = refs
    else:
        (qf_ref, kf_ref, vf_ref, qb_ref, kb_ref, vb_ref,
         of_ref, ob_ref, sfin_ref, s_ref, dec_ref, qd_ref, kd_ref) = refs
    h0 = pl.program_id(0) * RET_HEADS_PER_STEP
    i = pl.program_id(2)
    c = RET_BLOCK
    dk, dv = RET_QK_DIM, RET_V_DIM
    scale = RET_QK_DIM ** -0.5
    scans = [(d, hh) for d in range(2) for hh in range(RET_HEADS_PER_STEP)]

    @pl.when((pl.program_id(1) == 0) & (i == 0))
    def _():
        row = lax.broadcasted_iota(jnp.int32, (c, c), 0).astype(F32)
        col = lax.broadcasted_iota(jnp.int32, (c, c), 1).astype(F32)
        t_idx = lax.broadcasted_iota(jnp.int32, (c, 1), 0).astype(F32)
        for d, hh in scans:
            lg = lg_ref[d, h0 + hh]
            diff = row - col if d == 0 else col - row
            dec_ref[d, hh] = jnp.where(diff >= 0, jnp.exp(lg * jnp.maximum(diff, 0.0)), 0.0) * scale
            pos = t_idx if d == 0 else (c - 1.0) - t_idx
            qd_ref[d, hh] = jnp.exp(lg * (pos + 1.0))
            kd_ref[d, hh] = jnp.exp(lg * ((c - 1.0) - pos)) * scale

    @pl.when(i == 0)
    def _():
        for d, hh in scans:
            s_ref[d, hh] = (_pair_order_rows(s0_ref[0, 0, d, hh]) if latent
                            else jnp.zeros((dk, dv), F32))

    nt = (((1,), (1,)), ((), ()))
    tn = (((0,), (0,)), ((), ()))
    chunk_refs = ((qf_ref, kf_ref, vf_ref, of_ref), (qb_ref, kb_ref, vb_ref, ob_ref))
    for d, hh in scans:
        q_ref, k_ref, v_ref, o_ref = chunk_refs[d]
        q = q_ref[0, :, hh * dk:(hh + 1) * dk]
        k = k_ref[0, :, hh * dk:(hh + 1) * dk]
        v = v_ref[0, :, hh * dv:(hh + 1) * dv]
        a = lax.dot_general(q, k, nt, preferred_element_type=F32)
        inner_w = (a * dec_ref[d, hh]).astype(BF16)
        q_dec = (q.astype(F32) * qd_ref[d, hh]).astype(BF16)
        k_dec = (k.astype(F32) * kd_ref[d, hh]).astype(BF16)
        s_old = s_ref[d, hh]
        o = (jnp.dot(inner_w, v, preferred_element_type=F32)
             + jnp.dot(q_dec, s_old.astype(BF16), preferred_element_type=F32))
        s_new = (s_old * jnp.exp(lg_ref[d, h0 + hh] * c)
                 + lax.dot_general(k_dec, v, tn, preferred_element_type=F32))
        s_ref[d, hh] = s_new
        o_ref[0, :, hh * dv:(hh + 1) * dv] = o.astype(o_ref.dtype)
        if not latent:
            @pl.when(i == n_blocks - 1)
            def _(d=d, hh=hh, s_new=s_new):
                sfin_ref[0, 0, d, hh] = _pair_order_rows(s_new)


def retention(p, log_gamma, state0, layer_j):
    b, t, _ = p.shape
    latent = state0 is not None
    c = RET_BLOCK
    nb = t // c
    hps = RET_HEADS_PER_STEP
    dk, dv, nh = hps * RET_QK_DIM, hps * RET_V_DIM, RET_HEADS // hps

    def chunk_specs(blk_of):
        return [
            pl.BlockSpec((1, c, dk), lambda h, bi, i, lg: (bi, blk_of(i), h)),
            pl.BlockSpec((1, c, dk), lambda h, bi, i, lg: (bi, blk_of(i), nh + h)),
            pl.BlockSpec((1, c, dv), lambda h, bi, i, lg: (bi, blk_of(i), nh + h)),
        ]

    fwd = lambda i: i
    bwd = lambda i: nb - 1 - i
    in_specs = chunk_specs(fwd) + chunk_specs(bwd)
    args = [p] * 6
    o_shape = jax.ShapeDtypeStruct((b, t, RET_V_WIDTH), BF16)
    out_shape = [o_shape, o_shape]
    out_specs = [pl.BlockSpec((1, c, dv), lambda h, bi, i, lg: (bi, fwd(i), h)),
                 pl.BlockSpec((1, c, dv), lambda h, bi, i, lg: (bi, bwd(i), h))]
    state_dims = (RET_QK_DIM, RET_V_DIM)
    state_block = (1, 1, 2, hps) + state_dims
    if latent:
        in_specs.append(pl.BlockSpec(state_block, lambda h, bi, i, lg: (bi, layer_j, 0, h, 0, 0)))
        args.append(state0)
    else:
        out_shape.append(jax.ShapeDtypeStruct((b, 1, 2, RET_HEADS) + state_dims, F32))
        out_specs.append(pl.BlockSpec(state_block, lambda h, bi, i, lg: (bi, 0, 0, h, 0, 0)))
    return pl.pallas_call(
        functools.partial(_retention_kernel, latent=latent, n_blocks=nb),
        out_shape=tuple(out_shape),
        grid_spec=pltpu.PrefetchScalarGridSpec(
            num_scalar_prefetch=1,
            grid=(nh, b, nb),
            in_specs=in_specs,
            out_specs=tuple(out_specs),
            scratch_shapes=[pltpu.VMEM((2, hps) + state_dims, F32), pltpu.VMEM((2, hps, c, c), F32),
                            pltpu.VMEM((2, hps, c, 1), F32), pltpu.VMEM((2, hps, c, 1), F32)],
        ),
        compiler_params=_params(("arbitrary",) * 3, 40 << 20),
        name="retention",
    )(log_gamma, *args)


def _trunk(x, mods, ctx_k, ctx_v, ctx_state, wts, p_dtype):
    b, t, d = x.shape
    is_ctx = ctx_k is None
    nbm = mods.shape[1]
    rows_per_mod = (b * t) // nbm
    x2 = x.reshape(b * t, d)
    new_k = new_v = new_s = None
    for i in range(DEPTH):
        mod = [mods[i, :, k][:, None, :] for k in range(6)]
        j = i // 2
        if i % 2 == 0:
            p = modproj(x2, mod[0], mod[1], wts["norm_g"][i, 0][None], wts["even_w_in"][j],
                        rows_per_mod, p_dtype, tm=1024, tn=1024).reshape(b, t, EVEN_IN)
            qg, kg = wts["even_q_norm"][j][None], wts["even_k_norm"][j][None]
            if is_ctx:
                attn, new_k, new_v = ctx_attention(p, qg, kg)
            else:
                attn = na_attention(p, ctx_k, ctx_v, j, wts["na_rel_bias"][j], qg, kg)
            n2 = GRID_W if not is_ctx else math.isqrt(t)
            fm = fourier_mix(p, t // n2, n2)
            x2 = even_out(fm.reshape(b * t, FNET_WIDTH), attn.reshape(b * t, NA_WIDTH),
                          wts["even_w_out"][j], x2, mod[2], rows_per_mod, tm=1024, tn=1024)
        else:
            rope = (t, 2 * RET_QK_WIDTH) if not is_ctx else None
            p = modproj(x2, mod[0], mod[1], wts["norm_g"][i, 0][None], wts["odd_w_in"][j],
                        rows_per_mod, BF16, tm=1024, tn=1024, rope=rope)
            lg = jax.nn.log_sigmoid(wts["ret_decay_logit"][j].astype(F32))
            if is_ctx:
                o_f, o_b, new_s = retention(p.reshape(b, t, ODD_IN), lg, None, j)
            else:
                o_f, o_b = retention(p.reshape(b, t, ODD_IN), lg, ctx_state, j)
            x2 = odd_out(o_f.reshape(b * t, RET_V_WIDTH), o_b.reshape(b * t, RET_V_WIDTH), p,
                         wts["odd_w_out"][j], x2, mod[2], rows_per_mod, tm=256)
        x2 = ffn_block(x2, mod[3], mod[4], mod[5], wts["norm_g"][i, 1][None], wts["ffn_w_gate"],
                       wts["ffn_w_up"], wts["ffn_w_down"], i, rows_per_mod, tm=1024, tf=512)
    return x2.reshape(b, t, d), new_k, new_v, new_s


def kernel(x_prompt, x_sample, cache_k, cache_v, state_ret, c, c_ctx, ada_w, ada_b, norm_g,
           even_w_in, even_q_norm, even_k_norm, na_rel_bias, even_w_out, odd_w_in, ret_decay_logit,
           odd_w_out, ffn_w_gate, ffn_w_up, ffn_w_down):
    nb_lat = c.shape[0]
    cond = jnp.concatenate(
        [c, c_ctx[None, :], jnp.zeros((MOD_ROWS - nb_lat - 1, D_MODEL), F32)], axis=0)
    mods = adaln_all(cond, ada_w, ada_b).reshape(DEPTH, MOD_ROWS, 6, D_MODEL)
    wts = dict(
        norm_g=norm_g, even_q_norm=even_q_norm, even_k_norm=even_k_norm, na_rel_bias=na_rel_bias,
        ret_decay_logit=ret_decay_logit,
        even_w_in=even_w_in.astype(BF16), even_w_out=even_w_out.astype(BF16),
        odd_w_in=odd_w_prep(odd_w_in), odd_w_out=odd_w_out.astype(BF16),
        ffn_w_gate=ffn_w_gate.astype(BF16), ffn_w_up=ffn_w_up.astype(BF16),
        ffn_w_down=ffn_w_down.astype(BF16),
    )
    y_prompt, new_k, new_v, new_s = _trunk(
        x_prompt, mods[:, nb_lat:nb_lat + 1], None, None, None, wts, F32)
    y_sample, _, _, _ = _trunk(x_sample, mods[:, :nb_lat], cache_k, cache_v, state_ret, wts, BF16)
    return (y_prompt, y_sample, new_k, new_v, new_s)
```

```python
import functools
import math

import numpy as np
import jax
import jax.numpy as jnp
from jax import lax
from jax.experimental import pallas as pl
from jax.experimental.pallas import tpu as pltpu

F32 = jnp.float32
BF16 = jnp.bfloat16

D_MODEL = 2048
DEPTH = 2
GRID_W = 64
EPS = 1e-6
NEG_INF = -1e30
LOG2_E = math.log2(math.e)
FNET_GROUPS = 8
FNET_GROUP_DIM = 128
FNET_WIDTH = 1024
NA_HEADS = 8
NA_HEAD_DIM = 128
NA_WIDTH = 1024
NA_KH = 8
NA_KW = 16
EVEN_IN = FNET_WIDTH + 3 * NA_WIDTH
RET_HEADS = 8
RET_QK_DIM = 256
RET_V_DIM = 512
RET_QK_WIDTH = RET_HEADS * RET_QK_DIM
RET_V_WIDTH = RET_HEADS * RET_V_DIM
ODD_IN = 2 * RET_QK_WIDTH + 3 * RET_V_WIDTH
RET_BLOCK = 256
RET_HEADS_PER_STEP = 4
ROPE_BASE = 10000.0
D_FF = 5632
MOD_ROWS = 8
MOD_CHUNK = 128

V7X_VMEM_BUDGET = 56 * 1024 * 1024
VMEM_SLACK = 8 * 1024 * 1024


def _params(semantics, vmem_bytes):
    return pltpu.CompilerParams(dimension_semantics=semantics,
                                vmem_limit_bytes=min(int(vmem_bytes), V7X_VMEM_BUDGET))


def _silu(x):
    return x * (1.0 / (1.0 + jnp.exp(-x)))


def _adaln_kernel(c_ref, w_ref, b_ref, o_ref):
    s = _silu(c_ref[...]).astype(BF16)
    w = w_ref[0].astype(BF16)
    o_ref[0] = jnp.dot(s, w, preferred_element_type=F32) + b_ref[0]


def adaln_all(cond, ada_w, ada_b):
    n = ada_w.shape[-1]
    tn = 1024
    return pl.pallas_call(
        _adaln_kernel,
        out_shape=jax.ShapeDtypeStruct((DEPTH, MOD_ROWS, n), F32),
        grid=(DEPTH, n // tn),
        in_specs=[
            pl.BlockSpec((MOD_ROWS, D_MODEL), lambda l, j: (0, 0)),
            pl.BlockSpec((1, D_MODEL, tn), lambda l, j: (l, 0, j)),
            pl.BlockSpec((1, 1, tn), lambda l, j: (l, 0, j)),
        ],
        out_specs=pl.BlockSpec((1, MOD_ROWS, tn), lambda l, j: (l, 0, j)),
        compiler_params=_params(("arbitrary", "arbitrary"), 40 << 20),
        name="adaln",
    )(cond, ada_w, ada_b.reshape(DEPTH, 1, n))


def _modulate_into(h_ref, x_ref, g_ref, sh_ref, sc_ref):
    shift = sh_ref[0]
    gain = g_ref[...] * (1.0 + sc_ref[0])

    def body(c, carry):
        rows = pl.ds(pl.multiple_of(c * MOD_CHUNK, MOD_CHUNK), MOD_CHUNK)
        x = x_ref[rows, :]
        ms = jnp.mean(x * x, axis=-1, keepdims=True)
        h_ref[rows, :] = (x * lax.rsqrt(ms + EPS) * gain + shift).astype(h_ref.dtype)
        return carry

    lax.fori_loop(0, x_ref.shape[0] // MOD_CHUNK, body, 0, unroll=2)


ROPE_QUARTER = RET_QK_DIM // 4


def _pair_order(x, axis):
    q = ROPE_QUARTER
    parts = [lax.slice_in_dim(x, a * q, (a + 1) * q, axis=axis) for a in (0, 2, 1, 3)]
    return jnp.concatenate(parts, axis=axis)


def _pair_order_rows(s):
    return _pair_order(s, 0)


def _odd_w_prep_kernel(w_ref, o_ref):
    for head in range(2 * RET_HEADS):
        cs = slice(head * RET_QK_DIM, (head + 1) * RET_QK_DIM)
        o_ref[0, :, cs] = _pair_order(w_ref[0, :, cs], 1).astype(o_ref.dtype)
    rest = 2 * RET_QK_WIDTH
    o_ref[0, :, rest:] = w_ref[0, :, rest:].astype(o_ref.dtype)


def odd_w_prep(w):
    n, d, width = w.shape
    tr = 128
    return pl.pallas_call(
        _odd_w_prep_kernel,
        out_shape=jax.ShapeDtypeStruct(w.shape, BF16),
        grid=(n, d // tr),
        in_specs=[pl.BlockSpec((1, tr, width), lambda l, i: (l, i, 0))],
        out_specs=pl.BlockSpec((1, tr, width), lambda l, i: (l, i, 0)),
        compiler_params=_params(("arbitrary", "arbitrary"), 2 * tr * width * 6 + 3 * tr * width * 4 + VMEM_SLACK),
        name="odd_w_prep",
    )(w)


def _rope_tables(t_len):
    inv = ROPE_BASE ** (-jnp.arange(ROPE_QUARTER, dtype=F32) / ROPE_QUARTER)
    t = jnp.arange(t_len)
    ang = jnp.concatenate([(t // GRID_W).astype(F32)[:, None] * inv[None, :],
                           (t % GRID_W).astype(F32)[:, None] * inv[None, :]], axis=-1)
    return jnp.cos(ang), jnp.sin(ang)


def _rope(x, cos, sin):
    half = RET_QK_DIM // 2
    x1, x2 = x[:, :half], x[:, half:]
    return jnp.concatenate([x1 * cos - x2 * sin, x1 * sin + x2 * cos], axis=-1)


def _modproj_kernel(*refs, rope_tiles):
    if rope_tiles:
        x_ref, sh_ref, sc_ref, g_ref, w_ref, cos_ref, sin_ref, o_ref, h_ref = refs
    else:
        x_ref, sh_ref, sc_ref, g_ref, w_ref, o_ref, h_ref = refs
    j = pl.program_id(1)

    @pl.when(j == 0)
    def _():
        _modulate_into(h_ref, x_ref, g_ref, sh_ref, sc_ref)

    acc = jnp.dot(h_ref[...], w_ref[...], preferred_element_type=F32)
    if not rope_tiles:
        o_ref[...] = acc.astype(o_ref.dtype)
        return

    @pl.when(j < rope_tiles)
    def _():
        cos = cos_ref[...]
        sin = sin_ref[...]
        for s in range(acc.shape[1] // RET_QK_DIM):
            cs = slice(s * RET_QK_DIM, (s + 1) * RET_QK_DIM)
            o_ref[:, cs] = _rope(acc[:, cs], cos, sin).astype(o_ref.dtype)

    @pl.when(j >= rope_tiles)
    def _():
        o_ref[...] = acc.astype(o_ref.dtype)


def modproj(x, shift, scale, g, w, rows_per_mod, out_dtype, tm, tn, rope=None):
    m, d = x.shape
    n = w.shape[1]
    osz = jnp.dtype(out_dtype).itemsize
    vmem = 2 * tm * d * 4 + 2 * d * tn * 2 + 2 * tm * tn * osz + tm * d * 2 + 2 * tm * tn * 4 + VMEM_SLACK
    mod_idx = lambda i, j: ((i * tm) // rows_per_mod, 0, 0)
    in_specs = [
        pl.BlockSpec((tm, d), lambda i, j: (i, 0)),
        pl.BlockSpec((1, 1, d), mod_idx),
        pl.BlockSpec((1, 1, d), mod_idx),
        pl.BlockSpec((1, d), lambda i, j: (0, 0)),
        pl.BlockSpec((d, tn), lambda i, j: (0, j)),
    ]
    args = [x, shift, scale, g, w]
    rope_tiles = 0
    if rope is not None:
        seq_len, n_cols = rope
        rope_tiles = n_cols // tn
        cos, sin = _rope_tables(seq_len)
        tab = pl.BlockSpec((tm, RET_QK_DIM // 2), lambda i, j: (i % (seq_len // tm), 0))
        in_specs += [tab, tab]
        args += [cos, sin]
        vmem += 2 * 2 * tm * (RET_QK_DIM // 2) * 4
    return pl.pallas_call(
        functools.partial(_modproj_kernel, rope_tiles=rope_tiles),
        out_shape=jax.ShapeDtypeStruct((m, n), out_dtype),
        grid=(m // tm, n // tn),
        in_specs=in_specs,
        out_specs=pl.BlockSpec((tm, tn), lambda i, j: (i, j)),
        scratch_shapes=[pltpu.VMEM((tm, d), BF16)],
        compiler_params=_params(("arbitrary", "arbitrary"), vmem),
        name="modproj",
    )(*args)


def _head_rms(x, g):
    return x * lax.rsqrt(jnp.mean(x * x, axis=-1, keepdims=True) + EPS) * g


def _ctx_attn_kernel(q_ref, k_ref, v_ref, qg_ref, kg_ref, o_ref, nk_ref, nv_ref):
    scale = NA_HEAD_DIM ** -0.5
    for h in range(NA_HEADS):
        cs = slice(h * NA_HEAD_DIM, (h + 1) * NA_HEAD_DIM)
        q = _head_rms(q_ref[0, :, cs].astype(F32), qg_ref[...])
        k = _head_rms(k_ref[0, :, cs].astype(F32), kg_ref[...])
        v = v_ref[0, :, cs].astype(F32)
        nk_ref[0, 0, h] = k
        nv_ref[0, 0, h] = v
        s = lax.dot_general(q.astype(BF16), k.astype(BF16), (((1,), (1,)), ((), ())),
                            preferred_element_type=F32) * scale
        m = jnp.max(s, axis=-1, keepdims=True)
        e = jnp.exp(s - m)
        l = jnp.sum(e, axis=-1, keepdims=True)
        o = jnp.dot(e.astype(BF16), v.astype(BF16), preferred_element_type=F32) / l
        o_ref[0, :, cs] = o.astype(o_ref.dtype)


def ctx_attention(p, q_gain, k_gain):
    b, t, _ = p.shape
    cache_shape = (b, 1, NA_HEADS, t, NA_HEAD_DIM)
    blk = lambda c: pl.BlockSpec((1, t, NA_WIDTH), lambda i, c=c: (i, 0, c))
    gain = pl.BlockSpec((1, NA_HEAD_DIM), lambda i: (0, 0))
    cache_spec = pl.BlockSpec((1, 1, NA_HEADS, t, NA_HEAD_DIM), lambda i: (i, 0, 0, 0, 0))
    return pl.pallas_call(
        _ctx_attn_kernel,
        out_shape=(jax.ShapeDtypeStruct((b, t, NA_WIDTH), BF16),
                   jax.ShapeDtypeStruct(cache_shape, F32),
                   jax.ShapeDtypeStruct(cache_shape, F32)),
        grid=(b,),
        in_specs=[blk(1), blk(2), blk(3), gain, gain],
        out_specs=(pl.BlockSpec((1, t, NA_WIDTH), lambda i: (i, 0, 0)), cache_spec, cache_spec),
        compiler_params=_params(("arbitrary",), 32 << 20),
        name="ctx_attention",
    )(p, p, p, q_gain, k_gain)


NA_BIAS_ROWS = 2 * NA_KH - 1
NA_BIAS_COLS = 2 * NA_KW - 1
NA_MASK_TILE = NA_BIAS_ROWS
NA_QROWS = 4
NA_KROWS = NA_KH + NA_QROWS


def _na_build_bias(rb_ref, bias_ref, head):
    shape = (GRID_W, 2 * GRID_W)
    lane = lax.broadcasted_iota(jnp.int32, shape, 1)
    qc = lax.broadcasted_iota(jnp.int32, shape, 0)
    kc = lane & (GRID_W - 1)
    start = jnp.clip(qc - NA_KW // 2, 0, GRID_W - NA_KW)
    ok = (kc >= start) & (kc < start + NA_KW)
    delta = kc - qc + (NA_KW - 1)
    left = lane < GRID_W
    base = head * (NA_BIAS_ROWS * NA_BIAS_COLS)

    def build(dr, carry):
        acc = jnp.full(shape, NEG_INF, F32)
        for dc in range(NA_BIAS_COLS):
            acc = jnp.where(delta == dc, rb_ref[base + dr * NA_BIAS_COLS + dc], acc)
        tile = jnp.where(ok, acc * LOG2_E, NEG_INF)
        bias_ref[0, dr] = jnp.where(left, tile, 0.0)
        bias_ref[1, dr] = jnp.where(left, 0.0, tile)
        return carry

    lax.fori_loop(0, NA_BIAS_ROWS, build, 0)
    bias_ref[0, NA_MASK_TILE] = jnp.where(left, NEG_INF, 0.0)
    bias_ref[1, NA_MASK_TILE] = jnp.where(left, 0.0, NEG_INF)


def _na_kernel(rb_ref, q_ref, k_ref, v_ref, kc_ref, vc_ref, qg_ref, kg_ref, o_ref,
               qs_ref, ks_ref, vs_ref, bias_ref, *, rows):
    @pl.when(pl.program_id(1) == 0)
    def _():
        _na_build_bias(rb_ref, bias_ref, pl.program_id(0))

    scale = NA_HEAD_DIM ** -0.5 * LOG2_E
    qs_ref[...] = (_head_rms(q_ref[0].astype(F32), qg_ref[...]) * scale).astype(BF16)
    ks_ref[...] = _head_rms(k_ref[0].astype(F32), kg_ref[...]).astype(BF16)
    vs_ref[...] = v_ref[0].astype(BF16)
    kc = kc_ref[0, 0, 0].astype(BF16)
    vc = vc_ref[0, 0, 0].astype(BF16)
    nt = (((1,), (1,)), ((), ()))
    nq = NA_QROWS * GRID_W
    nk = NA_KROWS * GRID_W

    def tile_index(r, rs, kr):
        inside = (kr >= rs) & (kr < rs + NA_KH)
        return jnp.where(inside, kr - r + (NA_KH - 1), NA_MASK_TILE)

    def body(blk, carry):
        r0 = blk * NA_QROWS
        k0 = jnp.clip(r0 - NA_KH // 2, 0, rows - NA_KROWS)
        q = qs_ref[pl.ds(pl.multiple_of(r0 * GRID_W, nq), nq), :]
        kw = ks_ref[pl.ds(pl.multiple_of(k0 * GRID_W, GRID_W), nk), :]
        vw = vs_ref[pl.ds(pl.multiple_of(k0 * GRID_W, GRID_W), nk), :]
        bias_rows = []
        for i in range(NA_QROWS):
            r = r0 + i
            rs = jnp.clip(r - NA_KH // 2, 0, rows - NA_KH)
            pairs = [bias_ref[0, tile_index(r, rs, k0 + 2 * jp)]
                     + bias_ref[1, tile_index(r, rs, k0 + 2 * jp + 1)]
                     for jp in range(NA_KROWS // 2)]
            bias_rows.append(jnp.concatenate(pairs, axis=-1))
        bias = jnp.concatenate(bias_rows, axis=0)
        s_w = lax.dot_general(q, kw, nt, preferred_element_type=F32) + bias
        s_c = lax.dot_general(q, kc, nt, preferred_element_type=F32)
        m = jnp.maximum(jnp.max(s_w, axis=-1, keepdims=True), jnp.max(s_c, axis=-1, keepdims=True))
        e_w = jnp.exp2(s_w - m)
        e_c = jnp.exp2(s_c - m)
        l = jnp.sum(e_w, axis=-1, keepdims=True) + jnp.sum(e_c, axis=-1, keepdims=True)
        o = (jnp.dot(e_w.astype(BF16), vw, preferred_element_type=F32)
             + jnp.dot(e_c.astype(BF16), vc, preferred_element_type=F32)) / l
        o_ref[0, pl.ds(pl.multiple_of(r0 * GRID_W, nq), nq), :] = o.astype(o_ref.dtype)
        return carry

    lax.fori_loop(0, rows // NA_QROWS, body, 0, unroll=2)


def na_attention(p, cache_k, cache_v, layer_j, rel_bias, q_gain, k_gain):
    b, t, _ = p.shape
    past = cache_k.shape[3]
    rows = t // GRID_W
    hd = NA_HEAD_DIM
    col0 = FNET_WIDTH // hd
    blk = lambda c: pl.BlockSpec((1, t, hd), lambda h, i, rb, c=c: (i, 0, col0 + c * NA_HEADS + h))
    cache_spec = pl.BlockSpec((1, 1, 1, past, hd), lambda h, i, rb: (i, layer_j, h, 0, 0))
    gain = pl.BlockSpec((1, hd), lambda h, i, rb: (0, 0))
    return pl.pallas_call(
        functools.partial(_na_kernel, rows=rows),
        out_shape=jax.ShapeDtypeStruct((b, t, NA_WIDTH), BF16),
        grid_spec=pltpu.PrefetchScalarGridSpec(
            num_scalar_prefetch=1,
            grid=(NA_HEADS, b),
            in_specs=[blk(0), blk(1), blk(2), cache_spec, cache_spec, gain, gain],
            out_specs=pl.BlockSpec((1, t, hd), lambda h, i, rb: (i, 0, h)),
            scratch_shapes=[pltpu.VMEM((t, hd), BF16)] * 3
            + [pltpu.VMEM((2, NA_BIAS_ROWS + 1, GRID_W, 2 * GRID_W), F32)],
        ),
        compiler_params=_params(("arbitrary", "arbitrary"), 40 << 20),
        name="na_attention",
    )(rel_bias.astype(F32).reshape(-1), p, p, p, cache_k, cache_v, q_gain, k_gain)


FNET_BLOCK = 16


def _fnet_tables(n1, n2):
    t_len = n1 * n2
    tb = FNET_BLOCK
    eye = np.eye(tb)
    k1 = np.arange(n1)
    ang1 = (2.0 * np.pi / n1) * ((k1[:, None] * k1[None, :]) % n1)
    f1 = np.concatenate([np.cos(ang1), -np.sin(ang1)], axis=0) / math.sqrt(t_len)
    rows_mat = np.kron(f1, eye)
    t2 = np.arange(n2)
    tw = (2.0 * np.pi / t_len) * (k1[:, None] * t2[None, :])
    tw = tw.reshape(n1, n2 // tb, tb).transpose(1, 0, 2).reshape(n2 // tb, n1 * tb, 1)
    c = np.arange(FNET_GROUP_DIM)
    ang3 = (2.0 * np.pi / FNET_GROUP_DIM) * ((c[:, None] * c[None, :]) % FNET_GROUP_DIM)
    c3, s3 = np.cos(ang3), np.sin(ang3)
    chan = np.block([[c3, -s3], [s3, c3]]) / math.sqrt(FNET_GROUP_DIM)
    ang2 = (2.0 * np.pi / n2) * ((t2[:, None] * t2[None, :]) % n2)
    cols_re = np.einsum("pq,kt->kpqt", eye, np.cos(ang2)).reshape(n2 * tb, tb * n2)
    cols_im = np.einsum("pq,kt->kpqt", eye, np.sin(ang2)).reshape(n2 * tb, tb * n2)
    f32 = lambda x: jnp.asarray(x, dtype=F32)
    bf = lambda x: f32(x).astype(BF16)
    return bf(rows_mat), f32(np.cos(tw)), f32(np.sin(tw)), bf(chan), bf(cols_re), bf(cols_im)


def _fnet_rows_kernel(x_ref, mat_ref, twc_ref, tws_ref, chan_ref, vr_ref, vi_ref):
    _, n1, tb, width = x_ref.shape
    rows = n1 * tb
    gd = FNET_GROUP_DIM
    x = x_ref[0].reshape(rows, width).astype(BF16)
    h = jnp.dot(mat_ref[...], x, preferred_element_type=F32)
    hr, hi = h[:rows], h[rows:]
    c, s = twc_ref[0], tws_ref[0]
    gr = (hr * c + hi * s).astype(BF16)
    gi = (hi * c - hr * s).astype(BF16)
    stacked = jnp.concatenate(
        [jnp.concatenate([gr[:, g * gd:(g + 1) * gd], gi[:, g * gd:(g + 1) * gd]], axis=1)
         for g in range(FNET_GROUPS)], axis=0)
    v = jnp.dot(stacked, chan_ref[...], preferred_element_type=F32)
    for g in range(FNET_GROUPS):
        part = v[g * rows:(g + 1) * rows]
        vr_ref[0, :, :, g * gd:(g + 1) * gd] = part[:, :gd].reshape(n1, tb, gd).astype(vr_ref.dtype)
        vi_ref[0, :, :, g * gd:(g + 1) * gd] = part[:, gd:].reshape(n1, tb, gd).astype(vi_ref.dtype)


def _fnet_cols_kernel(vr_ref, vi_ref, re_ref, im_ref, o_ref):
    _, kb, n2, width = vr_ref.shape
    vr = vr_ref[0].reshape(kb * n2, width)
    vi = vi_ref[0].reshape(kb * n2, width)
    y = (jnp.dot(re_ref[...], vr, preferred_element_type=F32)
         + jnp.dot(im_ref[...], vi, preferred_element_type=F32))
    o_ref[0] = y.reshape(n2, kb, width).astype(o_ref.dtype)


def fourier_mix(p, n1, n2):
    b, t, width = p.shape
    tb = FNET_BLOCK
    w = FNET_WIDTH
    rows_mat, twc, tws, chan, cols_re, cols_im = _fnet_tables(n1, n2)
    const = lambda shape: pl.BlockSpec(shape, lambda i, s: (0,) * len(shape))
    tw_spec = pl.BlockSpec((1, n1 * tb, 1), lambda i, s: (s, 0, 0))
    v_shape = jax.ShapeDtypeStruct((b, n1, n2, w), BF16)
    v_spec = pl.BlockSpec((1, n1, tb, w), lambda i, s: (i, 0, s, 0))
    isz = p.dtype.itemsize
    vmem = (2 * n1 * tb * w * isz + 2 * rows_mat.size * 2 + 4 * n1 * tb * w * 2
            + 6 * n1 * tb * w * 4 + VMEM_SLACK)
    vr, vi = pl.pallas_call(
        _fnet_rows_kernel,
        out_shape=(v_shape, v_shape),
        grid=(b, n2 // tb),
        in_specs=[pl.BlockSpec((1, n1, tb, w), lambda i, s: (i, 0, s, 0)),
                  const(rows_mat.shape), tw_spec, tw_spec, const(chan.shape)],
        out_specs=(v_spec, v_spec),
        compiler_params=_params(("arbitrary", "arbitrary"), vmem),
        name="fnet_rows",
    )(p.reshape(b, n1, n2, width), rows_mat, twc, tws, chan)
    v_in = pl.BlockSpec((1, tb, n2, w), lambda i, s: (i, s, 0, 0))
    vmem = 2 * 2 * tb * n2 * w * 2 + 2 * 2 * cols_re.size * 2 + 2 * n2 * tb * w * 2 + 3 * n2 * tb * w * 4 + VMEM_SLACK
    out = pl.pallas_call(
        _fnet_cols_kernel,
        out_shape=jax.ShapeDtypeStruct((b, n2, n1, w), BF16),
        grid=(b, n1 // tb),
        in_specs=[v_in, v_in, const(cols_re.shape), const(cols_im.shape)],
        out_specs=pl.BlockSpec((1, n2, tb, w), lambda i, s: (i, 0, s, 0)),
        compiler_params=_params(("arbitrary", "arbitrary"), vmem),
        name="fnet_cols",
    )(vr, vi, cols_re, cols_im)
    return out.reshape(b, t, w)


def _even_out_kernel(f_ref, a_ref, wf_ref, wa_ref, x_ref, gate_ref, o_ref):
    y = (jnp.dot(f_ref[...], wf_ref[...], preferred_element_type=F32)
         + jnp.dot(a_ref[...], wa_ref[...], preferred_element_type=F32))
    o_ref[...] = x_ref[...] + gate_ref[0] * y


def even_out(f, a, w, x, gate, rows_per_mod, tm, tn):
    m, d = x.shape
    kf = f.shape[1]
    vmem = 2 * 2 * tm * kf * 2 + 2 * 2 * kf * tn * 2 + 4 * tm * tn * 4 + 2 * tm * tn * 4 + VMEM_SLACK
    return pl.pallas_call(
        _even_out_kernel,
        out_shape=jax.ShapeDtypeStruct((m, d), F32),
        grid=(d // tn, m // tm),
        in_specs=[
            pl.BlockSpec((tm, kf), lambda j, i: (i, 0)),
            pl.BlockSpec((tm, kf), lambda j, i: (i, 0)),
            pl.BlockSpec((kf, tn), lambda j, i: (0, j)),
            pl.BlockSpec((kf, tn), lambda j, i: (1, j)),
            pl.BlockSpec((tm, tn), lambda j, i: (i, j)),
            pl.BlockSpec((1, 1, tn), lambda j, i: ((i * tm) // rows_per_mod, 0, j)),
        ],
        out_specs=pl.BlockSpec((tm, tn), lambda j, i: (i, j)),
        compiler_params=_params(("arbitrary", "arbitrary"), vmem),
        name="even_out",
    )(f, a, w, w, x, gate)


def _gated_group_norm(o, g):
    o = o.astype(F32)
    oc = o - jnp.mean(o, axis=-1, keepdims=True)
    gn = oc * lax.rsqrt(jnp.mean(oc * oc, axis=-1, keepdims=True) + EPS)
    hg = 0.5 * g.astype(F32)
    return (hg + hg * jnp.tanh(hg)) * gn


def _odd_out_kernel(of_ref, ob_ref, gf_ref, gb_ref, w_ref, x_ref, gate_ref, o_ref):
    acc = None
    for h in range(RET_HEADS):
        cs = slice(h * RET_V_DIM, (h + 1) * RET_V_DIM)
        y = (_gated_group_norm(of_ref[:, cs], gf_ref[:, cs])
             + _gated_group_norm(ob_ref[:, cs], gb_ref[:, cs])).astype(BF16)
        part = jnp.dot(y, w_ref[cs, :], preferred_element_type=F32)
        acc = part if acc is None else acc + part
    o_ref[...] = x_ref[...] + gate_ref[0] * acc


def odd_out(o_f, o_b, p, w, x, gate, rows_per_mod, tm):
    m, d = x.shape
    k = w.shape[0]
    gcol = (2 * RET_QK_WIDTH + RET_V_WIDTH) // RET_V_WIDTH
    vmem = (2 * 4 * tm * k * 2 + k * d * 2 + 4 * tm * d * 4 + 2 * tm * d * 4
            + 6 * tm * RET_V_DIM * 4 + VMEM_SLACK)
    act = lambda c: pl.BlockSpec((tm, k), lambda i, c=c: (i, c))
    return pl.pallas_call(
        _odd_out_kernel,
        out_shape=jax.ShapeDtypeStruct((m, d), F32),
        grid=(m // tm,),
        in_specs=[
            act(0), act(0), act(gcol), act(gcol + 1),
            pl.BlockSpec((k, d), lambda i: (0, 0), pipeline_mode=pl.Buffered(1)),
            pl.BlockSpec((tm, d), lambda i: (i, 0)),
            pl.BlockSpec((1, 1, d), lambda i: ((i * tm) // rows_per_mod, 0, 0)),
        ],
        out_specs=pl.BlockSpec((tm, d), lambda i: (i, 0)),
        compiler_params=_params(("arbitrary",), vmem),
        name="odd_out",
    )(o_f, o_b, p, p, w, x, gate)


def _ffn_kernel(x_ref, sh_ref, sc_ref, gate_ref, g_ref, wg_ref, wu_ref, wd_ref, o_ref, h_ref):
    j = pl.program_id(1)

    @pl.when(j == 0)
    def _():
        _modulate_into(h_ref, x_ref, g_ref, sh_ref, sc_ref)
        o_ref[...] = x_ref[...]

    h = h_ref[...]
    a = jnp.dot(h, wg_ref[0], preferred_element_type=F32)
    u = jnp.dot(h, wu_ref[0], preferred_element_type=F32)
    act = (_silu(a) * u).astype(BF16)
    gate = gate_ref[0]
    half = o_ref.shape[1] // 2
    for c in range(2):
        cs = slice(c * half, (c + 1) * half)
        o_ref[:, cs] += gate[:, cs] * jnp.dot(act, wd_ref[0, :, cs], preferred_element_type=F32)


def ffn_block(x, shift, scale, gate, g, w_gate, w_up, w_down, layer, rows_per_mod, tm, tf):
    m, d = x.shape
    f = w_gate.shape[2]
    vmem = 4 * tm * d * 4 + tm * d * 2 + 2 * 3 * d * tf * 2 + 3 * tm * tf * 4 + tm * d * 4 + VMEM_SLACK
    mod_idx = lambda i, j: ((i * tm) // rows_per_mod, 0, 0)
    mod = pl.BlockSpec((1, 1, d), mod_idx)
    return pl.pallas_call(
        _ffn_kernel,
        out_shape=jax.ShapeDtypeStruct((m, d), F32),
        grid=(m // tm, f // tf),
        in_specs=[
            pl.BlockSpec((tm, d), lambda i, j: (i, 0)),
            mod, mod, mod,
            pl.BlockSpec((1, d), lambda i, j: (0, 0)),
            pl.BlockSpec((1, d, tf), lambda i, j: (layer, 0, j)),
            pl.BlockSpec((1, d, tf), lambda i, j: (layer, 0, j)),
            pl.BlockSpec((1, tf, d), lambda i, j: (layer, j, 0)),
        ],
        out_specs=pl.BlockSpec((tm, d), lambda i, j: (i, 0)),
        scratch_shapes=[pltpu.VMEM((tm, d), BF16)],
        compiler_params=_params(("arbitrary", "arbitrary"), vmem),
        name="ffn_block",
    )(x, shift, scale, gate, g, w_gate, w_up, w_down)


def _retention_kernel(lg_ref, *refs, latent, n_blocks):
    if latent:
        (qf_ref, kf_ref, vf_ref, qb_ref, kb_ref, vb_ref, s0_ref,
         of_ref, ob_ref, s_ref, dec_ref, qd_ref, kd_ref) = refs
    else:
        (qf_ref, kf_ref, vf_ref, qb_ref, kb_ref, vb_ref,
         of_ref, ob_ref, sfin_ref, s_ref, dec_ref, qd_ref, kd_ref) = refs
    h0 = pl.program_id(0) * RET_HEADS_PER_STEP
    i = pl.program_id(2)
    c = RET_BLOCK
    dk, dv = RET_QK_DIM, RET_V_DIM
    scale = RET_QK_DIM ** -0.5
    scans = [(d, hh) for d in range(2) for hh in range(RET_HEADS_PER_STEP)]

    @pl.when((pl.program_id(1) == 0) & (i == 0))
    def _():
        row = lax.broadcasted_iota(jnp.int32, (c, c), 0).astype(F32)
        col = lax.broadcasted_iota(jnp.int32, (c, c), 1).astype(F32)
        t_idx = lax.broadcasted_iota(jnp.int32, (c, 1), 0).astype(F32)
        for d, hh in scans:
            lg = lg_ref[d, h0 + hh]
            diff = row - col if d == 0 else col - row
            dec_ref[d, hh] = jnp.where(diff >= 0, jnp.exp(lg * jnp.maximum(diff, 0.0)), 0.0) * scale
            pos = t_idx if d == 0 else (c - 1.0) - t_idx
            qd_ref[d, hh] = jnp.exp(lg * (pos + 1.0))
            kd_ref[d, hh] = jnp.exp(lg * ((c - 1.0) - pos)) * scale

    @pl.when(i == 0)
    def _():
        for d, hh in scans:
            s_ref[d, hh] = (_pair_order_rows(s0_ref[0, 0, d, hh]) if latent
                            else jnp.zeros((dk, dv), F32))

    nt = (((1,), (1,)), ((), ()))
    tn = (((0,), (0,)), ((), ()))
    chunk_refs = ((qf_ref, kf_ref, vf_ref, of_ref), (qb_ref, kb_ref, vb_ref, ob_ref))
    for d, hh in scans:
        q_ref, k_ref, v_ref, o_ref = chunk_refs[d]
        q = q_ref[0, :, hh * dk:(hh + 1) * dk]
        k = k_ref[0, :, hh * dk:(hh + 1) * dk]
        v = v_ref[0, :, hh * dv:(hh + 1) * dv]
        a = lax.dot_general(q, k, nt, preferred_element_type=F32)
        inner_w = (a * dec_ref[d, hh]).astype(BF16)
        q_dec = (q.astype(F32) * qd_ref[d, hh]).astype(BF16)
        k_dec = (k.astype(F32) * kd_ref[d, hh]).astype(BF16)
        s_old = s_ref[d, hh]
        o = (jnp.dot(inner_w, v, preferred_element_type=F32)
             + jnp.dot(q_dec, s_old.astype(BF16), preferred_element_type=F32))
        s_new = (s_old * jnp.exp(lg_ref[d, h0 + hh] * c)
                 + lax.dot_general(k_dec, v, tn, preferred_element_type=F32))
        s_ref[d, hh] = s_new
        o_ref[0, :, hh * dv:(hh + 1) * dv] = o.astype(o_ref.dtype)
        if not latent:
            @pl.when(i == n_blocks - 1)
            def _(d=d, hh=hh, s_new=s_new):
                sfin_ref[0, 0, d, hh] = _pair_order_rows(s_new)


def retention(p, log_gamma, state0, layer_j):
    b, t, _ = p.shape
    latent = state0 is not None
    c = RET_BLOCK
    nb = t // c
    hps = RET_HEADS_PER_STEP
    dk, dv, nh = hps * RET_QK_DIM, hps * RET_V_DIM, RET_HEADS // hps

    def chunk_specs(blk_of):
        return [
            pl.BlockSpec((1, c, dk), lambda h, bi, i, lg: (bi, blk_of(i), h)),
            pl.BlockSpec((1, c, dk), lambda h, bi, i, lg: (bi, blk_of(i), nh + h)),
            pl.BlockSpec((1, c, dv), lambda h, bi, i, lg: (bi, blk_of(i), nh + h)),
        ]

    fwd = lambda i: i
    bwd = lambda i: nb - 1 - i
    in_specs = chunk_specs(fwd) + chunk_specs(bwd)
    args = [p] * 6
    o_shape = jax.ShapeDtypeStruct((b, t, RET_V_WIDTH), BF16)
    out_shape = [o_shape, o_shape]
    out_specs = [pl.BlockSpec((1, c, dv), lambda h, bi, i, lg: (bi, fwd(i), h)),
                 pl.BlockSpec((1, c, dv), lambda h, bi, i, lg: (bi, bwd(i), h))]
    state_dims = (RET_QK_DIM, RET_V_DIM)
    state_block = (1, 1, 2, hps) + state_dims
    if latent:
        in_specs.append(pl.BlockSpec(state_block, lambda h, bi, i, lg: (bi, layer_j, 0, h, 0, 0)))
        args.append(state0)
    else:
        out_shape.append(jax.ShapeDtypeStruct((b, 1, 2, RET_HEADS) + state_dims, F32))
        out_specs.append(pl.BlockSpec(state_block, lambda h, bi, i, lg: (bi, 0, 0, h, 0, 0)))
    return pl.pallas_call(
        functools.partial(_retention_kernel, latent=latent, n_blocks=nb),
        out_shape=tuple(out_shape),
        grid_spec=pltpu.PrefetchScalarGridSpec(
            num_scalar_prefetch=1,
            grid=(nh, b, nb),
            in_specs=in_specs,
            out_specs=tuple(out_specs),
            scratch_shapes=[pltpu.VMEM((2, hps) + state_dims, F32), pltpu.VMEM((2, hps, c, c), F32),
                            pltpu.VMEM((2, hps, c, 1), F32), pltpu.VMEM((2, hps, c, 1), F32)],
        ),
        compiler_params=_params(("arbitrary",) * 3, 40 << 20),
        name="retention",
    )(log_gamma, *args)


def _trunk(x, mods, ctx_k, ctx_v, ctx_state, wts, p_dtype):
    b, t, d = x.shape
    is_ctx = ctx_k is None
    nbm = mods.shape[1]
    rows_per_mod = (b * t) // nbm
    x2 = x.reshape(b * t, d)
    new_k = new_v = new_s = None
    for i in range(DEPTH):
        mod = [mods[i, :, k][:, None, :] for k in range(6)]
        j = i // 2
        if i % 2 == 0:
            p = modproj(x2, mod[0], mod[1], wts["norm_g"][i, 0][None], wts["even_w_in"][j],
                        rows_per_mod, p_dtype, tm=1024, tn=1024).reshape(b, t, EVEN_IN)
            qg, kg = wts["even_q_norm"][j][None], wts["even_k_norm"][j][None]
            if is_ctx:
                attn, new_k, new_v = ctx_attention(p, qg, kg)
            else:
                attn = na_attention(p, ctx_k, ctx_v, j, wts["na_rel_bias"][j], qg, kg)
            n2 = GRID_W if not is_ctx else math.isqrt(t)
            fm = fourier_mix(p, t // n2, n2)
            x2 = even_out(fm.reshape(b * t, FNET_WIDTH), attn.reshape(b * t, NA_WIDTH),
                          wts["even_w_out"][j], x2, mod[2], rows_per_mod, tm=1024, tn=1024)
        else:
            rope = (t, 2 * RET_QK_WIDTH) if not is_ctx else None
            p = modproj(x2, mod[0], mod[1], wts["norm_g"][i, 0][None], wts["odd_w_in"][j],
                        rows_per_mod, BF16, tm=1024, tn=1024, rope=rope)
            lg = jax.nn.log_sigmoid(wts["ret_decay_logit"][j].astype(F32))
            if is_ctx:
                o_f, o_b, new_s = retention(p.reshape(b, t, ODD_IN), lg, None, j)
            else:
                o_f, o_b = retention(p.reshape(b, t, ODD_IN), lg, ctx_state, j)
            x2 = odd_out(o_f.reshape(b * t, RET_V_WIDTH), o_b.reshape(b * t, RET_V_WIDTH), p,
                         wts["odd_w_out"][j], x2, mod[2], rows_per_mod, tm=256)
        x2 = ffn_block(x2, mod[3], mod[4], mod[5], wts["norm_g"][i, 1][None], wts["ffn_w_gate"],
                       wts["ffn_w_up"], wts["ffn_w_down"], i, rows_per_mod, tm=512, tf=512)
    return x2.reshape(b, t, d), new_k, new_v, new_s


def kernel(x_prompt, x_sample, cache_k, cache_v, state_ret, c, c_ctx, ada_w, ada_b, norm_g,
           even_w_in, even_q_norm, even_k_norm, na_rel_bias, even_w_out, odd_w_in, ret_decay_logit,
           odd_w_out, ffn_w_gate, ffn_w_up, ffn_w_down):
    nb_lat = c.shape[0]
    cond = jnp.concatenate(
        [c, c_ctx[None, :], jnp.zeros((MOD_ROWS - nb_lat - 1, D_MODEL), F32)], axis=0)
    mods = adaln_all(cond, ada_w, ada_b).reshape(DEPTH, MOD_ROWS, 6, D_MODEL)
    wts = dict(
        norm_g=norm_g, even_q_norm=even_q_norm, even_k_norm=even_k_norm, na_rel_bias=na_rel_bias,
        ret_decay_logit=ret_decay_logit,
        even_w_in=even_w_in.astype(BF16), even_w_out=even_w_out.astype(BF16),
        odd_w_in=odd_w_prep(odd_w_in), odd_w_out=odd_w_out.astype(BF16),
        ffn_w_gate=ffn_w_gate.astype(BF16), ffn_w_up=ffn_w_up.astype(BF16),
        ffn_w_down=ffn_w_down.astype(BF16),
    )
    y_prompt, new_k, new_v, new_s = _trunk(
        x_prompt, mods[:, nb_lat:nb_lat + 1], None, None, None, wts, F32)
    y_sample, _, _, _ = _trunk(x_sample, mods[:, :nb_lat], cache_k, cache_v, state_ret, wts, BF16)
    return (y_prompt, y_sample, new_k, new_v, new_s)
```

```python
import functools
import math

import numpy as np
import jax
import jax.numpy as jnp
from jax import lax
from jax.experimental import pallas as pl
from jax.experimental.pallas import tpu as pltpu

F32 = jnp.float32
BF16 = jnp.bfloat16

D_MODEL = 2048
DEPTH = 2
GRID_W = 64
EPS = 1e-6
NEG_INF = -1e30
LOG2_E = math.log2(math.e)
FNET_GROUPS = 8
FNET_GROUP_DIM = 128
FNET_WIDTH = 1024
NA_HEADS = 8
NA_HEAD_DIM = 128
NA_WIDTH = 1024
NA_KH = 8
NA_KW = 16
EVEN_IN = FNET_WIDTH + 3 * NA_WIDTH
RET_HEADS = 8
RET_QK_DIM = 256
RET_V_DIM = 512
RET_QK_WIDTH = RET_HEADS * RET_QK_DIM
RET_V_WIDTH = RET_HEADS * RET_V_DIM
ODD_IN = 2 * RET_QK_WIDTH + 3 * RET_V_WIDTH
RET_BLOCK = 256
RET_HEADS_PER_STEP = 4
ROPE_BASE = 10000.0
D_FF = 5632
MOD_ROWS = 8
MOD_CHUNK = 128

V7X_VMEM_BUDGET = 56 * 1024 * 1024
VMEM_SLACK = 8 * 1024 * 1024


def _params(semantics, vmem_bytes):
    return pltpu.CompilerParams(dimension_semantics=semantics,
                                vmem_limit_bytes=min(int(vmem_bytes), V7X_VMEM_BUDGET))


def _silu(x):
    return x * (1.0 / (1.0 + jnp.exp(-x)))


def _adaln_kernel(c_ref, w_ref, b_ref, o_ref):
    s = _silu(c_ref[...]).astype(BF16)
    w = w_ref[0].astype(BF16)
    o_ref[0] = jnp.dot(s, w, preferred_element_type=F32) + b_ref[0]


def adaln_all(cond, ada_w, ada_b):
    n = ada_w.shape[-1]
    tn = 1024
    return pl.pallas_call(
        _adaln_kernel,
        out_shape=jax.ShapeDtypeStruct((DEPTH, MOD_ROWS, n), F32),
        grid=(DEPTH, n // tn),
        in_specs=[
            pl.BlockSpec((MOD_ROWS, D_MODEL), lambda l, j: (0, 0)),
            pl.BlockSpec((1, D_MODEL, tn), lambda l, j: (l, 0, j)),
            pl.BlockSpec((1, 1, tn), lambda l, j: (l, 0, j)),
        ],
        out_specs=pl.BlockSpec((1, MOD_ROWS, tn), lambda l, j: (l, 0, j)),
        compiler_params=_params(("arbitrary", "arbitrary"), 40 << 20),
        name="adaln",
    )(cond, ada_w, ada_b.reshape(DEPTH, 1, n))


def _modulate_into(h_ref, x_ref, g_ref, sh_ref, sc_ref):
    shift = sh_ref[0]
    gain = g_ref[...] * (1.0 + sc_ref[0])

    def body(c, carry):
        rows = pl.ds(pl.multiple_of(c * MOD_CHUNK, MOD_CHUNK), MOD_CHUNK)
        x = x_ref[rows, :]
        ms = jnp.mean(x * x, axis=-1, keepdims=True)
        h_ref[rows, :] = (x * lax.rsqrt(ms + EPS) * gain + shift).astype(h_ref.dtype)
        return carry

    lax.fori_loop(0, x_ref.shape[0] // MOD_CHUNK, body, 0, unroll=2)


ROPE_QUARTER = RET_QK_DIM // 4


def _pair_order(x, axis):
    q = ROPE_QUARTER
    parts = [lax.slice_in_dim(x, a * q, (a + 1) * q, axis=axis) for a in (0, 2, 1, 3)]
    return jnp.concatenate(parts, axis=axis)


def _pair_order_rows(s):
    return _pair_order(s, 0)


def _odd_w_prep_kernel(w_ref, o_ref):
    for head in range(2 * RET_HEADS):
        cs = slice(head * RET_QK_DIM, (head + 1) * RET_QK_DIM)
        o_ref[0, :, cs] = _pair_order(w_ref[0, :, cs], 1).astype(o_ref.dtype)
    rest = 2 * RET_QK_WIDTH
    o_ref[0, :, rest:] = w_ref[0, :, rest:].astype(o_ref.dtype)


def odd_w_prep(w):
    n, d, width = w.shape
    tr = 128
    return pl.pallas_call(
        _odd_w_prep_kernel,
        out_shape=jax.ShapeDtypeStruct(w.shape, BF16),
        grid=(n, d // tr),
        in_specs=[pl.BlockSpec((1, tr, width), lambda l, i: (l, i, 0))],
        out_specs=pl.BlockSpec((1, tr, width), lambda l, i: (l, i, 0)),
        compiler_params=_params(("arbitrary", "arbitrary"), 2 * tr * width * 6 + 3 * tr * width * 4 + VMEM_SLACK),
        name="odd_w_prep",
    )(w)


def _rope_tables(t_len):
    inv = ROPE_BASE ** (-jnp.arange(ROPE_QUARTER, dtype=F32) / ROPE_QUARTER)
    t = jnp.arange(t_len)
    ang = jnp.concatenate([(t // GRID_W).astype(F32)[:, None] * inv[None, :],
                           (t % GRID_W).astype(F32)[:, None] * inv[None, :]], axis=-1)
    return jnp.cos(ang), jnp.sin(ang)


def _rope(x, cos, sin):
    half = RET_QK_DIM // 2
    x1, x2 = x[:, :half], x[:, half:]
    return jnp.concatenate([x1 * cos - x2 * sin, x1 * sin + x2 * cos], axis=-1)


def _modproj_kernel(*refs, rope_tiles):
    if rope_tiles:
        x_ref, sh_ref, sc_ref, g_ref, w_ref, cos_ref, sin_ref, o_ref, h_ref = refs
    else:
        x_ref, sh_ref, sc_ref, g_ref, w_ref, o_ref, h_ref = refs
    j = pl.program_id(1)

    @pl.when(j == 0)
    def _():
        _modulate_into(h_ref, x_ref, g_ref, sh_ref, sc_ref)

    acc = jnp.dot(h_ref[...], w_ref[...], preferred_element_type=F32)
    if not rope_tiles:
        o_ref[...] = acc.astype(o_ref.dtype)
        return

    @pl.when(j < rope_tiles)
    def _():
        cos = cos_ref[...]
        sin = sin_ref[...]
        for s in range(acc.shape[1] // RET_QK_DIM):
            cs = slice(s * RET_QK_DIM, (s + 1) * RET_QK_DIM)
            o_ref[:, cs] = _rope(acc[:, cs], cos, sin).astype(o_ref.dtype)

    @pl.when(j >= rope_tiles)
    def _():
        o_ref[...] = acc.astype(o_ref.dtype)


def modproj(x, shift, scale, g, w, rows_per_mod, out_dtype, tm, tn, rope=None):
    m, d = x.shape
    n = w.shape[1]
    osz = jnp.dtype(out_dtype).itemsize
    vmem = 2 * tm * d * 4 + 2 * d * tn * 2 + 2 * tm * tn * osz + tm * d * 2 + 2 * tm * tn * 4 + VMEM_SLACK
    mod_idx = lambda i, j: ((i * tm) // rows_per_mod, 0, 0)
    in_specs = [
        pl.BlockSpec((tm, d), lambda i, j: (i, 0)),
        pl.BlockSpec((1, 1, d), mod_idx),
        pl.BlockSpec((1, 1, d), mod_idx),
        pl.BlockSpec((1, d), lambda i, j: (0, 0)),
        pl.BlockSpec((d, tn), lambda i, j: (0, j)),
    ]
    args = [x, shift, scale, g, w]
    rope_tiles = 0
    if rope is not None:
        seq_len, n_cols = rope
        rope_tiles = n_cols // tn
        cos, sin = _rope_tables(seq_len)
        tab = pl.BlockSpec((tm, RET_QK_DIM // 2), lambda i, j: (i % (seq_len // tm), 0))
        in_specs += [tab, tab]
        args += [cos, sin]
        vmem += 2 * 2 * tm * (RET_QK_DIM // 2) * 4
    return pl.pallas_call(
        functools.partial(_modproj_kernel, rope_tiles=rope_tiles),
        out_shape=jax.ShapeDtypeStruct((m, n), out_dtype),
        grid=(m // tm, n // tn),
        in_specs=in_specs,
        out_specs=pl.BlockSpec((tm, tn), lambda i, j: (i, j)),
        scratch_shapes=[pltpu.VMEM((tm, d), BF16)],
        compiler_params=_params(("arbitrary", "arbitrary"), vmem),
        name="modproj",
    )(*args)


def _head_rms(x, g):
    return x * lax.rsqrt(jnp.mean(x * x, axis=-1, keepdims=True) + EPS) * g


def _ctx_attn_kernel(q_ref, k_ref, v_ref, qg_ref, kg_ref, o_ref, nk_ref, nv_ref):
    scale = NA_HEAD_DIM ** -0.5
    for h in range(NA_HEADS):
        cs = slice(h * NA_HEAD_DIM, (h + 1) * NA_HEAD_DIM)
        q = _head_rms(q_ref[0, :, cs].astype(F32), qg_ref[...])
        k = _head_rms(k_ref[0, :, cs].astype(F32), kg_ref[...])
        v = v_ref[0, :, cs].astype(F32)
        nk_ref[0, 0, h] = k
        nv_ref[0, 0, h] = v
        s = lax.dot_general(q.astype(BF16), k.astype(BF16), (((1,), (1,)), ((), ())),
                            preferred_element_type=F32) * scale
        m = jnp.max(s, axis=-1, keepdims=True)
        e = jnp.exp(s - m)
        l = jnp.sum(e, axis=-1, keepdims=True)
        o = jnp.dot(e.astype(BF16), v.astype(BF16), preferred_element_type=F32) / l
        o_ref[0, :, cs] = o.astype(o_ref.dtype)


def ctx_attention(p, q_gain, k_gain):
    b, t, _ = p.shape
    cache_shape = (b, 1, NA_HEADS, t, NA_HEAD_DIM)
    blk = lambda c: pl.BlockSpec((1, t, NA_WIDTH), lambda i, c=c: (i, 0, c))
    gain = pl.BlockSpec((1, NA_HEAD_DIM), lambda i: (0, 0))
    cache_spec = pl.BlockSpec((1, 1, NA_HEADS, t, NA_HEAD_DIM), lambda i: (i, 0, 0, 0, 0))
    return pl.pallas_call(
        _ctx_attn_kernel,
        out_shape=(jax.ShapeDtypeStruct((b, t, NA_WIDTH), BF16),
                   jax.ShapeDtypeStruct(cache_shape, F32),
                   jax.ShapeDtypeStruct(cache_shape, F32)),
        grid=(b,),
        in_specs=[blk(1), blk(2), blk(3), gain, gain],
        out_specs=(pl.BlockSpec((1, t, NA_WIDTH), lambda i: (i, 0, 0)), cache_spec, cache_spec),
        compiler_params=_params(("arbitrary",), 32 << 20),
        name="ctx_attention",
    )(p, p, p, q_gain, k_gain)


NA_BIAS_ROWS = 2 * NA_KH - 1
NA_BIAS_COLS = 2 * NA_KW - 1
NA_MASK_TILE = NA_BIAS_ROWS
NA_QROWS = 4
NA_KROWS = NA_KH + NA_QROWS


def _na_build_bias(rb_ref, bias_ref, head):
    shape = (GRID_W, 2 * GRID_W)
    lane = lax.broadcasted_iota(jnp.int32, shape, 1)
    qc = lax.broadcasted_iota(jnp.int32, shape, 0)
    kc = lane & (GRID_W - 1)
    start = jnp.clip(qc - NA_KW // 2, 0, GRID_W - NA_KW)
    ok = (kc >= start) & (kc < start + NA_KW)
    delta = kc - qc + (NA_KW - 1)
    left = lane < GRID_W
    base = head * (NA_BIAS_ROWS * NA_BIAS_COLS)

    def build(dr, carry):
        acc = jnp.full(shape, NEG_INF, F32)
        for dc in range(NA_BIAS_COLS):
            acc = jnp.where(delta == dc, rb_ref[base + dr * NA_BIAS_COLS + dc], acc)
        tile = jnp.where(ok, acc * LOG2_E, NEG_INF)
        bias_ref[0, dr] = jnp.where(left, tile, 0.0)
        bias_ref[1, dr] = jnp.where(left, 0.0, tile)
        return carry

    lax.fori_loop(0, NA_BIAS_ROWS, build, 0)
    bias_ref[0, NA_MASK_TILE] = jnp.where(left, NEG_INF, 0.0)
    bias_ref[1, NA_MASK_TILE] = jnp.where(left, 0.0, NEG_INF)


def _na_kernel(rb_ref, q_ref, k_ref, v_ref, kc_ref, vc_ref, qg_ref, kg_ref, o_ref,
               qs_ref, ks_ref, vs_ref, bias_ref, *, rows):
    @pl.when(pl.program_id(1) == 0)
    def _():
        _na_build_bias(rb_ref, bias_ref, pl.program_id(0))

    scale = NA_HEAD_DIM ** -0.5 * LOG2_E
    qs_ref[...] = (_head_rms(q_ref[0].astype(F32), qg_ref[...]) * scale).astype(BF16)
    ks_ref[...] = _head_rms(k_ref[0].astype(F32), kg_ref[...]).astype(BF16)
    vs_ref[...] = v_ref[0].astype(BF16)
    kc = kc_ref[0, 0, 0].astype(BF16)
    vc = vc_ref[0, 0, 0].astype(BF16)
    nt = (((1,), (1,)), ((), ()))
    nq = NA_QROWS * GRID_W
    nk = NA_KROWS * GRID_W

    def tile_index(r, rs, kr):
        inside = (kr >= rs) & (kr < rs + NA_KH)
        return jnp.where(inside, kr - r + (NA_KH - 1), NA_MASK_TILE)

    def body(blk, carry):
        r0 = blk * NA_QROWS
        k0 = jnp.clip(r0 - NA_KH // 2, 0, rows - NA_KROWS)
        q = qs_ref[pl.ds(pl.multiple_of(r0 * GRID_W, nq), nq), :]
        kw = ks_ref[pl.ds(pl.multiple_of(k0 * GRID_W, GRID_W), nk), :]
        vw = vs_ref[pl.ds(pl.multiple_of(k0 * GRID_W, GRID_W), nk), :]
        bias_rows = []
        for i in range(NA_QROWS):
            r = r0 + i
            rs = jnp.clip(r - NA_KH // 2, 0, rows - NA_KH)
            pairs = [bias_ref[0, tile_index(r, rs, k0 + 2 * jp)]
                     + bias_ref[1, tile_index(r, rs, k0 + 2 * jp + 1)]
                     for jp in range(NA_KROWS // 2)]
            bias_rows.append(jnp.concatenate(pairs, axis=-1))
        bias = jnp.concatenate(bias_rows, axis=0)
        s_w = lax.dot_general(q, kw, nt, preferred_element_type=F32) + bias
        s_c = lax.dot_general(q, kc, nt, preferred_element_type=F32)
        m = jnp.maximum(jnp.max(s_w, axis=-1, keepdims=True), jnp.max(s_c, axis=-1, keepdims=True))
        e_w = jnp.exp2(s_w - m)
        e_c = jnp.exp2(s_c - m)
        l = jnp.sum(e_w, axis=-1, keepdims=True) + jnp.sum(e_c, axis=-1, keepdims=True)
        o = (jnp.dot(e_w.astype(BF16), vw, preferred_element_type=F32)
             + jnp.dot(e_c.astype(BF16), vc, preferred_element_type=F32)) / l
        o_ref[0, pl.ds(pl.multiple_of(r0 * GRID_W, nq), nq), :] = o.astype(o_ref.dtype)
        return carry

    lax.fori_loop(0, rows // NA_QROWS, body, 0, unroll=2)


def na_attention(p, cache_k, cache_v, layer_j, rel_bias, q_gain, k_gain):
    b, t, _ = p.shape
    past = cache_k.shape[3]
    rows = t // GRID_W
    hd = NA_HEAD_DIM
    col0 = FNET_WIDTH // hd
    blk = lambda c: pl.BlockSpec((1, t, hd), lambda h, i, rb, c=c: (i, 0, col0 + c * NA_HEADS + h))
    cache_spec = pl.BlockSpec((1, 1, 1, past, hd), lambda h, i, rb: (i, layer_j, h, 0, 0))
    gain = pl.BlockSpec((1, hd), lambda h, i, rb: (0, 0))
    return pl.pallas_call(
        functools.partial(_na_kernel, rows=rows),
        out_shape=jax.ShapeDtypeStruct((b, t, NA_WIDTH), BF16),
        grid_spec=pltpu.PrefetchScalarGridSpec(
            num_scalar_prefetch=1,
            grid=(NA_HEADS, b),
            in_specs=[blk(0), blk(1), blk(2), cache_spec, cache_spec, gain, gain],
            out_specs=pl.BlockSpec((1, t, hd), lambda h, i, rb: (i, 0, h)),
            scratch_shapes=[pltpu.VMEM((t, hd), BF16)] * 3
            + [pltpu.VMEM((2, NA_BIAS_ROWS + 1, GRID_W, 2 * GRID_W), F32)],
        ),
        compiler_params=_params(("arbitrary", "arbitrary"), 40 << 20),
        name="na_attention",
    )(rel_bias.astype(F32).reshape(-1), p, p, p, cache_k, cache_v, q_gain, k_gain)


FNET_BLOCK = 16


def _fnet_tables(n1, n2):
    t_len = n1 * n2
    tb = FNET_BLOCK
    eye = np.eye(tb)
    k1 = np.arange(n1)
    ang1 = (2.0 * np.pi / n1) * ((k1[:, None] * k1[None, :]) % n1)
    f1 = np.concatenate([np.cos(ang1), -np.sin(ang1)], axis=0) / math.sqrt(t_len)
    rows_mat = np.kron(f1, eye)
    t2 = np.arange(n2)
    tw = (2.0 * np.pi / t_len) * (k1[:, None] * t2[None, :])
    tw = tw.reshape(n1, n2 // tb, tb).transpose(1, 0, 2).reshape(n2 // tb, n1 * tb, 1)
    c = np.arange(FNET_GROUP_DIM)
    ang3 = (2.0 * np.pi / FNET_GROUP_DIM) * ((c[:, None] * c[None, :]) % FNET_GROUP_DIM)
    c3, s3 = np.cos(ang3), np.sin(ang3)
    chan = np.block([[c3, -s3], [s3, c3]]) / math.sqrt(FNET_GROUP_DIM)
    ang2 = (2.0 * np.pi / n2) * ((t2[:, None] * t2[None, :]) % n2)
    cols_re = np.einsum("pq,kt->kpqt", eye, np.cos(ang2)).reshape(n2 * tb, tb * n2)
    cols_im = np.einsum("pq,kt->kpqt", eye, np.sin(ang2)).reshape(n2 * tb, tb * n2)
    f32 = lambda x: jnp.asarray(x, dtype=F32)
    bf = lambda x: f32(x).astype(BF16)
    return bf(rows_mat), f32(np.cos(tw)), f32(np.sin(tw)), bf(chan), bf(cols_re), bf(cols_im)


def _fnet_rows_kernel(x_ref, mat_ref, twc_ref, tws_ref, chan_ref, vr_ref, vi_ref):
    _, n1, tb, width = x_ref.shape
    rows = n1 * tb
    gd = FNET_GROUP_DIM
    x = x_ref[0].reshape(rows, width).astype(BF16)
    h = jnp.dot(mat_ref[...], x, preferred_element_type=F32)
    hr, hi = h[:rows], h[rows:]
    c, s = twc_ref[0], tws_ref[0]
    gr = (hr * c + hi * s).astype(BF16)
    gi = (hi * c - hr * s).astype(BF16)
    stacked = jnp.concatenate(
        [jnp.concatenate([gr[:, g * gd:(g + 1) * gd], gi[:, g * gd:(g + 1) * gd]], axis=1)
         for g in range(FNET_GROUPS)], axis=0)
    v = jnp.dot(stacked, chan_ref[...], preferred_element_type=F32)
    for g in range(FNET_GROUPS):
        part = v[g * rows:(g + 1) * rows]
        vr_ref[0, :, :, g * gd:(g + 1) * gd] = part[:, :gd].reshape(n1, tb, gd).astype(vr_ref.dtype)
        vi_ref[0, :, :, g * gd:(g + 1) * gd] = part[:, gd:].reshape(n1, tb, gd).astype(vi_ref.dtype)


def _fnet_cols_kernel(vr_ref, vi_ref, re_ref, im_ref, o_ref):
    _, kb, n2, width = vr_ref.shape
    vr = vr_ref[0].reshape(kb * n2, width)
    vi = vi_ref[0].reshape(kb * n2, width)
    y = (jnp.dot(re_ref[...], vr, preferred_element_type=F32)
         + jnp.dot(im_ref[...], vi, preferred_element_type=F32))
    o_ref[0] = y.reshape(n2, kb, width).astype(o_ref.dtype)


def fourier_mix(p, n1, n2):
    b, t, width = p.shape
    tb = FNET_BLOCK
    w = FNET_WIDTH
    rows_mat, twc, tws, chan, cols_re, cols_im = _fnet_tables(n1, n2)
    const = lambda shape: pl.BlockSpec(shape, lambda i, s: (0,) * len(shape))
    tw_spec = pl.BlockSpec((1, n1 * tb, 1), lambda i, s: (s, 0, 0))
    v_shape = jax.ShapeDtypeStruct((b, n1, n2, w), BF16)
    v_spec = pl.BlockSpec((1, n1, tb, w), lambda i, s: (i, 0, s, 0))
    isz = p.dtype.itemsize
    vmem = (2 * n1 * tb * w * isz + 2 * rows_mat.size * 2 + 4 * n1 * tb * w * 2
            + 6 * n1 * tb * w * 4 + VMEM_SLACK)
    vr, vi = pl.pallas_call(
        _fnet_rows_kernel,
        out_shape=(v_shape, v_shape),
        grid=(b, n2 // tb),
        in_specs=[pl.BlockSpec((1, n1, tb, w), lambda i, s: (i, 0, s, 0)),
                  const(rows_mat.shape), tw_spec, tw_spec, const(chan.shape)],
        out_specs=(v_spec, v_spec),
        compiler_params=_params(("arbitrary", "arbitrary"), vmem),
        name="fnet_rows",
    )(p.reshape(b, n1, n2, width), rows_mat, twc, tws, chan)
    v_in = pl.BlockSpec((1, tb, n2, w), lambda i, s: (i, s, 0, 0))
    vmem = 2 * 2 * tb * n2 * w * 2 + 2 * 2 * cols_re.size * 2 + 2 * n2 * tb * w * 2 + 3 * n2 * tb * w * 4 + VMEM_SLACK
    out = pl.pallas_call(
        _fnet_cols_kernel,
        out_shape=jax.ShapeDtypeStruct((b, n2, n1, w), BF16),
        grid=(b, n1 // tb),
        in_specs=[v_in, v_in, const(cols_re.shape), const(cols_im.shape)],
        out_specs=pl.BlockSpec((1, n2, tb, w), lambda i, s: (i, 0, s, 0)),
        compiler_params=_params(("arbitrary", "arbitrary"), vmem),
        name="fnet_cols",
    )(vr, vi, cols_re, cols_im)
    return out.reshape(b, t, w)


def _even_out_kernel(f_ref, a_ref, wf_ref, wa_ref, x_ref, gate_ref, o_ref):
    y = (jnp.dot(f_ref[...], wf_ref[...], preferred_element_type=F32)
         + jnp.dot(a_ref[...], wa_ref[...], preferred_element_type=F32))
    o_ref[...] = x_ref[...] + gate_ref[0] * y


def even_out(f, a, w, x, gate, rows_per_mod, tm, tn):
    m, d = x.shape
    kf = f.shape[1]
    vmem = 2 * 2 * tm * kf * 2 + 2 * 2 * kf * tn * 2 + 4 * tm * tn * 4 + 2 * tm * tn * 4 + VMEM_SLACK
    return pl.pallas_call(
        _even_out_kernel,
        out_shape=jax.ShapeDtypeStruct((m, d), F32),
        grid=(d // tn, m // tm),
        in_specs=[
            pl.BlockSpec((tm, kf), lambda j, i: (i, 0)),
            pl.BlockSpec((tm, kf), lambda j, i: (i, 0)),
            pl.BlockSpec((kf, tn), lambda j, i: (0, j)),
            pl.BlockSpec((kf, tn), lambda j, i: (1, j)),
            pl.BlockSpec((tm, tn), lambda j, i: (i, j)),
            pl.BlockSpec((1, 1, tn), lambda j, i: ((i * tm) // rows_per_mod, 0, j)),
        ],
        out_specs=pl.BlockSpec((tm, tn), lambda j, i: (i, j)),
        compiler_params=_params(("arbitrary", "arbitrary"), vmem),
        name="even_out",
    )(f, a, w, w, x, gate)


def _gated_group_norm(o, g):
    o = o.astype(F32)
    oc = o - jnp.mean(o, axis=-1, keepdims=True)
    gn = oc * lax.rsqrt(jnp.mean(oc * oc, axis=-1, keepdims=True) + EPS)
    hg = 0.5 * g.astype(F32)
    return (hg + hg * jnp.tanh(hg)) * gn


def _odd_out_kernel(of_ref, ob_ref, gf_ref, gb_ref, w_ref, x_ref, gate_ref, o_ref):
    acc = None
    for h in range(RET_HEADS):
        cs = slice(h * RET_V_DIM, (h + 1) * RET_V_DIM)
        y = (_gated_group_norm(of_ref[:, cs], gf_ref[:, cs])
             + _gated_group_norm(ob_ref[:, cs], gb_ref[:, cs])).astype(BF16)
        part = jnp.dot(y, w_ref[cs, :], preferred_element_type=F32)
        acc = part if acc is None else acc + part
    o_ref[...] = x_ref[...] + gate_ref[0] * acc


def odd_out(o_f, o_b, p, w, x, gate, rows_per_mod, tm):
    m, d = x.shape
    k = w.shape[0]
    gcol = (2 * RET_QK_WIDTH + RET_V_WIDTH) // RET_V_WIDTH
    vmem = (2 * 4 * tm * k * 2 + k * d * 2 + 4 * tm * d * 4 + 2 * tm * d * 4
            + 6 * tm * RET_V_DIM * 4 + VMEM_SLACK)
    act = lambda c: pl.BlockSpec((tm, k), lambda i, c=c: (i, c))
    return pl.pallas_call(
        _odd_out_kernel,
        out_shape=jax.ShapeDtypeStruct((m, d), F32),
        grid=(m // tm,),
        in_specs=[
            act(0), act(0), act(gcol), act(gcol + 1),
            pl.BlockSpec((k, d), lambda i: (0, 0), pipeline_mode=pl.Buffered(1)),
            pl.BlockSpec((tm, d), lambda i: (i, 0)),
            pl.BlockSpec((1, 1, d), lambda i: ((i * tm) // rows_per_mod, 0, 0)),
        ],
        out_specs=pl.BlockSpec((tm, d), lambda i: (i, 0)),
        compiler_params=_params(("arbitrary",), vmem),
        name="odd_out",
    )(o_f, o_b, p, p, w, x, gate)


FFN_RING = 3


def _ffn_kernel(x_ref, sh_ref, sc_ref, gate_ref, g_ref, wg_hbm, wu_hbm, wd_hbm, o_ref,
                h_ref, wg_buf, wu_buf, wd_buf, sem, *, layer, n_j, n_steps):
    j = pl.program_id(1)
    s = pl.program_id(0) * n_j + j
    tf = wg_buf.shape[2]

    def tile_copies(step):
        col = pl.multiple_of(lax.rem(step, n_j) * tf, tf)
        slot = lax.rem(step, FFN_RING)
        return (pltpu.make_async_copy(wg_hbm.at[layer, :, pl.ds(col, tf)], wg_buf.at[slot], sem.at[0, slot]),
                pltpu.make_async_copy(wu_hbm.at[layer, :, pl.ds(col, tf)], wu_buf.at[slot], sem.at[1, slot]),
                pltpu.make_async_copy(wd_hbm.at[layer, pl.ds(col, tf), :], wd_buf.at[slot], sem.at[2, slot]))

    @pl.when(s == 0)
    def _():
        for step in range(FFN_RING - 1):
            for c in tile_copies(step):
                c.start()

    @pl.when(s + (FFN_RING - 1) < n_steps)
    def _():
        for c in tile_copies(s + (FFN_RING - 1)):
            c.start()

    @pl.when(j == 0)
    def _():
        _modulate_into(h_ref, x_ref, g_ref, sh_ref, sc_ref)
        o_ref[...] = jnp.zeros_like(o_ref)

    for c in tile_copies(s):
        c.wait()
    slot = lax.rem(s, FFN_RING)
    h = h_ref[...]
    a = jnp.dot(h, wg_buf[slot], preferred_element_type=F32)
    u = jnp.dot(h, wu_buf[slot], preferred_element_type=F32)
    act = (_silu(a) * u).astype(BF16)
    o_ref[...] += jnp.dot(act, wd_buf[slot], preferred_element_type=F32)

    @pl.when(j == n_j - 1)
    def _():
        o_ref[...] = x_ref[...] + gate_ref[0] * o_ref[...]


def ffn_block(x, shift, scale, gate, g, w_gate, w_up, w_down, layer, rows_per_mod, tm, tf):
    m, d = x.shape
    f = w_gate.shape[2]
    n_i, n_j = m // tm, f // tf
    vmem = (4 * tm * d * 4 + tm * d * 2 + FFN_RING * 3 * d * tf * 2 + 4 * tm * tf * 4 + tm * d * 4
            + VMEM_SLACK)
    mod_idx = lambda i, j: ((i * tm) // rows_per_mod, 0, 0)
    mod = pl.BlockSpec((1, 1, d), mod_idx)
    hbm = pl.BlockSpec(memory_space=pl.ANY)
    return pl.pallas_call(
        functools.partial(_ffn_kernel, layer=layer, n_j=n_j, n_steps=n_i * n_j),
        out_shape=jax.ShapeDtypeStruct((m, d), F32),
        grid=(n_i, n_j),
        in_specs=[
            pl.BlockSpec((tm, d), lambda i, j: (i, 0)),
            mod, mod, mod,
            pl.BlockSpec((1, d), lambda i, j: (0, 0)),
            hbm, hbm, hbm,
        ],
        out_specs=pl.BlockSpec((tm, d), lambda i, j: (i, 0)),
        scratch_shapes=[pltpu.VMEM((tm, d), BF16),
                        pltpu.VMEM((FFN_RING, d, tf), BF16), pltpu.VMEM((FFN_RING, d, tf), BF16),
                        pltpu.VMEM((FFN_RING, tf, d), BF16),
                        pltpu.SemaphoreType.DMA((3, FFN_RING))],
        compiler_params=_params(("arbitrary", "arbitrary"), vmem),
        name="ffn_block",
    )(x, shift, scale, gate, g, w_gate, w_up, w_down)


def _retention_kernel(lg_ref, *refs, latent, n_blocks):
    if latent:
        (qf_ref, kf_ref, vf_ref, qb_ref, kb_ref, vb_ref, s0_ref,
         of_ref, ob_ref, s_ref, dec_ref, qd_ref, kd_ref) = refs
    else:
        (qf_ref, kf_ref, vf_ref, qb_ref, kb_ref, vb_ref,
         of_ref, ob_ref, sfin_ref, s_ref, dec_ref, qd_ref, kd_ref) = refs
    h0 = pl.program_id(0) * RET_HEADS_PER_STEP
    i = pl.program_id(2)
    c = RET_BLOCK
    dk, dv = RET_QK_DIM, RET_V_DIM
    scale = RET_QK_DIM ** -0.5
    scans = [(d, hh) for d in range(2) for hh in range(RET_HEADS_PER_STEP)]

    @pl.when((pl.program_id(1) == 0) & (i == 0))
    def _():
        row = lax.broadcasted_iota(jnp.int32, (c, c), 0).astype(F32)
        col = lax.broadcasted_iota(jnp.int32, (c, c), 1).astype(F32)
        t_idx = lax.broadcasted_iota(jnp.int32, (c, 1), 0).astype(F32)
        for d, hh in scans:
            lg = lg_ref[d, h0 + hh]
            diff = row - col if d == 0 else col - row
            dec_ref[d, hh] = jnp.where(diff >= 0, jnp.exp(lg * jnp.maximum(diff, 0.0)), 0.0) * scale
            pos = t_idx if d == 0 else (c - 1.0) - t_idx
            qd_ref[d, hh] = jnp.exp(lg * (pos + 1.0))
            kd_ref[d, hh] = jnp.exp(lg * ((c - 1.0) - pos)) * scale

    @pl.when(i == 0)
    def _():
        for d, hh in scans:
            s_ref[d, hh] = (_pair_order_rows(s0_ref[0, 0, d, hh]) if latent
                            else jnp.zeros((dk, dv), F32))

    nt = (((1,), (1,)), ((), ()))
    tn = (((0,), (0,)), ((), ()))
    chunk_refs = ((qf_ref, kf_ref, vf_ref, of_ref), (qb_ref, kb_ref, vb_ref, ob_ref))
    for d, hh in scans:
        q_ref, k_ref, v_ref, o_ref = chunk_refs[d]
        q = q_ref[0, :, hh * dk:(hh + 1) * dk]
        k = k_ref[0, :, hh * dk:(hh + 1) * dk]
        v = v_ref[0, :, hh * dv:(hh + 1) * dv]
        a = lax.dot_general(q, k, nt, preferred_element_type=F32)
        inner_w = (a * dec_ref[d, hh]).astype(BF16)
        q_dec = (q.astype(F32) * qd_ref[d, hh]).astype(BF16)
        k_dec = (k.astype(F32) * kd_ref[d, hh]).astype(BF16)
        s_old = s_ref[d, hh]
        o = (jnp.dot(inner_w, v, preferred_element_type=F32)
             + jnp.dot(q_dec, s_old.astype(BF16), preferred_element_type=F32))
        s_new = (s_old * jnp.exp(lg_ref[d, h0 + hh] * c)
                 + lax.dot_general(k_dec, v, tn, preferred_element_type=F32))
        s_ref[d, hh] = s_new
        o_ref[0, :, hh * dv:(hh + 1) * dv] = o.astype(o_ref.dtype)
        if not latent:
            @pl.when(i == n_blocks - 1)
            def _(d=d, hh=hh, s_new=s_new):
                sfin_ref[0, 0, d, hh] = _pair_order_rows(s_new)


def retention(p, log_gamma, state0, layer_j):
    b, t, _ = p.shape
    latent = state0 is not None
    c = RET_BLOCK
    nb = t // c
    hps = RET_HEADS_PER_STEP
    dk, dv, nh = hps * RET_QK_DIM, hps * RET_V_DIM, RET_HEADS // hps

    def chunk_specs(blk_of):
        return [
            pl.BlockSpec((1, c, dk), lambda h, bi, i, lg: (bi, blk_of(i), h)),
            pl.BlockSpec((1, c, dk), lambda h, bi, i, lg: (bi, blk_of(i), nh + h)),
            pl.BlockSpec((1, c, dv), lambda h, bi, i, lg: (bi, blk_of(i), nh + h)),
        ]

    fwd = lambda i: i
    bwd = lambda i: nb - 1 - i
    in_specs = chunk_specs(fwd) + chunk_specs(bwd)
    args = [p] * 6
    o_shape = jax.ShapeDtypeStruct((b, t, RET_V_WIDTH), BF16)
    out_shape = [o_shape, o_shape]
    out_specs = [pl.BlockSpec((1, c, dv), lambda h, bi, i, lg: (bi, fwd(i), h)),
                 pl.BlockSpec((1, c, dv), lambda h, bi, i, lg: (bi, bwd(i), h))]
    state_dims = (RET_QK_DIM, RET_V_DIM)
    state_block = (1, 1, 2, hps) + state_dims
    if latent:
        in_specs.append(pl.BlockSpec(state_block, lambda h, bi, i, lg: (bi, layer_j, 0, h, 0, 0)))
        args.append(state0)
    else:
        out_shape.append(jax.ShapeDtypeStruct((b, 1, 2, RET_HEADS) + state_dims, F32))
        out_specs.append(pl.BlockSpec(state_block, lambda h, bi, i, lg: (bi, 0, 0, h, 0, 0)))
    return pl.pallas_call(
        functools.partial(_retention_kernel, latent=latent, n_blocks=nb),
        out_shape=tuple(out_shape),
        grid_spec=pltpu.PrefetchScalarGridSpec(
            num_scalar_prefetch=1,
            grid=(nh, b, nb),
            in_specs=in_specs,
            out_specs=tuple(out_specs),
            scratch_shapes=[pltpu.VMEM((2, hps) + state_dims, F32), pltpu.VMEM((2, hps, c, c), F32),
                            pltpu.VMEM((2, hps, c, 1), F32), pltpu.VMEM((2, hps, c, 1), F32)],
        ),
        compiler_params=_params(("arbitrary",) * 3, 40 << 20),
        name="retention",
    )(log_gamma, *args)


def _trunk(x, mods, ctx_k, ctx_v, ctx_state, wts, p_dtype):
    b, t, d = x.shape
    is_ctx = ctx_k is None
    nbm = mods.shape[1]
    rows_per_mod = (b * t) // nbm
    x2 = x.reshape(b * t, d)
    new_k = new_v = new_s = None
    for i in range(DEPTH):
        mod = [mods[i, :, k][:, None, :] for k in range(6)]
        j = i // 2
        if i % 2 == 0:
            p = modproj(x2, mod[0], mod[1], wts["norm_g"][i, 0][None], wts["even_w_in"][j],
                        rows_per_mod, p_dtype, tm=1024, tn=1024).reshape(b, t, EVEN_IN)
            qg, kg = wts["even_q_norm"][j][None], wts["even_k_norm"][j][None]
            if is_ctx:
                attn, new_k, new_v = ctx_attention(p, qg, kg)
            else:
                attn = na_attention(p, ctx_k, ctx_v, j, wts["na_rel_bias"][j], qg, kg)
            n2 = GRID_W if not is_ctx else math.isqrt(t)
            fm = fourier_mix(p, t // n2, n2)
            x2 = even_out(fm.reshape(b * t, FNET_WIDTH), attn.reshape(b * t, NA_WIDTH),
                          wts["even_w_out"][j], x2, mod[2], rows_per_mod, tm=1024, tn=1024)
        else:
            rope = (t, 2 * RET_QK_WIDTH) if not is_ctx else None
            p = modproj(x2, mod[0], mod[1], wts["norm_g"][i, 0][None], wts["odd_w_in"][j],
                        rows_per_mod, BF16, tm=1024, tn=1024, rope=rope)
            lg = jax.nn.log_sigmoid(wts["ret_decay_logit"][j].astype(F32))
            if is_ctx:
                o_f, o_b, new_s = retention(p.reshape(b, t, ODD_IN), lg, None, j)
            else:
                o_f, o_b = retention(p.reshape(b, t, ODD_IN), lg, ctx_state, j)
            x2 = odd_out(o_f.reshape(b * t, RET_V_WIDTH), o_b.reshape(b * t, RET_V_WIDTH), p,
                         wts["odd_w_out"][j], x2, mod[2], rows_per_mod, tm=256)
        x2 = ffn_block(x2, mod[3], mod[4], mod[5], wts["norm_g"][i, 1][None], wts["ffn_w_gate"],
                       wts["ffn_w_up"], wts["ffn_w_down"], i, rows_per_mod, tm=512, tf=512)
    return x2.reshape(b, t, d), new_k, new_v, new_s


def kernel(x_prompt, x_sample, cache_k, cache_v, state_ret, c, c_ctx, ada_w, ada_b, norm_g,
           even_w_in, even_q_norm, even_k_norm, na_rel_bias, even_w_out, odd_w_in, ret_decay_logit,
           odd_w_out, ffn_w_gate, ffn_w_up, ffn_w_down):
    nb_lat = c.shape[0]
    cond = jnp.concatenate(
        [c, c_ctx[None, :], jnp.zeros((MOD_ROWS - nb_lat - 1, D_MODEL), F32)], axis=0)
    mods = adaln_all(cond, ada_w, ada_b).reshape(DEPTH, MOD_ROWS, 6, D_MODEL)
    wts = dict(
        norm_g=norm_g, even_q_norm=even_q_norm, even_k_norm=even_k_norm, na_rel_bias=na_rel_bias,
        ret_decay_logit=ret_decay_logit,
        even_w_in=even_w_in.astype(BF16), even_w_out=even_w_out.astype(BF16),
        odd_w_in=odd_w_prep(odd_w_in), odd_w_out=odd_w_out.astype(BF16),
        ffn_w_gate=ffn_w_gate.astype(BF16), ffn_w_up=ffn_w_up.astype(BF16),
        ffn_w_down=ffn_w_down.astype(BF16),
    )
    y_prompt, new_k, new_v, new_s = _trunk(
        x_prompt, mods[:, nb_lat:nb_lat + 1], None, None, None, wts, F32)
    y_sample, _, _, _ = _trunk(x_sample, mods[:, :nb_lat], cache_k, cache_v, state_ret, wts, BF16)
    return (y_prompt, y_sample, new_k, new_v, new_s)
```

```python
import functools
import math

import numpy as np
import jax
import jax.numpy as jnp
from jax import lax
from jax.experimental import pallas as pl
from jax.experimental.pallas import tpu as pltpu

F32 = jnp.float32
BF16 = jnp.bfloat16

D_MODEL = 2048
DEPTH = 2
GRID_W = 64
EPS = 1e-6
NEG_INF = -1e30
LOG2_E = math.log2(math.e)
FNET_GROUPS = 8
FNET_GROUP_DIM = 128
FNET_WIDTH = 1024
NA_HEADS = 8
NA_HEAD_DIM = 128
NA_WIDTH = 1024
NA_KH = 8
NA_KW = 16
EVEN_IN = FNET_WIDTH + 3 * NA_WIDTH
RET_HEADS = 8
RET_QK_DIM = 256
RET_V_DIM = 512
RET_QK_WIDTH = RET_HEADS * RET_QK_DIM
RET_V_WIDTH = RET_HEADS * RET_V_DIM
ODD_IN = 2 * RET_QK_WIDTH + 3 * RET_V_WIDTH
RET_BLOCK = 256
RET_HEADS_PER_STEP = 4
ROPE_BASE = 10000.0
D_FF = 5632
MOD_ROWS = 8
MOD_CHUNK = 128

V7X_VMEM_BUDGET = 56 * 1024 * 1024
VMEM_SLACK = 8 * 1024 * 1024


def _params(semantics, vmem_bytes):
    return pltpu.CompilerParams(dimension_semantics=semantics,
                                vmem_limit_bytes=min(int(vmem_bytes), V7X_VMEM_BUDGET))


def _silu(x):
    return x * (1.0 / (1.0 + jnp.exp(-x)))


def _adaln_kernel(c_ref, w_ref, b_ref, o_ref):
    s = _silu(c_ref[...]).astype(BF16)
    w = w_ref[0].astype(BF16)
    o_ref[0] = jnp.dot(s, w, preferred_element_type=F32) + b_ref[0]


def adaln_all(cond, ada_w, ada_b):
    n = ada_w.shape[-1]
    tn = 1024
    return pl.pallas_call(
        _adaln_kernel,
        out_shape=jax.ShapeDtypeStruct((DEPTH, MOD_ROWS, n), F32),
        grid=(DEPTH, n // tn),
        in_specs=[
            pl.BlockSpec((MOD_ROWS, D_MODEL), lambda l, j: (0, 0)),
            pl.BlockSpec((1, D_MODEL, tn), lambda l, j: (l, 0, j)),
            pl.BlockSpec((1, 1, tn), lambda l, j: (l, 0, j)),
        ],
        out_specs=pl.BlockSpec((1, MOD_ROWS, tn), lambda l, j: (l, 0, j)),
        compiler_params=_params(("arbitrary", "arbitrary"), 40 << 20),
        name="adaln",
    )(cond, ada_w, ada_b.reshape(DEPTH, 1, n))


def _modulate_into(h_ref, x_ref, g_ref, sh_ref, sc_ref):
    shift = sh_ref[0]
    gain = g_ref[...] * (1.0 + sc_ref[0])

    def body(c, carry):
        rows = pl.ds(pl.multiple_of(c * MOD_CHUNK, MOD_CHUNK), MOD_CHUNK)
        x = x_ref[rows, :]
        ms = jnp.mean(x * x, axis=-1, keepdims=True)
        h_ref[rows, :] = (x * lax.rsqrt(ms + EPS) * gain + shift).astype(h_ref.dtype)
        return carry

    lax.fori_loop(0, x_ref.shape[0] // MOD_CHUNK, body, 0, unroll=2)


ROPE_QUARTER = RET_QK_DIM // 4


def _pair_order(x, axis):
    q = ROPE_QUARTER
    parts = [lax.slice_in_dim(x, a * q, (a + 1) * q, axis=axis) for a in (0, 2, 1, 3)]
    return jnp.concatenate(parts, axis=axis)


def _pair_order_rows(s):
    return _pair_order(s, 0)


def _odd_w_prep_kernel(w_ref, o_ref):
    for head in range(2 * RET_HEADS):
        cs = slice(head * RET_QK_DIM, (head + 1) * RET_QK_DIM)
        o_ref[0, :, cs] = _pair_order(w_ref[0, :, cs], 1).astype(o_ref.dtype)
    rest = 2 * RET_QK_WIDTH
    o_ref[0, :, rest:] = w_ref[0, :, rest:].astype(o_ref.dtype)


def odd_w_prep(w):
    n, d, width = w.shape
    tr = 128
    return pl.pallas_call(
        _odd_w_prep_kernel,
        out_shape=jax.ShapeDtypeStruct(w.shape, BF16),
        grid=(n, d // tr),
        in_specs=[pl.BlockSpec((1, tr, width), lambda l, i: (l, i, 0))],
        out_specs=pl.BlockSpec((1, tr, width), lambda l, i: (l, i, 0)),
        compiler_params=_params(("arbitrary", "arbitrary"), 2 * tr * width * 6 + 3 * tr * width * 4 + VMEM_SLACK),
        name="odd_w_prep",
    )(w)


def _rope_tables(t_len):
    inv = ROPE_BASE ** (-jnp.arange(ROPE_QUARTER, dtype=F32) / ROPE_QUARTER)
    t = jnp.arange(t_len)
    ang = jnp.concatenate([(t // GRID_W).astype(F32)[:, None] * inv[None, :],
                           (t % GRID_W).astype(F32)[:, None] * inv[None, :]], axis=-1)
    return jnp.cos(ang), jnp.sin(ang)


def _rope(x, cos, sin):
    half = RET_QK_DIM // 2
    x1, x2 = x[:, :half], x[:, half:]
    return jnp.concatenate([x1 * cos - x2 * sin, x1 * sin + x2 * cos], axis=-1)


def _modproj_kernel(*refs, rope_tiles):
    if rope_tiles:
        x_ref, sh_ref, sc_ref, g_ref, w_ref, cos_ref, sin_ref, o_ref, h_ref = refs
    else:
        x_ref, sh_ref, sc_ref, g_ref, w_ref, o_ref, h_ref = refs
    j = pl.program_id(1)

    @pl.when(j == 0)
    def _():
        _modulate_into(h_ref, x_ref, g_ref, sh_ref, sc_ref)

    acc = jnp.dot(h_ref[...], w_ref[...], preferred_element_type=F32)
    if not rope_tiles:
        o_ref[...] = acc.astype(o_ref.dtype)
        return

    @pl.when(j < rope_tiles)
    def _():
        cos = cos_ref[...]
        sin = sin_ref[...]
        for s in range(acc.shape[1] // RET_QK_DIM):
            cs = slice(s * RET_QK_DIM, (s + 1) * RET_QK_DIM)
            o_ref[:, cs] = _rope(acc[:, cs], cos, sin).astype(o_ref.dtype)

    @pl.when(j >= rope_tiles)
    def _():
        o_ref[...] = acc.astype(o_ref.dtype)


def modproj(x, shift, scale, g, w, rows_per_mod, out_dtype, tm, tn, rope=None):
    m, d = x.shape
    n = w.shape[1]
    osz = jnp.dtype(out_dtype).itemsize
    vmem = 2 * tm * d * 4 + 2 * d * tn * 2 + 2 * tm * tn * osz + tm * d * 2 + 2 * tm * tn * 4 + VMEM_SLACK
    mod_idx = lambda i, j: ((i * tm) // rows_per_mod, 0, 0)
    in_specs = [
        pl.BlockSpec((tm, d), lambda i, j: (i, 0)),
        pl.BlockSpec((1, 1, d), mod_idx),
        pl.BlockSpec((1, 1, d), mod_idx),
        pl.BlockSpec((1, d), lambda i, j: (0, 0)),
        pl.BlockSpec((d, tn), lambda i, j: (0, j)),
    ]
    args = [x, shift, scale, g, w]
    rope_tiles = 0
    if rope is not None:
        seq_len, n_cols = rope
        rope_tiles = n_cols // tn
        cos, sin = _rope_tables(seq_len)
        tab = pl.BlockSpec((tm, RET_QK_DIM // 2), lambda i, j: (i % (seq_len // tm), 0))
        in_specs += [tab, tab]
        args += [cos, sin]
        vmem += 2 * 2 * tm * (RET_QK_DIM // 2) * 4
    return pl.pallas_call(
        functools.partial(_modproj_kernel, rope_tiles=rope_tiles),
        out_shape=jax.ShapeDtypeStruct((m, n), out_dtype),
        grid=(m // tm, n // tn),
        in_specs=in_specs,
        out_specs=pl.BlockSpec((tm, tn), lambda i, j: (i, j)),
        scratch_shapes=[pltpu.VMEM((tm, d), BF16)],
        compiler_params=_params(("arbitrary", "arbitrary"), vmem),
        name="modproj",
    )(*args)


def _head_rms(x, g):
    return x * lax.rsqrt(jnp.mean(x * x, axis=-1, keepdims=True) + EPS) * g


def _ctx_attn_kernel(q_ref, k_ref, v_ref, qg_ref, kg_ref, o_ref, nk_ref, nv_ref):
    scale = NA_HEAD_DIM ** -0.5
    for h in range(NA_HEADS):
        cs = slice(h * NA_HEAD_DIM, (h + 1) * NA_HEAD_DIM)
        q = _head_rms(q_ref[0, :, cs].astype(F32), qg_ref[...])
        k = _head_rms(k_ref[0, :, cs].astype(F32), kg_ref[...])
        v = v_ref[0, :, cs].astype(F32)
        nk_ref[0, 0, h] = k
        nv_ref[0, 0, h] = v
        s = lax.dot_general(q.astype(BF16), k.astype(BF16), (((1,), (1,)), ((), ())),
                            preferred_element_type=F32) * scale
        m = jnp.max(s, axis=-1, keepdims=True)
        e = jnp.exp(s - m)
        l = jnp.sum(e, axis=-1, keepdims=True)
        o = jnp.dot(e.astype(BF16), v.astype(BF16), preferred_element_type=F32) / l
        o_ref[0, :, cs] = o.astype(o_ref.dtype)


def ctx_attention(p, q_gain, k_gain):
    b, t, _ = p.shape
    cache_shape = (b, 1, NA_HEADS, t, NA_HEAD_DIM)
    blk = lambda c: pl.BlockSpec((1, t, NA_WIDTH), lambda i, c=c: (i, 0, c))
    gain = pl.BlockSpec((1, NA_HEAD_DIM), lambda i: (0, 0))
    cache_spec = pl.BlockSpec((1, 1, NA_HEADS, t, NA_HEAD_DIM), lambda i: (i, 0, 0, 0, 0))
    return pl.pallas_call(
        _ctx_attn_kernel,
        out_shape=(jax.ShapeDtypeStruct((b, t, NA_WIDTH), BF16),
                   jax.ShapeDtypeStruct(cache_shape, F32),
                   jax.ShapeDtypeStruct(cache_shape, F32)),
        grid=(b,),
        in_specs=[blk(1), blk(2), blk(3), gain, gain],
        out_specs=(pl.BlockSpec((1, t, NA_WIDTH), lambda i: (i, 0, 0)), cache_spec, cache_spec),
        compiler_params=_params(("arbitrary",), 32 << 20),
        name="ctx_attention",
    )(p, p, p, q_gain, k_gain)


NA_BIAS_ROWS = 2 * NA_KH - 1
NA_BIAS_COLS = 2 * NA_KW - 1
NA_MASK_TILE = NA_BIAS_ROWS
NA_QROWS = 4
NA_KROWS = NA_KH + NA_QROWS


def _na_build_bias(rb_ref, bias_ref, head):
    shape = (GRID_W, 2 * GRID_W)
    lane = lax.broadcasted_iota(jnp.int32, shape, 1)
    qc = lax.broadcasted_iota(jnp.int32, shape, 0)
    kc = lane & (GRID_W - 1)
    start = jnp.clip(qc - NA_KW // 2, 0, GRID_W - NA_KW)
    ok = (kc >= start) & (kc < start + NA_KW)
    delta = kc - qc + (NA_KW - 1)
    left = lane < GRID_W
    base = head * (NA_BIAS_ROWS * NA_BIAS_COLS)

    def build(dr, carry):
        acc = jnp.full(shape, NEG_INF, F32)
        for dc in range(NA_BIAS_COLS):
            acc = jnp.where(delta == dc, rb_ref[base + dr * NA_BIAS_COLS + dc], acc)
        tile = jnp.where(ok, acc * LOG2_E, NEG_INF)
        bias_ref[0, dr] = jnp.where(left, tile, 0.0)
        bias_ref[1, dr] = jnp.where(left, 0.0, tile)
        return carry

    lax.fori_loop(0, NA_BIAS_ROWS, build, 0)
    bias_ref[0, NA_MASK_TILE] = jnp.where(left, NEG_INF, 0.0)
    bias_ref[1, NA_MASK_TILE] = jnp.where(left, 0.0, NEG_INF)


def _na_kernel(rb_ref, q_ref, k_ref, v_ref, kc_ref, vc_ref, qg_ref, kg_ref, o_ref,
               qs_ref, ks_ref, vs_ref, bias_ref, *, rows):
    @pl.when(pl.program_id(1) == 0)
    def _():
        _na_build_bias(rb_ref, bias_ref, pl.program_id(0))

    scale = NA_HEAD_DIM ** -0.5 * LOG2_E
    qs_ref[...] = (_head_rms(q_ref[0].astype(F32), qg_ref[...]) * scale).astype(BF16)
    ks_ref[...] = _head_rms(k_ref[0].astype(F32), kg_ref[...]).astype(BF16)
    vs_ref[...] = v_ref[0].astype(BF16)
    kc = kc_ref[0, 0, 0].astype(BF16)
    vc = vc_ref[0, 0, 0].astype(BF16)
    nt = (((1,), (1,)), ((), ()))
    nq = NA_QROWS * GRID_W
    nk = NA_KROWS * GRID_W

    def tile_index(r, rs, kr):
        inside = (kr >= rs) & (kr < rs + NA_KH)
        return jnp.where(inside, kr - r + (NA_KH - 1), NA_MASK_TILE)

    def body(blk, carry):
        r0 = blk * NA_QROWS
        k0 = jnp.clip(r0 - NA_KH // 2, 0, rows - NA_KROWS)
        q = qs_ref[pl.ds(pl.multiple_of(r0 * GRID_W, nq), nq), :]
        kw = ks_ref[pl.ds(pl.multiple_of(k0 * GRID_W, GRID_W), nk), :]
        vw = vs_ref[pl.ds(pl.multiple_of(k0 * GRID_W, GRID_W), nk), :]
        bias_rows = []
        for i in range(NA_QROWS):
            r = r0 + i
            rs = jnp.clip(r - NA_KH // 2, 0, rows - NA_KH)
            pairs = [bias_ref[0, tile_index(r, rs, k0 + 2 * jp)]
                     + bias_ref[1, tile_index(r, rs, k0 + 2 * jp + 1)]
                     for jp in range(NA_KROWS // 2)]
            bias_rows.append(jnp.concatenate(pairs, axis=-1))
        bias = jnp.concatenate(bias_rows, axis=0)
        s_w = lax.dot_general(q, kw, nt, preferred_element_type=F32) + bias
        s_c = lax.dot_general(q, kc, nt, preferred_element_type=F32)
        m = jnp.maximum(jnp.max(s_w, axis=-1, keepdims=True), jnp.max(s_c, axis=-1, keepdims=True))
        e_w = jnp.exp2(s_w - m)
        e_c = jnp.exp2(s_c - m)
        l = jnp.sum(e_w, axis=-1, keepdims=True) + jnp.sum(e_c, axis=-1, keepdims=True)
        o = (jnp.dot(e_w.astype(BF16), vw, preferred_element_type=F32)
             + jnp.dot(e_c.astype(BF16), vc, preferred_element_type=F32)) / l
        o_ref[0, pl.ds(pl.multiple_of(r0 * GRID_W, nq), nq), :] = o.astype(o_ref.dtype)
        return carry

    lax.fori_loop(0, rows // NA_QROWS, body, 0, unroll=2)


def na_attention(p, cache_k, cache_v, layer_j, rel_bias, q_gain, k_gain):
    b, t, _ = p.shape
    past = cache_k.shape[3]
    rows = t // GRID_W
    hd = NA_HEAD_DIM
    col0 = FNET_WIDTH // hd
    blk = lambda c: pl.BlockSpec((1, t, hd), lambda h, i, rb, c=c: (i, 0, col0 + c * NA_HEADS + h))
    cache_spec = pl.BlockSpec((1, 1, 1, past, hd), lambda h, i, rb: (i, layer_j, h, 0, 0))
    gain = pl.BlockSpec((1, hd), lambda h, i, rb: (0, 0))
    return pl.pallas_call(
        functools.partial(_na_kernel, rows=rows),
        out_shape=jax.ShapeDtypeStruct((b, t, NA_WIDTH), BF16),
        grid_spec=pltpu.PrefetchScalarGridSpec(
            num_scalar_prefetch=1,
            grid=(NA_HEADS, b),
            in_specs=[blk(0), blk(1), blk(2), cache_spec, cache_spec, gain, gain],
            out_specs=pl.BlockSpec((1, t, hd), lambda h, i, rb: (i, 0, h)),
            scratch_shapes=[pltpu.VMEM((t, hd), BF16)] * 3
            + [pltpu.VMEM((2, NA_BIAS_ROWS + 1, GRID_W, 2 * GRID_W), F32)],
        ),
        compiler_params=_params(("arbitrary", "arbitrary"), 40 << 20),
        name="na_attention",
    )(rel_bias.astype(F32).reshape(-1), p, p, p, cache_k, cache_v, q_gain, k_gain)


FNET_BLOCK = 16


def _fnet_tables(n1, n2):
    t_len = n1 * n2
    tb = FNET_BLOCK
    eye = np.eye(tb)
    k1 = np.arange(n1)
    ang1 = (2.0 * np.pi / n1) * ((k1[:, None] * k1[None, :]) % n1)
    f1 = np.concatenate([np.cos(ang1), -np.sin(ang1)], axis=0) / math.sqrt(t_len)
    rows_mat = np.kron(f1, eye)
    t2 = np.arange(n2)
    tw = (2.0 * np.pi / t_len) * (k1[:, None] * t2[None, :])
    tw = tw.reshape(n1, n2 // tb, tb).transpose(1, 0, 2).reshape(n2 // tb, n1 * tb, 1)
    c = np.arange(FNET_GROUP_DIM)
    ang3 = (2.0 * np.pi / FNET_GROUP_DIM) * ((c[:, None] * c[None, :]) % FNET_GROUP_DIM)
    c3, s3 = np.cos(ang3), np.sin(ang3)
    chan = np.block([[c3, -s3], [s3, c3]]) / math.sqrt(FNET_GROUP_DIM)
    ang2 = (2.0 * np.pi / n2) * ((t2[:, None] * t2[None, :]) % n2)
    cols_re = np.einsum("pq,kt->kpqt", eye, np.cos(ang2)).reshape(n2 * tb, tb * n2)
    cols_im = np.einsum("pq,kt->kpqt", eye, np.sin(ang2)).reshape(n2 * tb, tb * n2)
    f32 = lambda x: jnp.asarray(x, dtype=F32)
    bf = lambda x: f32(x).astype(BF16)
    return bf(rows_mat), f32(np.cos(tw)), f32(np.sin(tw)), bf(chan), bf(cols_re), bf(cols_im)


def _fnet_rows_kernel(x_ref, mat_ref, twc_ref, tws_ref, chan_ref, vr_ref, vi_ref):
    _, n1, tb, width = x_ref.shape
    rows = n1 * tb
    gd = FNET_GROUP_DIM
    x = x_ref[0].reshape(rows, width).astype(BF16)
    h = jnp.dot(mat_ref[...], x, preferred_element_type=F32)
    hr, hi = h[:rows], h[rows:]
    c, s = twc_ref[0], tws_ref[0]
    gr = (hr * c + hi * s).astype(BF16)
    gi = (hi * c - hr * s).astype(BF16)
    stacked = jnp.concatenate(
        [jnp.concatenate([gr[:, g * gd:(g + 1) * gd], gi[:, g * gd:(g + 1) * gd]], axis=1)
         for g in range(FNET_GROUPS)], axis=0)
    v = jnp.dot(stacked, chan_ref[...], preferred_element_type=F32)
    for g in range(FNET_GROUPS):
        part = v[g * rows:(g + 1) * rows]
        vr_ref[0, :, :, g * gd:(g + 1) * gd] = part[:, :gd].reshape(n1, tb, gd).astype(vr_ref.dtype)
        vi_ref[0, :, :, g * gd:(g + 1) * gd] = part[:, gd:].reshape(n1, tb, gd).astype(vi_ref.dtype)


def _fnet_cols_kernel(vr_ref, vi_ref, re_ref, im_ref, o_ref):
    _, kb, n2, width = vr_ref.shape
    vr = vr_ref[0].reshape(kb * n2, width)
    vi = vi_ref[0].reshape(kb * n2, width)
    y = (jnp.dot(re_ref[...], vr, preferred_element_type=F32)
         + jnp.dot(im_ref[...], vi, preferred_element_type=F32))
    o_ref[0] = y.reshape(n2, kb, width).astype(o_ref.dtype)


def fourier_mix(p, n1, n2):
    b, t, width = p.shape
    tb = FNET_BLOCK
    w = FNET_WIDTH
    rows_mat, twc, tws, chan, cols_re, cols_im = _fnet_tables(n1, n2)
    const = lambda shape: pl.BlockSpec(shape, lambda i, s: (0,) * len(shape))
    tw_spec = pl.BlockSpec((1, n1 * tb, 1), lambda i, s: (s, 0, 0))
    v_shape = jax.ShapeDtypeStruct((b, n1, n2, w), BF16)
    v_spec = pl.BlockSpec((1, n1, tb, w), lambda i, s: (i, 0, s, 0))
    isz = p.dtype.itemsize
    vmem = (2 * n1 * tb * w * isz + 2 * rows_mat.size * 2 + 4 * n1 * tb * w * 2
            + 6 * n1 * tb * w * 4 + VMEM_SLACK)
    vr, vi = pl.pallas_call(
        _fnet_rows_kernel,
        out_shape=(v_shape, v_shape),
        grid=(b, n2 // tb),
        in_specs=[pl.BlockSpec((1, n1, tb, w), lambda i, s: (i, 0, s, 0)),
                  const(rows_mat.shape), tw_spec, tw_spec, const(chan.shape)],
        out_specs=(v_spec, v_spec),
        compiler_params=_params(("arbitrary", "arbitrary"), vmem),
        name="fnet_rows",
    )(p.reshape(b, n1, n2, width), rows_mat, twc, tws, chan)
    v_in = pl.BlockSpec((1, tb, n2, w), lambda i, s: (i, s, 0, 0))
    vmem = 2 * 2 * tb * n2 * w * 2 + 2 * 2 * cols_re.size * 2 + 2 * n2 * tb * w * 2 + 3 * n2 * tb * w * 4 + VMEM_SLACK
    out = pl.pallas_call(
        _fnet_cols_kernel,
        out_shape=jax.ShapeDtypeStruct((b, n2, n1, w), BF16),
        grid=(b, n1 // tb),
        in_specs=[v_in, v_in, const(cols_re.shape), const(cols_im.shape)],
        out_specs=pl.BlockSpec((1, n2, tb, w), lambda i, s: (i, 0, s, 0)),
        compiler_params=_params(("arbitrary", "arbitrary"), vmem),
        name="fnet_cols",
    )(vr, vi, cols_re, cols_im)
    return out.reshape(b, t, w)


def _even_out_kernel(f_ref, a_ref, wf_ref, wa_ref, x_ref, gate_ref, o_ref):
    y = (jnp.dot(f_ref[...], wf_ref[...], preferred_element_type=F32)
         + jnp.dot(a_ref[...], wa_ref[...], preferred_element_type=F32))
    o_ref[...] = x_ref[...] + gate_ref[0] * y


def even_out(f, a, w, x, gate, rows_per_mod, tm, tn):
    m, d = x.shape
    kf = f.shape[1]
    vmem = 2 * 2 * tm * kf * 2 + 2 * 2 * kf * tn * 2 + 4 * tm * tn * 4 + 2 * tm * tn * 4 + VMEM_SLACK
    return pl.pallas_call(
        _even_out_kernel,
        out_shape=jax.ShapeDtypeStruct((m, d), F32),
        grid=(d // tn, m // tm),
        in_specs=[
            pl.BlockSpec((tm, kf), lambda j, i: (i, 0)),
            pl.BlockSpec((tm, kf), lambda j, i: (i, 0)),
            pl.BlockSpec((kf, tn), lambda j, i: (0, j)),
            pl.BlockSpec((kf, tn), lambda j, i: (1, j)),
            pl.BlockSpec((tm, tn), lambda j, i: (i, j)),
            pl.BlockSpec((1, 1, tn), lambda j, i: ((i * tm) // rows_per_mod, 0, j)),
        ],
        out_specs=pl.BlockSpec((tm, tn), lambda j, i: (i, j)),
        compiler_params=_params(("arbitrary", "arbitrary"), vmem),
        name="even_out",
    )(f, a, w, w, x, gate)


def _gated_group_norm(o, g):
    o = o.astype(F32)
    oc = o - jnp.mean(o, axis=-1, keepdims=True)
    gn = oc * lax.rsqrt(jnp.mean(oc * oc, axis=-1, keepdims=True) + EPS)
    hg = 0.5 * g
    return (hg + hg * jnp.tanh(hg)).astype(F32) * gn


def _odd_out_kernel(of_ref, ob_ref, gf_ref, gb_ref, w_ref, x_ref, gate_ref, o_ref):
    acc = None
    for h in range(RET_HEADS):
        cs = slice(h * RET_V_DIM, (h + 1) * RET_V_DIM)
        y = (_gated_group_norm(of_ref[:, cs], gf_ref[:, cs])
             + _gated_group_norm(ob_ref[:, cs], gb_ref[:, cs])).astype(BF16)
        part = jnp.dot(y, w_ref[cs, :], preferred_element_type=F32)
        acc = part if acc is None else acc + part
    o_ref[...] = x_ref[...] + gate_ref[0] * acc


def odd_out(o_f, o_b, p, w, x, gate, rows_per_mod, tm):
    m, d = x.shape
    k = w.shape[0]
    gcol = (2 * RET_QK_WIDTH + RET_V_WIDTH) // RET_V_WIDTH
    vmem = (2 * 4 * tm * k * 2 + k * d * 2 + 4 * tm * d * 4 + 2 * tm * d * 4
            + 6 * tm * RET_V_DIM * 4 + VMEM_SLACK)
    act = lambda c: pl.BlockSpec((tm, k), lambda i, c=c: (i, c))
    return pl.pallas_call(
        _odd_out_kernel,
        out_shape=jax.ShapeDtypeStruct((m, d), F32),
        grid=(m // tm,),
        in_specs=[
            act(0), act(0), act(gcol), act(gcol + 1),
            pl.BlockSpec((k, d), lambda i: (0, 0), pipeline_mode=pl.Buffered(1)),
            pl.BlockSpec((tm, d), lambda i: (i, 0)),
            pl.BlockSpec((1, 1, d), lambda i: ((i * tm) // rows_per_mod, 0, 0)),
        ],
        out_specs=pl.BlockSpec((tm, d), lambda i: (i, 0)),
        compiler_params=_params(("arbitrary",), vmem),
        name="odd_out",
    )(o_f, o_b, p, p, w, x, gate)


def _ffn_kernel(x_ref, sh_ref, sc_ref, gate_ref, g_ref, wg_ref, wu_ref, wd_ref, o_ref, h_ref):
    j = pl.program_id(1)

    @pl.when(j == 0)
    def _():
        _modulate_into(h_ref, x_ref, g_ref, sh_ref, sc_ref)
        o_ref[...] = jnp.zeros_like(o_ref)

    h = h_ref[...]
    a = jnp.dot(h, wg_ref[0], preferred_element_type=F32)
    u = jnp.dot(h, wu_ref[0], preferred_element_type=F32)
    act = (_silu(a) * u).astype(BF16)
    o_ref[...] += jnp.dot(act, wd_ref[0], preferred_element_type=F32)

    @pl.when(j == pl.num_programs(1) - 1)
    def _():
        o_ref[...] = x_ref[...] + gate_ref[0] * o_ref[...]


def ffn_block(x, shift, scale, gate, g, w_gate, w_up, w_down, layer, rows_per_mod, tm, tf):
    m, d = x.shape
    f = w_gate.shape[2]
    vmem = 4 * tm * d * 4 + tm * d * 2 + 2 * 3 * d * tf * 2 + 4 * tm * tf * 4 + tm * d * 4 + VMEM_SLACK
    mod_idx = lambda i, j: ((i * tm) // rows_per_mod, 0, 0)
    mod = pl.BlockSpec((1, 1, d), mod_idx)
    return pl.pallas_call(
        _ffn_kernel,
        out_shape=jax.ShapeDtypeStruct((m, d), F32),
        grid=(m // tm, f // tf),
        in_specs=[
            pl.BlockSpec((tm, d), lambda i, j: (i, 0)),
            mod, mod, mod,
            pl.BlockSpec((1, d), lambda i, j: (0, 0)),
            pl.BlockSpec((1, d, tf), lambda i, j: (layer, 0, j)),
            pl.BlockSpec((1, d, tf), lambda i, j: (layer, 0, j)),
            pl.BlockSpec((1, tf, d), lambda i, j: (layer, j, 0)),
        ],
        out_specs=pl.BlockSpec((tm, d), lambda i, j: (i, 0)),
        scratch_shapes=[pltpu.VMEM((tm, d), BF16)],
        compiler_params=_params(("arbitrary", "arbitrary"), vmem),
        name="ffn_block",
    )(x, shift, scale, gate, g, w_gate, w_up, w_down)


def _retention_kernel(lg_ref, *refs, latent, n_blocks):
    if latent:
        (qf_ref, kf_ref, vf_ref, qb_ref, kb_ref, vb_ref, s0_ref,
         of_ref, ob_ref, s_ref, dec_ref, qd_ref, kd_ref) = refs
    else:
        (qf_ref, kf_ref, vf_ref, qb_ref, kb_ref, vb_ref,
         of_ref, ob_ref, sfin_ref, s_ref, dec_ref, qd_ref, kd_ref) = refs
    h0 = pl.program_id(0) * RET_HEADS_PER_STEP
    i = pl.program_id(2)
    c = RET_BLOCK
    dk, dv = RET_QK_DIM, RET_V_DIM
    scale = RET_QK_DIM ** -0.5
    scans = [(d, hh) for d in range(2) for hh in range(RET_HEADS_PER_STEP)]

    @pl.when((pl.program_id(1) == 0) & (i == 0))
    def _():
        row = lax.broadcasted_iota(jnp.int32, (c, c), 0).astype(F32)
        col = lax.broadcasted_iota(jnp.int32, (c, c), 1).astype(F32)
        t_idx = lax.broadcasted_iota(jnp.int32, (c, 1), 0).astype(F32)
        for d, hh in scans:
            lg = lg_ref[d, h0 + hh]
            diff = row - col if d == 0 else col - row
            dec_ref[d, hh] = jnp.where(diff >= 0, jnp.exp(lg * jnp.maximum(diff, 0.0)), 0.0) * scale
            pos = t_idx if d == 0 else (c - 1.0) - t_idx
            qd_ref[d, hh] = jnp.exp(lg * (pos + 1.0))
            kd_ref[d, hh] = jnp.exp(lg * ((c - 1.0) - pos)) * scale

    @pl.when(i == 0)
    def _():
        for d, hh in scans:
            s_ref[d, hh] = (_pair_order_rows(s0_ref[0, 0, d, hh]) if latent
                            else jnp.zeros((dk, dv), F32))

    nt = (((1,), (1,)), ((), ()))
    tn = (((0,), (0,)), ((), ()))
    chunk_refs = ((qf_ref, kf_ref, vf_ref, of_ref), (qb_ref, kb_ref, vb_ref, ob_ref))
    for d, hh in scans:
        q_ref, k_ref, v_ref, o_ref = chunk_refs[d]
        q = q_ref[0, :, hh * dk:(hh + 1) * dk]
        k = k_ref[0, :, hh * dk:(hh + 1) * dk]
        v = v_ref[0, :, hh * dv:(hh + 1) * dv]
        a = lax.dot_general(q, k, nt, preferred_element_type=F32)
        inner_w = (a * dec_ref[d, hh]).astype(BF16)
        q_dec = (q.astype(F32) * qd_ref[d, hh]).astype(BF16)
        k_dec = (k.astype(F32) * kd_ref[d, hh]).astype(BF16)
        s_old = s_ref[d, hh]
        o = (jnp.dot(inner_w, v, preferred_element_type=F32)
             + jnp.dot(q_dec, s_old.astype(BF16), preferred_element_type=F32))
        s_new = (s_old * jnp.exp(lg_ref[d, h0 + hh] * c)
                 + lax.dot_general(k_dec, v, tn, preferred_element_type=F32))
        s_ref[d, hh] = s_new
        o_ref[0, :, hh * dv:(hh + 1) * dv] = o.astype(o_ref.dtype)
        if not latent:
            @pl.when(i == n_blocks - 1)
            def _(d=d, hh=hh, s_new=s_new):
                sfin_ref[0, 0, d, hh] = _pair_order_rows(s_new)


def retention(p, log_gamma, state0, layer_j):
    b, t, _ = p.shape
    latent = state0 is not None
    c = RET_BLOCK
    nb = t // c
    hps = RET_HEADS_PER_STEP
    dk, dv, nh = hps * RET_QK_DIM, hps * RET_V_DIM, RET_HEADS // hps

    def chunk_specs(blk_of):
        return [
            pl.BlockSpec((1, c, dk), lambda h, bi, i, lg: (bi, blk_of(i), h)),
            pl.BlockSpec((1, c, dk), lambda h, bi, i, lg: (bi, blk_of(i), nh + h)),
            pl.BlockSpec((1, c, dv), lambda h, bi, i, lg: (bi, blk_of(i), nh + h)),
        ]

    fwd = lambda i: i
    bwd = lambda i: nb - 1 - i
    in_specs = chunk_specs(fwd) + chunk_specs(bwd)
    args = [p] * 6
    o_shape = jax.ShapeDtypeStruct((b, t, RET_V_WIDTH), BF16)
    out_shape = [o_shape, o_shape]
    out_specs = [pl.BlockSpec((1, c, dv), lambda h, bi, i, lg: (bi, fwd(i), h)),
                 pl.BlockSpec((1, c, dv), lambda h, bi, i, lg: (bi, bwd(i), h))]
    state_dims = (RET_QK_DIM, RET_V_DIM)
    state_block = (1, 1, 2, hps) + state_dims
    if latent:
        in_specs.append(pl.BlockSpec(state_block, lambda h, bi, i, lg: (bi, layer_j, 0, h, 0, 0)))
        args.append(state0)
    else:
        out_shape.append(jax.ShapeDtypeStruct((b, 1, 2, RET_HEADS) + state_dims, F32))
        out_specs.append(pl.BlockSpec(state_block, lambda h, bi, i, lg: (bi, 0, 0, h, 0, 0)))
    return pl.pallas_call(
        functools.partial(_retention_kernel, latent=latent, n_blocks=nb),
        out_shape=tuple(out_shape),
        grid_spec=pltpu.PrefetchScalarGridSpec(
            num_scalar_prefetch=1,
            grid=(nh, b, nb),
            in_specs=in_specs,
            out_specs=tuple(out_specs),
            scratch_shapes=[pltpu.VMEM((2, hps) + state_dims, F32), pltpu.VMEM((2, hps, c, c), F32),
                            pltpu.VMEM((2, hps, c, 1), F32), pltpu.VMEM((2, hps, c, 1), F32)],
        ),
        compiler_params=_params(("arbitrary",) * 3, 40 << 20),
        name="retention",
    )(log_gamma, *args)


def _trunk(x, mods, ctx_k, ctx_v, ctx_state, wts, p_dtype):
    b, t, d = x.shape
    is_ctx = ctx_k is None
    nbm = mods.shape[1]
    rows_per_mod = (b * t) // nbm
    x2 = x.reshape(b * t, d)
    new_k = new_v = new_s = None
    for i in range(DEPTH):
        mod = [mods[i, :, k][:, None, :] for k in range(6)]
        j = i // 2
        if i % 2 == 0:
            p = modproj(x2, mod[0], mod[1], wts["norm_g"][i, 0][None], wts["even_w_in"][j],
                        rows_per_mod, p_dtype, tm=1024, tn=1024).reshape(b, t, EVEN_IN)
            qg, kg = wts["even_q_norm"][j][None], wts["even_k_norm"][j][None]
            if is_ctx:
                attn, new_k, new_v = ctx_attention(p, qg, kg)
            else:
                attn = na_attention(p, ctx_k, ctx_v, j, wts["na_rel_bias"][j], qg, kg)
            n2 = GRID_W if not is_ctx else math.isqrt(t)
            fm = fourier_mix(p, t // n2, n2)
            x2 = even_out(fm.reshape(b * t, FNET_WIDTH), attn.reshape(b * t, NA_WIDTH),
                          wts["even_w_out"][j], x2, mod[2], rows_per_mod, tm=1024, tn=1024)
        else:
            rope = (t, 2 * RET_QK_WIDTH) if not is_ctx else None
            p = modproj(x2, mod[0], mod[1], wts["norm_g"][i, 0][None], wts["odd_w_in"][j],
                        rows_per_mod, BF16, tm=1024, tn=1024, rope=rope)
            lg = jax.nn.log_sigmoid(wts["ret_decay_logit"][j].astype(F32))
            if is_ctx:
                o_f, o_b, new_s = retention(p.reshape(b, t, ODD_IN), lg, None, j)
            else:
                o_f, o_b = retention(p.reshape(b, t, ODD_IN), lg, ctx_state, j)
            x2 = odd_out(o_f.reshape(b * t, RET_V_WIDTH), o_b.reshape(b * t, RET_V_WIDTH), p,
                         wts["odd_w_out"][j], x2, mod[2], rows_per_mod, tm=256)
        x2 = ffn_block(x2, mod[3], mod[4], mod[5], wts["norm_g"][i, 1][None], wts["ffn_w_gate"],
                       wts["ffn_w_up"], wts["ffn_w_down"], i, rows_per_mod, tm=512, tf=512)
    return x2.reshape(b, t, d), new_k, new_v, new_s


def kernel(x_prompt, x_sample, cache_k, cache_v, state_ret, c, c_ctx, ada_w, ada_b, norm_g,
           even_w_in, even_q_norm, even_k_norm, na_rel_bias, even_w_out, odd_w_in, ret_decay_logit,
           odd_w_out, ffn_w_gate, ffn_w_up, ffn_w_down):
    nb_lat = c.shape[0]
    cond = jnp.concatenate(
        [c, c_ctx[None, :], jnp.zeros((MOD_ROWS - nb_lat - 1, D_MODEL), F32)], axis=0)
    mods = adaln_all(cond, ada_w, ada_b).reshape(DEPTH, MOD_ROWS, 6, D_MODEL)
    wts = dict(
        norm_g=norm_g, even_q_norm=even_q_norm, even_k_norm=even_k_norm, na_rel_bias=na_rel_bias,
        ret_decay_logit=ret_decay_logit,
        even_w_in=even_w_in.astype(BF16), even_w_out=even_w_out.astype(BF16),
        odd_w_in=odd_w_prep(odd_w_in), odd_w_out=odd_w_out.astype(BF16),
        ffn_w_gate=ffn_w_gate.astype(BF16), ffn_w_up=ffn_w_up.astype(BF16),
        ffn_w_down=ffn_w_down.astype(BF16),
    )
    y_prompt, new_k, new_v, new_s = _trunk(
        x_prompt, mods[:, nb_lat:nb_lat + 1], None, None, None, wts, F32)
    y_sample, _, _, _ = _trunk(x_sample, mods[:, :nb_lat], cache_k, cache_v, state_ret, wts, BF16)
    return (y_prompt, y_sample, new_k, new_v, new_s)
```
